```python
import math
import jax, jax.numpy as jnp
from jax import lax
import numpy as np

D_MODEL = 1024
BATCH = 1
SEQ = 16384
DEPTH = 4

HEAD_DIM = 64
GRID_W = 64
NORM_EPS = 1e-6
NA_HEADS = 4
NA_KH = 8
NA_KW = 16
SWA_Q_HEADS = 8
SWA_KV_HEADS = 2
SWA_WINDOW = 128
SWA_BLOCK = 128
ROPE_THETA = 10000.0
GDN_HEADS = 4
GDN_DK = 64
GDN_DV = 64
GDN_CONV = 3
GDN_CHUNK = 64
D_FF = 2816
FFN_CONV = 3

NA_WIDTH = NA_HEADS * HEAD_DIM
SWA_Q_WIDTH = SWA_Q_HEADS * HEAD_DIM
SWA_KV_WIDTH = SWA_KV_HEADS * HEAD_DIM
GDN_QK_WIDTH = GDN_HEADS * GDN_DK
GDN_V_WIDTH = GDN_HEADS * GDN_DV
N_BRANCH = 3
IN_SPLITS = (NA_WIDTH, NA_WIDTH, NA_WIDTH,
             SWA_Q_WIDTH, SWA_KV_WIDTH, SWA_KV_WIDTH,
             2 * GDN_QK_WIDTH + GDN_V_WIDTH, GDN_V_WIDTH, 2 * GDN_HEADS, 2 * GDN_HEADS,
             N_BRANCH * D_MODEL)
IN_WIDTH = 3 * NA_WIDTH + SWA_Q_WIDTH + 2 * SWA_KV_WIDTH + 2 * GDN_QK_WIDTH + 2 * GDN_V_WIDTH + 4 * GDN_HEADS + N_BRANCH * D_MODEL

kernel_name = "hybrid_na_swa_gdn_encoder"


def rms_norm(x, g):
    xf = x.astype(jnp.float32)
    y = xf * lax.rsqrt(jnp.mean(xf * xf, axis=-1, keepdims=True) + NORM_EPS)
    return (y * g.astype(jnp.float32)).astype(x.dtype)


def l2_norm(x):
    xf = x.astype(jnp.float32)
    return (xf * lax.rsqrt(jnp.sum(xf * xf, axis=-1, keepdims=True) + NORM_EPS)).astype(x.dtype)


def split_heads(t, n_heads):
    return t.reshape(t.shape[0], t.shape[1], n_heads, -1)


def depthwise_conv_centred(x, w):
    k_size = w.shape[0]
    half = k_size // 2
    length = x.shape[1]
    xp = jnp.pad(x, ((0, 0), (half, half), (0, 0)))
    y = xp[:, 0:length] * w[0]
    for i in range(1, k_size):
        y = y + xp[:, i:i + length] * w[i]
    return y


def rope_tables(length):
    inv = 1.0 / (ROPE_THETA ** (jnp.arange(0, HEAD_DIM, 2, dtype=jnp.float32) / HEAD_DIM))
    ang = jnp.arange(length, dtype=jnp.float32)[:, None] * inv[None, :]
    return jnp.cos(ang), jnp.sin(ang)


def apply_rope(x, cos, sin):
    xf = x.astype(jnp.float32)
    half = HEAD_DIM // 2
    x1, x2 = xf[..., :half], xf[..., half:]
    c, s = cos[None, :, None, :], sin[None, :, None, :]
    return jnp.concatenate([x1 * c - x2 * s, x2 * c + x1 * s], axis=-1).astype(x.dtype)


def neighborhood_attention_2d(q, k, v, rpb):
    B, L, H, d = q.shape
    rows = L // GRID_W
    kh = min(NA_KH, rows)
    r = jnp.arange(rows)
    row_start = jnp.clip(r - kh // 2, 0, rows - kh)
    row_idx = row_start[:, None] + jnp.arange(kh)
    c = jnp.arange(GRID_W)
    col_start = jnp.clip(c - NA_KW // 2, 0, GRID_W - NA_KW)
    col_in = (c[None, :] >= col_start[:, None]) & (c[None, :] < col_start[:, None] + NA_KW)
    qg = q.reshape(B, rows, GRID_W, H, d)
    kg = k.reshape(B, rows, GRID_W, H, d)[:, row_idx]
    vg = v.reshape(B, rows, GRID_W, H, d)[:, row_idx]
    s = jnp.einsum('brqhd,brjkhd->brhqjk', qg, kg).astype(jnp.float32) * (HEAD_DIM ** -0.5)
    dr = row_idx - r[:, None] + (NA_KH - 1)
    dc = jnp.clip(c[None, :] - c[:, None] + (NA_KW - 1), 0, 2 * NA_KW - 2)
    bias = rpb.astype(jnp.float32)[:, dr[:, None, :, None], dc[None, :, None, :]]
    s = jnp.where(col_in[:, None, :], s + jnp.moveaxis(bias, 0, 1), -jnp.inf)
    p = jax.nn.softmax(s, axis=(-2, -1)).astype(v.dtype)
    return jnp.einsum('brhqjk,brjkhd->brqhd', p, vg).reshape(B, L, H, d)


def sliding_window_gqa(q, k, v, sink):
    B, L, Hq, d = q.shape
    Hkv = k.shape[2]
    G = Hq // Hkv
    blk = SWA_BLOCK
    nb = L // blk
    def key_blocks(t):
        tp = jnp.pad(t, ((0, 0), (blk, blk), (0, 0), (0, 0))).reshape(B, nb + 2, blk, Hkv, d)
        return jnp.concatenate([tp[:, :-2], tp[:, 1:-1], tp[:, 2:]], axis=2)
    kb, vb = key_blocks(k), key_blocks(v)
    qb = q.reshape(B, nb, blk, Hkv, G, d)
    s = jnp.einsum('bnqhgd,bnkhd->bnhgqk', qb, kb).astype(jnp.float32) * (HEAD_DIM ** -0.5)
    i = jnp.arange(blk)[:, None]
    j = jnp.arange(3 * blk)[None, :]
    kpos = jnp.arange(nb)[:, None, None] * blk - blk + j
    valid = (jnp.abs(j - blk - i) <= SWA_WINDOW) & (kpos >= 0) & (kpos < L)
    s = jnp.where(valid[None, :, None, None], s, -jnp.inf)
    sink_col = jnp.broadcast_to(sink.astype(jnp.float32).reshape(Hkv, G)[None, None, :, :, None, None],
                                s.shape[:-1] + (1,))
    p = jax.nn.softmax(jnp.concatenate([s, sink_col], axis=-1), axis=-1)[..., :-1].astype(v.dtype)
    return jnp.einsum('bnhgqk,bnkhd->bnqhgd', p, vb).reshape(B, L, Hq, d)


def gated_delta_rule_chunked(q, k, v, beta, g):
    out_dtype = v.dtype
    B, L, H, dk = q.shape
    dv = v.shape[-1]
    C = GDN_CHUNK
    n = L // C
    def chunks(t):
        t = jnp.moveaxis(t.astype(jnp.float32), 2, 1)
        return t.reshape((B, H, n, C) + t.shape[3:])
    q, k, v, beta, g = chunks(q), chunks(k), chunks(v), chunks(beta), chunks(g)
    g = jnp.cumsum(g, axis=-1)
    incl = jnp.tril(jnp.ones((C, C), dtype=bool))
    strict = jnp.tril(jnp.ones((C, C), dtype=bool), -1)
    decay = jnp.exp(jnp.where(incl, g[..., :, None] - g[..., None, :], -jnp.inf))
    k_beta = k * beta[..., None]
    lower = jnp.where(strict, jnp.einsum('bhnid,bhnjd->bhnij', k_beta, k) * decay, 0.0)
    rhs = jnp.concatenate([v * beta[..., None], k_beta * jnp.exp(g)[..., None]], axis=-1)
    sol = lax.linalg.triangular_solve(lower + jnp.eye(C, dtype=jnp.float32), rhs,
                                      left_side=True, lower=True, unit_diagonal=True)
    u, w = sol[..., :dv], sol[..., dv:]
    qk = jnp.einsum('bhnid,bhnjd->bhnij', q, k) * decay
    q_dec = q * jnp.exp(g)[..., None]
    k_tail = k * jnp.exp(g[..., -1:] - g)[..., None]
    chunk_dec = jnp.exp(g[..., -1])
    def step(S, xs):
        qk_c, q_c, w_c, u_c, k_c, d_c = xs
        v_new = u_c - jnp.einsum('bhck,bhkv->bhcv', w_c, S)
        o_c = jnp.einsum('bhck,bhkv->bhcv', q_c, S) + jnp.einsum('bhij,bhjv->bhiv', qk_c, v_new)
        S = S * d_c[..., None, None] + jnp.einsum('bhck,bhcv->bhkv', k_c, v_new)
        return S, o_c
    xs = tuple(jnp.moveaxis(t, 2, 0) for t in (qk, q_dec, w, u, k_tail, chunk_dec))
    _, o = lax.scan(step, jnp.zeros((B, H, dk, dv), jnp.float32), xs)
    o = jnp.moveaxis(o, 0, 2).reshape(B, H, L, dv)
    return jnp.moveaxis(o, 1, 2).astype(out_dtype)


def gated_deltanet_bidir(qkv, z, beta_logit, a_logit, conv_w, a_log, dt_bias, norm_g):
    B, L, _ = qkv.shape
    qkv = jax.nn.silu(depthwise_conv_centred(qkv, conv_w))
    q, k, v = jnp.split(qkv, [GDN_QK_WIDTH, 2 * GDN_QK_WIDTH], axis=-1)
    q = l2_norm(q.reshape(B, L, GDN_HEADS, GDN_DK)) * (GDN_DK ** -0.5)
    k = l2_norm(k.reshape(B, L, GDN_HEADS, GDN_DK))
    v = v.reshape(B, L, GDN_HEADS, GDN_DV)
    beta = jax.nn.sigmoid(beta_logit.astype(jnp.float32)).reshape(B, L, 2, GDN_HEADS)
    g = -jnp.exp(a_log.astype(jnp.float32)) * jax.nn.softplus(
        a_logit.astype(jnp.float32).reshape(B, L, 2, GDN_HEADS) + dt_bias.astype(jnp.float32))
    flip = lambda t: jnp.flip(t, axis=1)
    o_fwd = gated_delta_rule_chunked(q, k, v, beta[:, :, 0], g[:, :, 0])
    o_bwd = flip(gated_delta_rule_chunked(flip(q), flip(k), flip(v), flip(beta[:, :, 1]), flip(g[:, :, 1])))
    o = rms_norm(o_fwd + o_bwd, norm_g) * jax.nn.silu(z.reshape(B, L, GDN_HEADS, GDN_DV))
    return o.reshape(B, L, GDN_V_WIDTH)


def setup_inputs(seed: int = 0) -> dict:
    key = jax.random.key(seed)
    ks = jax.random.split(key, 20)
    f32 = jnp.float32
    nrm = lambda k, shape, scale: jax.random.normal(k, shape, f32) * scale
    gain = lambda k, shape: 1.0 + 0.02 * jax.random.normal(k, shape, f32)
    dt = jnp.exp(jax.random.uniform(ks[8], (DEPTH, 2, GDN_HEADS), f32, math.log(1e-3), math.log(1e-1)))
    return {
        "x": jax.random.normal(ks[0], (BATCH, SEQ, D_MODEL), f32),
        "attn_norm": gain(ks[1], (DEPTH, D_MODEL)),
        "w_in": nrm(ks[2], (DEPTH, D_MODEL, IN_WIDTH), D_MODEL ** -0.5),
        "qk_norm": gain(ks[3], (DEPTH, 4, HEAD_DIM)),
        "na_rpb": nrm(ks[4], (DEPTH, NA_HEADS, 2 * NA_KH - 1, 2 * NA_KW - 1), 0.1),
        "swa_sink": nrm(ks[5], (DEPTH, SWA_Q_HEADS), 0.5),
        "gdn_conv_w": nrm(ks[6], (DEPTH, GDN_CONV, 2 * GDN_QK_WIDTH + GDN_V_WIDTH), GDN_CONV ** -0.5),
        "gdn_a_log": jnp.log(jax.random.uniform(ks[7], (DEPTH, 2, GDN_HEADS), f32, 1.0, 16.0)),
        "gdn_dt_bias": dt + jnp.log(-jnp.expm1(-dt)),
        "gdn_norm": gain(ks[9], (DEPTH, GDN_DV)),
        "w_branch_na": nrm(ks[10], (DEPTH, NA_WIDTH, D_MODEL), NA_WIDTH ** -0.5),
        "w_branch_swa": nrm(ks[11], (DEPTH, SWA_Q_WIDTH, D_MODEL), SWA_Q_WIDTH ** -0.5),
        "w_branch_gdn": nrm(ks[12], (DEPTH, GDN_V_WIDTH, D_MODEL), GDN_V_WIDTH ** -0.5),
        "w_out": nrm(ks[13], (DEPTH, D_MODEL, D_MODEL), D_MODEL ** -0.5),
        "ffn_norm": gain(ks[14], (DEPTH, D_MODEL)),
        "w_up": nrm(ks[15], (DEPTH, D_MODEL, 2 * D_FF), D_MODEL ** -0.5),
        "ffn_conv_w": nrm(ks[16], (DEPTH, FFN_CONV, 2 * D_FF), FFN_CONV ** -0.5),
        "ffn_conv_b": nrm(ks[17], (DEPTH, 2 * D_FF), 0.02),
        "w_down": nrm(ks[18], (DEPTH, D_FF, D_MODEL), D_FF ** -0.5),
    }


def reference(x, attn_norm, w_in, qk_norm, na_rpb, swa_sink, gdn_conv_w, gdn_a_log, gdn_dt_bias, gdn_norm,
              w_branch_na, w_branch_swa, w_branch_gdn, w_out, ffn_norm, w_up, ffn_conv_w, ffn_conv_b, w_down):
    B, L, _ = x.shape
    cos, sin = rope_tables(L)
    splits = np.cumsum(IN_SPLITS)[:-1].tolist()
    for l in range(DEPTH):
        h = rms_norm(x, attn_norm[l])
        (qa, ka, va, qs, ks_, vs, qkv_c, z_c, beta_c, a_c, gate_c) = jnp.split(h @ w_in[l], splits, axis=-1)
        qa = rms_norm(split_heads(qa, NA_HEADS), qk_norm[l, 0])
        ka = rms_norm(split_heads(ka, NA_HEADS), qk_norm[l, 1])
        y_na = neighborhood_attention_2d(qa, ka, split_heads(va, NA_HEADS), na_rpb[l]).reshape(B, L, NA_WIDTH)
        qs = apply_rope(rms_norm(split_heads(qs, SWA_Q_HEADS), qk_norm[l, 2]), cos, sin)
        ks_ = apply_rope(rms_norm(split_heads(ks_, SWA_KV_HEADS), qk_norm[l, 3]), cos, sin)
        y_swa = sliding_window_gqa(qs, ks_, split_heads(vs, SWA_KV_HEADS), swa_sink[l]).reshape(B, L, SWA_Q_WIDTH)
        y_gdn = gated_deltanet_bidir(qkv_c, z_c, beta_c, a_c, gdn_conv_w[l], gdn_a_log[l], gdn_dt_bias[l], gdn_norm[l])
        g_na, g_swa, g_gdn = jnp.split(jax.nn.sigmoid(gate_c), N_BRANCH, axis=-1)
        merged = (g_na * (y_na @ w_branch_na[l]) + g_swa * (y_swa @ w_branch_swa[l])
                  + g_gdn * (y_gdn @ w_branch_gdn[l]))
        x = x + merged @ w_out[l]
        h = rms_norm(x, ffn_norm[l])
        u = depthwise_conv_centred(h @ w_up[l], ffn_conv_w[l]) + ffn_conv_b[l]
        a, b = jnp.split(u, 2, axis=-1)
        x = x + (jax.nn.silu(a) * b) @ w_down[l]
    return x
```

```python
import functools

import numpy as np
import jax
import jax.numpy as jnp
from jax import lax
from jax.experimental import pallas as pl
from jax.experimental.pallas import tpu as pltpu

F32 = jnp.float32
BF16 = jnp.bfloat16

D_MODEL = 1024
HEAD_DIM = 64
GRID_W = 64
NORM_EPS = 1e-6
NA_HEADS = 4
NA_KH = 8
NA_KW = 16
SWA_Q_HEADS = 8
SWA_KV_HEADS = 2
SWA_BLOCK = 128
ROPE_THETA = 10000.0
GDN_HEADS = 4
GDN_CHUNK = 64
D_FF = 2816

LANES = 128
NEG_BIG = -1e30
VMEM_LIMIT = 56 * 1024 * 1024

SEG_QK = 1152
SEG_PLAIN = 1408
SEG_GATE = 3072
SEG_BA = 128
IN_PACKED = SEG_QK + SEG_PLAIN + SEG_GATE + SEG_BA

TM = 512
NA_ROWS = 8
GDN_G = 4
FFN_NC = 1408
HALO = 8


def _cparams(sem):
    return pltpu.CompilerParams(dimension_semantics=sem, vmem_limit_bytes=VMEM_LIMIT)


def _dot(a, b):
    return jnp.dot(a, b, preferred_element_type=F32)


def _sigmoid(x):
    return 1.0 / (1.0 + jnp.exp(-x))


def _group_sum(sq, bd):
    hi = sq.astype(BF16)
    lo = (sq - hi.astype(F32)).astype(BF16)
    return _dot(hi, bd) + _dot(lo, bd)


def _inproj_kernel(x_ref, g_ref, w_ref, gain_ref, cos_ref, sin_ref, bd_ref,
                   oqk_ref, oplain_ref, ogate_ref, oba_ref):
    x = x_ref[...]
    ms = jnp.mean(x * x, axis=-1, keepdims=True)
    h = (x * lax.rsqrt(ms + NORM_EPS) * g_ref[...]).astype(BF16)
    bd = bd_ref[...]
    lane = lax.broadcasted_iota(jnp.int32, (1, LANES), 1)
    first_half = (lane % HEAD_DIM) < (HEAD_DIM // 2)
    cos = cos_ref[...]
    sin = sin_ref[...]

    t = _dot(h, w_ref[:, 0:SEG_QK])
    for c in range(SEG_QK // LANES):
        tc = t[:, c * LANES:(c + 1) * LANES]
        ss = _group_sum(tc * tc, bd)
        y = tc * lax.rsqrt(ss * (1.0 / HEAD_DIM) + NORM_EPS) * gain_ref[:, c * LANES:(c + 1) * LANES]
        if c >= 4:
            rot = jnp.where(first_half, pltpu.roll(y, LANES - HEAD_DIM // 2, 1), pltpu.roll(y, HEAD_DIM // 2, 1))
            y = y * cos + rot * sin
        oqk_ref[:, c * LANES:(c + 1) * LANES] = y.astype(BF16)

    o = SEG_QK
    oplain_ref[...] = _dot(h, w_ref[:, o:o + SEG_PLAIN]).astype(BF16)
    o += SEG_PLAIN
    for c in range(3):
        tg = _dot(h, w_ref[:, o + c * D_MODEL:o + (c + 1) * D_MODEL])
        ogate_ref[:, c * D_MODEL:(c + 1) * D_MODEL] = _sigmoid(tg).astype(BF16)
    o += SEG_GATE
    oba_ref[...] = _dot(h, w_ref[:, o:o + SEG_BA])


def _inproj(x, g, w, gain, cos, sin, bd):
    L = x.shape[0]
    row = lambda w_: pl.BlockSpec((TM, w_), lambda i: (i, 0))
    const = lambda a: pl.BlockSpec(a.shape, lambda i: (0,) * a.ndim)
    return pl.pallas_call(
        _inproj_kernel,
        grid=(L // TM,),
        in_specs=[row(D_MODEL), const(g),
                  pl.BlockSpec(w.shape, lambda i: (0, 0), pipeline_mode=pl.Buffered(1)),
                  const(gain), row(LANES), row(LANES), const(bd)],
        out_specs=[row(SEG_QK), row(SEG_PLAIN), row(SEG_GATE), row(SEG_BA)],
        out_shape=[jax.ShapeDtypeStruct((L, SEG_QK), BF16),
                   jax.ShapeDtypeStruct((L, SEG_PLAIN), BF16),
                   jax.ShapeDtypeStruct((L, SEG_GATE), BF16),
                   jax.ShapeDtypeStruct((L, SEG_BA), F32)],
        compiler_params=_cparams(("parallel",)),
        name="inproj",
    )(x, g, w, gain, cos, sin, bd)


def _toeplitz_kernel(onehot_ref, rpb_ref, o_ref):
    r = rpb_ref[...]
    hi = r.astype(BF16)
    r1 = r - hi.astype(F32)
    mid = r1.astype(BF16)
    lo = (r1 - mid.astype(F32)).astype(BF16)
    oh = onehot_ref[...]
    o_ref[...] = _dot(oh, hi) + _dot(oh, mid) + _dot(oh, lo)


def _na_bias_tables(na_rpb):
    depth, H, nr, nc = na_rpb.shape
    qc = np.arange(GRID_W)[:, None]
    kc = np.arange(GRID_W)[None, :]
    dc = np.clip(kc - qc + (NA_KW - 1), 0, 2 * NA_KW - 2)
    onehot = np.zeros((GRID_W * GRID_W, 32), np.float32)
    onehot[np.arange(GRID_W * GRID_W), dc.reshape(-1)] = 1.0
    cols = depth * H * nr
    cols_p = -(-cols // LANES) * LANES
    rpb_t = jnp.pad(na_rpb.reshape(cols, nc).T, ((0, 32 - nc), (0, cols_p - cols)))
    w = pl.pallas_call(
        _toeplitz_kernel,
        out_shape=jax.ShapeDtypeStruct((GRID_W * GRID_W, cols_p), F32),
        name="na_bias",
    )(jnp.asarray(onehot, BF16), rpb_t)
    w = w[:, :cols].reshape(GRID_W, GRID_W, depth, H, nr)
    w = jnp.transpose(w, (2, 3, 4, 0, 1))
    col_start = np.clip(np.arange(GRID_W) - NA_KW // 2, 0, GRID_W - NA_KW)
    col_in = (kc >= col_start[:, None]) & (kc < col_start[:, None] + NA_KW)
    w = jnp.where(jnp.asarray(col_in), w, NEG_BIG)
    tabs = []
    for s in range(NA_KH):
        tiles = [w[:, :, j - s + NA_KH - 1] for j in range(NA_KH)]
        tabs.append(jnp.concatenate(tiles, axis=-1))
    return jnp.stack(tabs, axis=2)


def _na_kernel(q_ref, kp_ref, kc_ref, kn_ref, vp_ref, vc_ref, vn_ref, bias_ref, o_ref, kbuf, vbuf, *, rows):
    i = pl.program_id(0)
    blk = NA_ROWS * GRID_W
    kbuf[0:blk, :] = kp_ref[...]
    kbuf[blk:2 * blk, :] = kc_ref[...]
    kbuf[2 * blk:3 * blk, :] = kn_ref[...]
    vbuf[0:blk, :] = vp_ref[...]
    vbuf[blk:2 * blk, :] = vc_ref[...]
    vbuf[2 * blk:3 * blk, :] = vn_ref[...]
    win = NA_KH * GRID_W

    def body(r, carry):
        grow = i * NA_ROWS + r
        rstart = jnp.clip(grow - NA_KH // 2, 0, rows - NA_KH)
        shift = grow - rstart
        off = pl.multiple_of((rstart - (i - 1) * NA_ROWS) * GRID_W, GRID_W)
        kw = kbuf[pl.ds(off, win), :]
        vw = vbuf[pl.ds(off, win), :]
        q = q_ref[pl.ds(pl.multiple_of(r * GRID_W, GRID_W), GRID_W), :]
        outs = []
        for h in range(NA_HEADS):
            sl = slice(h * HEAD_DIM, (h + 1) * HEAD_DIM)
            s = lax.dot_general(q[:, sl], kw[:, sl], (((1,), (1,)), ((), ())), preferred_element_type=F32)
            s = s + bias_ref[h, shift]
            m = jnp.max(s, axis=-1, keepdims=True)
            p = jnp.exp(s - m)
            l = jnp.sum(p, axis=-1, keepdims=True)
            o = _dot(p.astype(BF16), vw[:, sl])
            outs.append(o / l)
        o_ref[pl.ds(pl.multiple_of(r * GRID_W, GRID_W), GRID_W), :] = jnp.concatenate(outs, axis=-1).astype(BF16)
        return carry

    lax.fori_loop(0, NA_ROWS, body, 0)


def _na_attention(qk, plain, bias):
    L = qk.shape[0]
    rows = L // GRID_W
    nblk = rows // NA_ROWS
    blk = NA_ROWS * GRID_W
    w = NA_HEADS * HEAD_DIM
    prev = lambda i: jnp.maximum(i - 1, 0)
    nxt = lambda i: jnp.minimum(i + 1, nblk - 1)
    spec = lambda f, c: pl.BlockSpec((blk, w), lambda i: (f(i), c))
    same = lambda i: i
    return pl.pallas_call(
        functools.partial(_na_kernel, rows=rows),
        grid=(nblk,),
        in_specs=[spec(same, 0),
                  spec(prev, 1), spec(same, 1), spec(nxt, 1),
                  spec(prev, 3), spec(same, 3), spec(nxt, 3),
                  pl.BlockSpec(bias.shape, lambda i: (0, 0, 0, 0))],
        out_specs=pl.BlockSpec((blk, w), lambda i: (i, 0)),
        out_shape=jax.ShapeDtypeStruct((L, w), BF16),
        scratch_shapes=[pltpu.VMEM((3 * blk, w), BF16), pltpu.VMEM((3 * blk, w), BF16)],
        compiler_params=_cparams(("parallel",)),
        name="na_attn",
    )(qk, qk, qk, qk, plain, plain, plain, bias)


def _swa_kernel(q_ref, kp_ref, kc_ref, kn_ref, vp_ref, vc_ref, vn_ref, sink_ref, o_ref, *, nblk):
    i = pl.program_id(0)
    B = SWA_BLOCK
    G = SWA_Q_HEADS // SWA_KV_HEADS
    k3 = jnp.concatenate([kp_ref[...], kc_ref[...], kn_ref[...]], axis=0)
    v3 = jnp.concatenate([vp_ref[...], vc_ref[...], vn_ref[...]], axis=0)
    q = q_ref[...]
    qi = lax.broadcasted_iota(jnp.int32, (G * B, 3 * B), 0) % B
    kj = lax.broadcasted_iota(jnp.int32, (G * B, 3 * B), 1)
    valid = (jnp.abs(kj - B - qi) <= B) & ((i > 0) | (kj >= B)) & ((i < nblk - 1) | (kj < 2 * B))
    for g in range(SWA_KV_HEADS):
        ks = k3[:, g * HEAD_DIM:(g + 1) * HEAD_DIM]
        vs = v3[:, g * HEAD_DIM:(g + 1) * HEAD_DIM]
        qs = jnp.concatenate([q[:, (g * G + hh) * HEAD_DIM:(g * G + hh + 1) * HEAD_DIM] for hh in range(G)], axis=0)
        s = lax.dot_general(qs, ks, (((1,), (1,)), ((), ())), preferred_element_type=F32)
        s = jnp.where(valid, s, NEG_BIG)
        sink = sink_ref[g]
        m = jnp.maximum(jnp.max(s, axis=-1, keepdims=True), sink)
        p = jnp.exp(s - m)
        l = jnp.sum(p, axis=-1, keepdims=True) + jnp.exp(sink - m)
        o = _dot(p.astype(BF16), vs) / l
        for hh in range(G):
            h = g * G + hh
            o_ref[:, h * HEAD_DIM:(h + 1) * HEAD_DIM] = o[hh * B:(hh + 1) * B].astype(BF16)


def _swa_attention(qk, plain, sink_col):
    L = qk.shape[0]
    B = SWA_BLOCK
    nblk = L // B
    prev = lambda i: jnp.maximum(i - 1, 0)
    nxt = lambda i: jnp.minimum(i + 1, nblk - 1)
    same = lambda i: i
    kspec = lambda f: pl.BlockSpec((B, LANES), lambda i: (f(i), 8))
    vspec = lambda f: pl.BlockSpec((B, LANES), lambda i: (f(i), 10))
    return pl.pallas_call(
        functools.partial(_swa_kernel, nblk=nblk),
        grid=(nblk,),
        in_specs=[pl.BlockSpec((B, 512), lambda i: (i, 1)),
                  kspec(prev), kspec(same), kspec(nxt),
                  vspec(prev), vspec(same), vspec(nxt),
                  pl.BlockSpec(sink_col.shape, lambda i: (0, 0, 0))],
        out_specs=pl.BlockSpec((B, 512), lambda i: (i, 0)),
        out_shape=jax.ShapeDtypeStruct((L, 512), BF16),
        compiler_params=_cparams(("parallel",)),
        name="swa_attn",
    )(qk, qk, qk, qk, plain, plain, plain, sink_col)


def _gdn_prep_kernel(x_ref, hp_ref, hn_ref, cw_ref, ba_ref, alog_ref, dtb_ref, bd_ref,
                     qkv_ref, gb_ref, gbt_ref, *, nblk):
    i = pl.program_id(0)
    x = x_ref[...].astype(F32)
    rid = lax.broadcasted_iota(jnp.int32, (TM, 1), 0)
    hp = jnp.where(i > 0, hp_ref[HALO - 1:HALO, :].astype(F32), 0.0)
    hn = jnp.where(i < nblk - 1, hn_ref[0:1, :].astype(F32), 0.0)
    xprev = jnp.where(rid == 0, hp, pltpu.roll(x, 1, 0))
    xnext = jnp.where(rid == TM - 1, hn, pltpu.roll(x, TM - 1, 0))
    y = xprev * cw_ref[0:1, :] + x * cw_ref[1:2, :] + xnext * cw_ref[2:3, :]
    y = y * _sigmoid(y)
    bd = bd_ref[...]
    for c in range(4):
        yc = y[:, c * LANES:(c + 1) * LANES]
        n = yc * lax.rsqrt(_group_sum(yc * yc, bd) + NORM_EPS)
        if c < 2:
            n = n * (HEAD_DIM ** -0.5)
        qkv_ref[:, c * LANES:(c + 1) * LANES] = n.astype(BF16)
    qkv_ref[:, 4 * LANES:6 * LANES] = y[:, 4 * LANES:6 * LANES].astype(BF16)

    ba = ba_ref[...]
    lane = lax.broadcasted_iota(jnp.int32, (1, LANES), 1)
    beta = _sigmoid(ba)
    sp_in = ba + dtb_ref[...]
    softplus = jnp.maximum(sp_in, 0.0) + jnp.log(1.0 + jnp.exp(-jnp.abs(sp_in)))
    g = jnp.where((lane >= 8) & (lane < 16), -jnp.exp(alog_ref[...]) * softplus, 0.0)
    rc = rid % GDN_CHUNK
    pre = g
    suf = g
    s = 1
    while s < GDN_CHUNK:
        pre = pre + jnp.where(rc >= s, pltpu.roll(pre, s, 0), 0.0)
        suf = suf + jnp.where(rc < GDN_CHUNK - s, pltpu.roll(suf, TM - s, 0), 0.0)
        s *= 2
    tot = pre + suf - g
    gc = jnp.where(lane < 12, pre, suf)
    slab = jnp.where(lane < 8, beta, jnp.where(lane < 16, gc, jnp.where(lane < 24, pltpu.roll(tot, 8, 1), 0.0)))
    gb_ref[...] = slab
    gbt_ref[...] = slab.T[0:24, :]


def _gdn_prep(plain, ba, conv_w, alog_row, dtb_row, bd):
    L = plain.shape[0]
    nblk = L // TM
    hb = TM // HALO
    const = lambda a: pl.BlockSpec(a.shape, lambda i: (0,) * a.ndim)
    return pl.pallas_call(
        functools.partial(_gdn_prep_kernel, nblk=nblk),
        grid=(nblk,),
        in_specs=[pl.BlockSpec((TM, 768), lambda i: (i, 0)),
                  pl.BlockSpec((HALO, 768), lambda i: (jnp.maximum(i * hb - 1, 0), 0)),
                  pl.BlockSpec((HALO, 768), lambda i: (jnp.minimum((i + 1) * hb, L // HALO - 1), 0)),
                  const(conv_w),
                  pl.BlockSpec((TM, LANES), lambda i: (i, 0)),
                  const(alog_row), const(dtb_row), const(bd)],
        out_specs=[pl.BlockSpec((TM, 768), lambda i: (i, 0)),
                   pl.BlockSpec((TM, LANES), lambda i: (i, 0)),
                   pl.BlockSpec((24, TM), lambda i: (0, i))],
        out_shape=[jax.ShapeDtypeStruct((L, 768), BF16),
                   jax.ShapeDtypeStruct((L, LANES), F32),
                   jax.ShapeDtypeStruct((24, L), F32)],
        compiler_params=_cparams(("parallel",)),
        name="gdn_prep",
    )(plain, plain, plain, conv_w, ba, alog_row, dtb_row, bd)


def _bmm(a, b):
    return jnp.einsum('bij,bjk->bik', a.astype(BF16), b.astype(BF16), preferred_element_type=F32)


def _bmm_nt(a, b):
    return jnp.einsum('bid,bjd->bij', a.astype(BF16), b.astype(BF16), preferred_element_type=F32)


def _bmm_tn(a, b):
    return jnp.einsum('bck,bcv->bkv', a.astype(BF16), b.astype(BF16), preferred_element_type=F32)


def _gdn_dir_kernel(qkv_ref, gb_ref, gbt_ref, o_ref, s_ref, *, rev):
    G, C, H = GDN_G, GDN_CHUNK, GDN_HEADS
    step = pl.program_id(0)

    @pl.when(step == 0)
    def _():
        s_ref[...] = jnp.zeros_like(s_ref)

    qkv = qkv_ref[...]

    def heads(x):
        return jnp.concatenate([x[:, h * HEAD_DIM:(h + 1) * HEAD_DIM].reshape(G, C, HEAD_DIM)
                                for h in range(H)], axis=0)

    q = heads(qkv[:, 0:256])
    k = heads(qkv[:, 256:512])
    v = heads(qkv[:, 512:768]).astype(F32)
    kf = k.astype(F32)
    qf = q.astype(F32)
    gb = gb_ref[...]
    gbt = gbt_ref[...]
    d0 = H if rev else 0

    def colv(base):
        return jnp.concatenate([gb[:, base + h:base + h + 1].reshape(G, C, 1) for h in range(H)], axis=0)

    beta = colv(d0)
    gcc = colv(8 + d0)
    gl = colv(16 + d0)
    grow = jnp.concatenate([gbt[8 + d0 + h:9 + d0 + h, c * C:(c + 1) * C].reshape(1, 1, C)
                            for h in range(H) for c in range(G)], axis=0)
    ii = lax.broadcasted_iota(jnp.int32, (1, C, C), 1)
    jj = lax.broadcasted_iota(jnp.int32, (1, C, C), 2)
    incl = (jj >= ii) if rev else (jj <= ii)
    strict = (jj > ii) if rev else (jj < ii)
    decay = jnp.exp(jnp.where(incl, gcc - grow, NEG_BIG))
    kk = _bmm_nt(k, k)
    qk = _bmm_nt(q, k)
    nmat = jnp.where(strict, kk * decay, 0.0) * beta
    eg = jnp.exp(gcc)
    rhs = jnp.concatenate([v * beta, kf * (beta * eg)], axis=-1)
    x = rhs - _bmm(nmat, rhs)
    p = nmat
    for _ in range(5):
        p = _bmm(p, p)
        x = x + _bmm(p, x)
    u = x[:, :, 0:HEAD_DIM]
    w = x[:, :, HEAD_DIM:2 * HEAD_DIM]
    qkm = jnp.where(incl, qk * decay, 0.0)
    q_dec = qf * eg
    k_tail = kf * jnp.exp(gl - gcc)
    dch = jnp.exp(gl[:, 0:1, :])

    def sel(a, c):
        return a.reshape((H, G) + a.shape[1:])[:, c]

    S = s_ref[...]
    for c in (range(G - 1, -1, -1) if rev else range(G)):
        vn = sel(u, c) - _bmm(sel(w, c), S)
        oc = _bmm(sel(q_dec, c), S) + _bmm(sel(qkm, c), vn)
        S = S * sel(dch, c) + _bmm_tn(sel(k_tail, c), vn)
        o_ref[c * C:(c + 1) * C, :] = jnp.concatenate([oc[h] for h in range(H)], axis=-1)
    s_ref[...] = S


def _gdn_dir(qkv, gb, gbt, rev):
    L = qkv.shape[0]
    T = GDN_G * GDN_CHUNK
    n = L // T
    idx = (lambda i: n - 1 - i) if rev else (lambda i: i)
    return pl.pallas_call(
        functools.partial(_gdn_dir_kernel, rev=rev),
        grid=(n,),
        in_specs=[pl.BlockSpec((T, 768), lambda i: (idx(i), 0)),
                  pl.BlockSpec((T, LANES), lambda i: (idx(i), 0)),
                  pl.BlockSpec((24, T), lambda i: (0, idx(i)))],
        out_specs=pl.BlockSpec((T, 256), lambda i: (idx(i), 0)),
        out_shape=jax.ShapeDtypeStruct((L, 256), F32),
        scratch_shapes=[pltpu.VMEM((GDN_HEADS, HEAD_DIM, HEAD_DIM), F32)],
        compiler_params=_cparams(("arbitrary",)),
        name="gdn_bwd" if rev else "gdn_fwd",
    )(qkv, gb, gbt)


def _merge_kernel(x_ref, yna_ref, yswa_ref, of_ref, ob_ref, z_ref, gate_ref, gn_ref, bd_ref,
                  wna_ref, wswa_ref, wgdn_ref, wout_ref, o_ref):
    bd = bd_ref[...]
    o = of_ref[...] + ob_ref[...]
    z = z_ref[...].astype(F32)
    parts = []
    for c in range(2):
        oc = o[:, c * LANES:(c + 1) * LANES]
        ms = _group_sum(oc * oc, bd) * (1.0 / HEAD_DIM)
        parts.append(oc * lax.rsqrt(ms + NORM_EPS))
    ygdn = jnp.concatenate(parts, axis=-1) * gn_ref[...] * (z * _sigmoid(z))
    m = (gate_ref[:, 0:D_MODEL].astype(F32) * _dot(yna_ref[...], wna_ref[...])
         + gate_ref[:, D_MODEL:2 * D_MODEL].astype(F32) * _dot(yswa_ref[...], wswa_ref[...])
         + gate_ref[:, 2 * D_MODEL:3 * D_MODEL].astype(F32) * _dot(ygdn.astype(BF16), wgdn_ref[...]))
    o_ref[...] = x_ref[...] + _dot(m.astype(BF16), wout_ref[...])


def _merge(x, yna, yswa, of, ob, plain, gate, gn, bd, wna, wswa, wgdn, wout):
    L = x.shape[0]
    row = lambda w_, c=0: pl.BlockSpec((TM, w_), lambda i: (i, c))
    const = lambda a: pl.BlockSpec(a.shape, lambda i: (0,) * a.ndim)
    return pl.pallas_call(
        _merge_kernel,
        grid=(L // TM,),
        in_specs=[row(D_MODEL), row(256), row(512), row(256), row(256), row(256, 4), row(SEG_GATE),
                  const(gn), const(bd), const(wna), const(wswa), const(wgdn), const(wout)],
        out_specs=row(D_MODEL),
        out_shape=jax.ShapeDtypeStruct((L, D_MODEL), F32),
        compiler_params=_cparams(("parallel",)),
        name="merge",
    )(x, yna, yswa, of, ob, plain, gate, gn, bd, wna, wswa, wgdn, wout)


def _ffn_kernel(x_ref, xp_ref, xn_ref, g_ref, wa_ref, wb_ref, cwa_ref, cwb_ref, ba_ref, bb_ref, wd_ref,
                o_ref, hbuf, acc, *, nblk):
    i = pl.program_id(0)
    j = pl.program_id(1)

    def normed(xv):
        ms = jnp.mean(xv * xv, axis=-1, keepdims=True)
        return xv * lax.rsqrt(ms + NORM_EPS) * g_ref[...]

    @pl.when(j == 0)
    def _():
        hbuf[0:HALO, :] = jnp.where(i > 0, normed(xp_ref[...]), 0.0).astype(BF16)
        hbuf[HALO:HALO + TM, :] = normed(x_ref[...]).astype(BF16)
        hbuf[HALO + TM:, :] = jnp.where(i < nblk - 1, normed(xn_ref[...]), 0.0).astype(BF16)
        acc[...] = jnp.zeros_like(acc)

    h = hbuf[...]
    n = TM + 2 * HALO

    def conv(w_ref, cw_ref, b_ref):
        u = _dot(h, w_ref[...])
        y = pltpu.roll(u, 1, 0) * cw_ref[0:1, :] + u * cw_ref[1:2, :] + pltpu.roll(u, n - 1, 0) * cw_ref[2:3, :]
        return y[HALO:HALO + TM] + b_ref[...]

    a = conv(wa_ref, cwa_ref, ba_ref)
    b = conv(wb_ref, cwb_ref, bb_ref)
    gated = (a * _sigmoid(a) * b).astype(BF16)
    acc[...] += _dot(gated, wd_ref[...])

    @pl.when(j == pl.num_programs(1) - 1)
    def _():
        o_ref[...] = x_ref[...] + acc[...]


def _ffn(x, g, w_up, conv_w, conv_b, w_down):
    L = x.shape[0]
    nblk = L // TM
    nj = D_FF // FFN_NC
    hb = TM // HALO
    return pl.pallas_call(
        functools.partial(_ffn_kernel, nblk=nblk),
        grid=(nblk, nj),
        in_specs=[pl.BlockSpec((TM, D_MODEL), lambda i, j: (i, 0)),
                  pl.BlockSpec((HALO, D_MODEL), lambda i, j: (jnp.maximum(i * hb - 1, 0), 0)),
                  pl.BlockSpec((HALO, D_MODEL), lambda i, j: (jnp.minimum((i + 1) * hb, L // HALO - 1), 0)),
                  pl.BlockSpec((1, D_MODEL), lambda i, j: (0, 0)),
                  pl.BlockSpec((D_MODEL, FFN_NC), lambda i, j: (0, j)),
                  pl.BlockSpec((D_MODEL, FFN_NC), lambda i, j: (0, nj + j)),
                  pl.BlockSpec((3, FFN_NC), lambda i, j: (0, j)),
                  pl.BlockSpec((3, FFN_NC), lambda i, j: (0, nj + j)),
                  pl.BlockSpec((1, FFN_NC), lambda i, j: (0, j)),
                  pl.BlockSpec((1, FFN_NC), lambda i, j: (0, nj + j)),
                  pl.BlockSpec((FFN_NC, D_MODEL), lambda i, j: (j, 0))],
        out_specs=pl.BlockSpec((TM, D_MODEL), lambda i, j: (i, 0)),
        out_shape=jax.ShapeDtypeStruct((L, D_MODEL), F32),
        scratch_shapes=[pltpu.VMEM((TM + 2 * HALO, D_MODEL), BF16), pltpu.VMEM((TM, D_MODEL), F32)],
        compiler_params=_cparams(("parallel", "arbitrary")),
        name="ffn",
    )(x, x, x, g, w_up, w_up, conv_w, conv_w, conv_b, conv_b, w_down)


def _pack_w_in(w):
    qa, ka, va = w[:, 0:256], w[:, 256:512], w[:, 512:768]
    qs, ks, vs = w[:, 768:1280], w[:, 1280:1408], w[:, 1408:1536]
    qkv_c, z_c = w[:, 1536:2304], w[:, 2304:2560]
    ba, gate = w[:, 2560:2576], w[:, 2576:5648]
    pad = jnp.zeros((w.shape[0], SEG_BA - 16), w.dtype)
    return jnp.concatenate([qa, ka, qs, ks, qkv_c, va, z_c, vs, gate, ba, pad], axis=1).astype(BF16)


def _rope_tables(L):
    inv = 1.0 / (ROPE_THETA ** (jnp.arange(0, HEAD_DIM, 2, dtype=F32) / HEAD_DIM))
    ang = jnp.arange(L, dtype=F32)[:, None] * inv[None, :]
    cos, sin = jnp.cos(ang), jnp.sin(ang)
    cos128 = jnp.concatenate([cos, cos, cos, cos], axis=1)
    sin128 = jnp.concatenate([-sin, sin, -sin, sin], axis=1)
    return cos128, sin128


def kernel(x, attn_norm, w_in, qk_norm, na_rpb, swa_sink, gdn_conv_w, gdn_a_log, gdn_dt_bias, gdn_norm,
           w_branch_na, w_branch_swa, w_branch_gdn, w_out, ffn_norm, w_up, ffn_conv_w, ffn_conv_b, w_down):
    B, L, D = x.shape
    assert B == 1 and D == D_MODEL and L % (NA_ROWS * GRID_W) == 0 and L // GRID_W >= 2 * NA_ROWS
    depth = w_in.shape[0]
    cos128, sin128 = _rope_tables(L)
    blockdiag = jnp.asarray(np.kron(np.eye(2), np.ones((HEAD_DIM, HEAD_DIM))), BF16)
    bias_tabs = _na_bias_tables(na_rpb)
    scale = HEAD_DIM ** -0.5
    G = SWA_Q_HEADS // SWA_KV_HEADS
    xs = x[0]
    for l in range(depth):
        gain = jnp.concatenate([jnp.tile(qk_norm[l, 0] * scale, NA_HEADS), jnp.tile(qk_norm[l, 1], NA_HEADS),
                                jnp.tile(qk_norm[l, 2] * scale, SWA_Q_HEADS), jnp.tile(qk_norm[l, 3], SWA_KV_HEADS)])[None]
        qk, plain, gate, ba = _inproj(xs, attn_norm[l][None], _pack_w_in(w_in[l]), gain, cos128, sin128, blockdiag)
        y_na = _na_attention(qk, plain, bias_tabs[l])
        sink_col = jnp.repeat(swa_sink[l].reshape(SWA_KV_HEADS, G), SWA_BLOCK, axis=1)[:, :, None]
        y_swa = _swa_attention(qk, plain, sink_col)
        alog_row = jnp.zeros((1, LANES), F32).at[0, 8:16].set(gdn_a_log[l].reshape(-1))
        dtb_row = jnp.zeros((1, LANES), F32).at[0, 8:16].set(gdn_dt_bias[l].reshape(-1))
        qkv_n, gb, gbt = _gdn_prep(plain, ba, gdn_conv_w[l], alog_row, dtb_row, blockdiag)
        o_f = _gdn_dir(qkv_n, gb, gbt, rev=False)
        o_b = _gdn_dir(qkv_n, gb, gbt, rev=True)
        xs = _merge(xs, y_na, y_swa, o_f, o_b, plain, gate, jnp.tile(gdn_norm[l], GDN_HEADS)[None], blockdiag,
                    w_branch_na[l].astype(BF16), w_branch_swa[l].astype(BF16), w_branch_gdn[l].astype(BF16),
                    w_out[l].astype(BF16))
        xs = _ffn(xs, ffn_norm[l][None], w_up[l].astype(BF16), ffn_conv_w[l], ffn_conv_b[l][None],
                  w_down[l].astype(BF16))
    return xs[None]
```

```python
import functools

import numpy as np
import jax
import jax.numpy as jnp
from jax import lax
from jax.experimental import pallas as pl
from jax.experimental.pallas import tpu as pltpu

F32 = jnp.float32
BF16 = jnp.bfloat16

D_MODEL = 1024
HEAD_DIM = 64
GRID_W = 64
NORM_EPS = 1e-6
NA_HEADS = 4
NA_KH = 8
NA_KW = 16
SWA_Q_HEADS = 8
SWA_KV_HEADS = 2
SWA_BLOCK = 128
ROPE_THETA = 10000.0
GDN_HEADS = 4
GDN_CHUNK = 64
D_FF = 2816

LANES = 128
NEG_BIG = -1e30
VMEM_LIMIT = 56 * 1024 * 1024

SEG_QK = 1152
SEG_PLAIN = 1408
SEG_GATE = 3072
SEG_BA = 128
IN_PACKED = SEG_QK + SEG_PLAIN + SEG_GATE + SEG_BA

TM = 512
NA_ROWS = 8
SWA_QB = 4
GDN_G = 4
FFN_NC = 1408
HALO = 8


def _cparams(sem):
    return pltpu.CompilerParams(dimension_semantics=sem, vmem_limit_bytes=VMEM_LIMIT)


def _dot(a, b):
    return jnp.dot(a, b, preferred_element_type=F32)


def _sigmoid(x):
    return 1.0 / (1.0 + jnp.exp(-x))


def _group_sum(sq, bd):
    hi = sq.astype(BF16)
    lo = (sq - hi.astype(F32)).astype(BF16)
    return _dot(hi, bd) + _dot(lo, bd)


def _inproj_kernel(x_ref, g_ref, w_ref, gain_ref, cos_ref, sin_ref, bd_ref,
                   oqk_ref, oplain_ref, ogate_ref, oba_ref):
    x = x_ref[...]
    ms = jnp.mean(x * x, axis=-1, keepdims=True)
    h = (x * lax.rsqrt(ms + NORM_EPS) * g_ref[...]).astype(BF16)
    bd = bd_ref[...]
    lane = lax.broadcasted_iota(jnp.int32, (1, LANES), 1)
    first_half = (lane % HEAD_DIM) < (HEAD_DIM // 2)
    cos = cos_ref[...]
    sin = sin_ref[...]

    t = _dot(h, w_ref[:, 0:SEG_QK])
    for c in range(SEG_QK // LANES):
        tc = t[:, c * LANES:(c + 1) * LANES]
        ss = _group_sum(tc * tc, bd)
        y = tc * lax.rsqrt(ss * (1.0 / HEAD_DIM) + NORM_EPS) * gain_ref[:, c * LANES:(c + 1) * LANES]
        if c >= 4:
            rot = jnp.where(first_half, pltpu.roll(y, LANES - HEAD_DIM // 2, 1), pltpu.roll(y, HEAD_DIM // 2, 1))
            y = y * cos + rot * sin
        oqk_ref[:, c * LANES:(c + 1) * LANES] = y.astype(BF16)

    o = SEG_QK
    oplain_ref[...] = _dot(h, w_ref[:, o:o + SEG_PLAIN]).astype(BF16)
    o += SEG_PLAIN
    for c in range(3):
        tg = _dot(h, w_ref[:, o + c * D_MODEL:o + (c + 1) * D_MODEL])
        ogate_ref[:, c * D_MODEL:(c + 1) * D_MODEL] = _sigmoid(tg).astype(BF16)
    o += SEG_GATE
    oba_ref[...] = _dot(h, w_ref[:, o:o + SEG_BA])


def _inproj(x, g, w, gain, cos, sin, bd):
    L = x.shape[0]
    row = lambda w_: pl.BlockSpec((TM, w_), lambda i: (i, 0))
    const = lambda a: pl.BlockSpec(a.shape, lambda i: (0,) * a.ndim)
    return pl.pallas_call(
        _inproj_kernel,
        grid=(L // TM,),
        in_specs=[row(D_MODEL), const(g),
                  pl.BlockSpec(w.shape, lambda i: (0, 0), pipeline_mode=pl.Buffered(1)),
                  const(gain), row(LANES), row(LANES), const(bd)],
        out_specs=[row(SEG_QK), row(SEG_PLAIN), row(SEG_GATE), row(SEG_BA)],
        out_shape=[jax.ShapeDtypeStruct((L, SEG_QK), BF16),
                   jax.ShapeDtypeStruct((L, SEG_PLAIN), BF16),
                   jax.ShapeDtypeStruct((L, SEG_GATE), BF16),
                   jax.ShapeDtypeStruct((L, SEG_BA), F32)],
        compiler_params=_cparams(("parallel",)),
        name="inproj",
    )(x, g, w, gain, cos, sin, bd)


def _toeplitz_kernel(onehot_ref, rpb_ref, o_ref):
    r = rpb_ref[...]
    hi = r.astype(BF16)
    r1 = r - hi.astype(F32)
    mid = r1.astype(BF16)
    lo = (r1 - mid.astype(F32)).astype(BF16)
    oh = onehot_ref[...]
    o_ref[...] = _dot(oh, hi) + _dot(oh, mid) + _dot(oh, lo)


def _na_bias_tables(na_rpb, rows):
    depth, H, nr, nc = na_rpb.shape
    qc = np.arange(GRID_W)[:, None]
    kc = np.arange(GRID_W)[None, :]
    dc = np.clip(kc - qc + (NA_KW - 1), 0, 2 * NA_KW - 2)
    onehot = np.zeros((GRID_W * GRID_W, 32), np.float32)
    onehot[np.arange(GRID_W * GRID_W), dc.reshape(-1)] = 1.0
    cols = depth * H * nr
    cols_p = -(-cols // LANES) * LANES
    rpb_t = jnp.pad(na_rpb.reshape(cols, nc).T, ((0, 32 - nc), (0, cols_p - cols)))
    w = pl.pallas_call(
        _toeplitz_kernel,
        out_shape=jax.ShapeDtypeStruct((GRID_W * GRID_W, cols_p), F32),
        name="na_bias",
    )(jnp.asarray(onehot, BF16), rpb_t)
    w = w[:, :cols].reshape(GRID_W, GRID_W, depth, H, nr)
    col_start = np.clip(np.arange(GRID_W) - NA_KW // 2, 0, GRID_W - NA_KW)
    col_in = (kc >= col_start[:, None]) & (kc < col_start[:, None] + NA_KW)
    w = jnp.where(jnp.asarray(col_in)[:, :, None, None, None], w, NEG_BIG)
    w = jnp.transpose(w, (2, 3, 4, 1, 0))
    neg = jnp.full((depth, H, GRID_W, GRID_W), NEG_BIG, F32)
    tabs = []
    for r0 in _na_pair_type_rows(rows):
        wstart, a_idx = _na_pair_structure(r0, rows)
        key_rows = []
        for j in range(NA_KH + 1):
            key_rows.append(jnp.concatenate([neg if a_idx[j, e] < 0 else w[:, :, a_idx[j, e]] for e in range(2)], axis=-1))
        tabs.append(jnp.concatenate(key_rows, axis=-2))
    return jnp.stack(tabs, axis=1)


def _na_pair_structure(r0, rows):
    wstart = int(np.clip(r0 - NA_KH // 2, 0, rows - NA_KH - 1))
    a_idx = np.full((NA_KH + 1, 2), -1, np.int64)
    for e in range(2):
        rr = r0 + e
        rs = int(np.clip(rr - NA_KH // 2, 0, rows - NA_KH))
        for j in range(NA_KH + 1):
            krow = wstart + j
            if rs <= krow < rs + NA_KH:
                a_idx[j, e] = krow - rr + NA_KH - 1
    return wstart, a_idx


def _na_pair_type_rows(rows):
    reps = [0, 2, 4, rows - 4, rows - 2]
    for r0 in range(0, rows, 2):
        t = 3 + (r0 - (rows - 4)) // 2 if r0 >= rows - 4 else min(r0 // 2, 2)
        ws, a = _na_pair_structure(r0, rows)
        ws_t, a_t = _na_pair_structure(reps[t], rows)
        assert (a == a_t).all() and r0 - ws == reps[t] - ws_t
    return reps


def _na_kernel(q_ref, kp_ref, kc_ref, kn_ref, vp_ref, vc_ref, vn_ref, bias_ref, o_ref, kbuf, vbuf, *, rows):
    i = pl.program_id(0)
    blk = NA_ROWS * GRID_W
    kbuf[0:blk, :] = kp_ref[...]
    kbuf[blk:2 * blk, :] = kc_ref[...]
    kbuf[2 * blk:3 * blk, :] = kn_ref[...]
    vbuf[0:blk, :] = vp_ref[...]
    vbuf[blk:2 * blk, :] = vc_ref[...]
    vbuf[2 * blk:3 * blk, :] = vn_ref[...]
    win = (NA_KH + 1) * GRID_W
    pair = 2 * GRID_W

    for pp in range(NA_ROWS // 2):
        r0 = i * NA_ROWS + 2 * pp
        wstart = jnp.clip(r0 - NA_KH // 2, 0, rows - NA_KH - 1)
        t = jnp.where(r0 >= rows - 4, 3 + (r0 - (rows - 4)) // 2, jnp.minimum(r0 // 2, 2))
        off = pl.multiple_of((wstart - (i - 1) * NA_ROWS) * GRID_W, GRID_W)
        kw = kbuf[pl.ds(off, win), :]
        vw = vbuf[pl.ds(off, win), :]
        q = q_ref[pp * pair:(pp + 1) * pair, :]
        outs = []
        for h in range(NA_HEADS):
            sl = slice(h * HEAD_DIM, (h + 1) * HEAD_DIM)
            s = lax.dot_general(kw[:, sl], q[:, sl], (((1,), (1,)), ((), ())), preferred_element_type=F32)
            s = s + bias_ref[t, h]
            m = jnp.max(s, axis=0, keepdims=True)
            p = jnp.exp(s - m)
            l = jnp.sum(p, axis=0, keepdims=True)
            pn = (p * (1.0 / l)).astype(BF16)
            outs.append(lax.dot_general(pn, vw[:, sl], (((0,), (0,)), ((), ())), preferred_element_type=F32))
        o_ref[pp * pair:(pp + 1) * pair, :] = jnp.concatenate(outs, axis=-1).astype(BF16)


def _na_attention(qk, plain, bias):
    L = qk.shape[0]
    rows = L // GRID_W
    nblk = rows // NA_ROWS
    blk = NA_ROWS * GRID_W
    w = NA_HEADS * HEAD_DIM
    prev = lambda i: jnp.maximum(i - 1, 0)
    nxt = lambda i: jnp.minimum(i + 1, nblk - 1)
    spec = lambda f, c: pl.BlockSpec((blk, w), lambda i: (f(i), c))
    same = lambda i: i
    return pl.pallas_call(
        functools.partial(_na_kernel, rows=rows),
        grid=(nblk,),
        in_specs=[spec(same, 0),
                  spec(prev, 1), spec(same, 1), spec(nxt, 1),
                  spec(prev, 3), spec(same, 3), spec(nxt, 3),
                  pl.BlockSpec(bias.shape, lambda i: (0,) * bias.ndim)],
        out_specs=pl.BlockSpec((blk, w), lambda i: (i, 0)),
        out_shape=jax.ShapeDtypeStruct((L, w), BF16),
        scratch_shapes=[pltpu.VMEM((3 * blk, w), BF16), pltpu.VMEM((3 * blk, w), BF16)],
        compiler_params=_cparams(("parallel",)),
        name="na_attn",
    )(qk, qk, qk, qk, plain, plain, plain, bias)


def _swa_kernel(q_ref, kp_ref, kc_ref, kn_ref, vp_ref, vc_ref, vn_ref, band_ref, sink_ref, o_ref, *, nstep):
    i = pl.program_id(0)
    B = SWA_BLOCK
    G = SWA_Q_HEADS // SWA_KV_HEADS
    k_all = jnp.concatenate([kp_ref[...], kc_ref[...], kn_ref[...]], axis=0)
    v_all = jnp.concatenate([vp_ref[...], vc_ref[...], vn_ref[...]], axis=0)
    for b in range(SWA_QB):
        if b == 0:
            t = jnp.where(i == 0, 1, 0)
        elif b == SWA_QB - 1:
            t = jnp.where(i == nstep - 1, 2, 0)
        else:
            t = 0
        band = band_ref[t]
        for g in range(SWA_KV_HEADS):
            ks = k_all[b * B:(b + 3) * B, g * HEAD_DIM:(g + 1) * HEAD_DIM]
            vs = v_all[b * B:(b + 3) * B, g * HEAD_DIM:(g + 1) * HEAD_DIM]
            qs = jnp.concatenate([q_ref[b * B:(b + 1) * B, (g * G + hh) * HEAD_DIM:(g * G + hh + 1) * HEAD_DIM]
                                  for hh in range(G)], axis=0)
            s = lax.dot_general(ks, qs, (((1,), (1,)), ((), ())), preferred_element_type=F32) + band
            sink = sink_ref[g]
            m = jnp.maximum(jnp.max(s, axis=0, keepdims=True), sink)
            p = jnp.exp(s - m)
            l = jnp.sum(p, axis=0, keepdims=True) + jnp.exp(sink - m)
            pn = (p * (1.0 / l)).astype(BF16)
            o = lax.dot_general(pn, vs, (((0,), (0,)), ((), ())), preferred_element_type=F32)
            for hh in range(G):
                h = g * G + hh
                o_ref[b * B:(b + 1) * B, h * HEAD_DIM:(h + 1) * HEAD_DIM] = o[hh * B:(hh + 1) * B].astype(BF16)


def _swa_band_tables():
    B = SWA_BLOCK
    G = SWA_Q_HEADS // SWA_KV_HEADS
    qi = np.arange(G * B)[None, :] % B
    kj = np.arange(3 * B)[:, None]
    band = np.abs(kj - B - qi) <= B
    tabs = [band, band & (kj >= B), band & (kj < 2 * B)]
    return jnp.asarray(np.where(np.stack(tabs), 0.0, NEG_BIG), F32)


def _swa_attention(qk, plain, band, sink_col):
    L = qk.shape[0]
    B = SWA_BLOCK
    T = SWA_QB * B
    nstep = L // T
    nblk = L // B
    prev = lambda i: jnp.maximum(i * SWA_QB - 1, 0)
    nxt = lambda i: jnp.minimum((i + 1) * SWA_QB, nblk - 1)
    return pl.pallas_call(
        functools.partial(_swa_kernel, nstep=nstep),
        grid=(nstep,),
        in_specs=[pl.BlockSpec((T, 512), lambda i: (i, 1)),
                  pl.BlockSpec((B, LANES), lambda i: (prev(i), 8)),
                  pl.BlockSpec((T, LANES), lambda i: (i, 8)),
                  pl.BlockSpec((B, LANES), lambda i: (nxt(i), 8)),
                  pl.BlockSpec((B, LANES), lambda i: (prev(i), 10)),
                  pl.BlockSpec((T, LANES), lambda i: (i, 10)),
                  pl.BlockSpec((B, LANES), lambda i: (nxt(i), 10)),
                  pl.BlockSpec(band.shape, lambda i: (0, 0, 0)),
                  pl.BlockSpec(sink_col.shape, lambda i: (0, 0, 0))],
        out_specs=pl.BlockSpec((T, 512), lambda i: (i, 0)),
        out_shape=jax.ShapeDtypeStruct((L, 512), BF16),
        compiler_params=_cparams(("parallel",)),
        name="swa_attn",
    )(qk, qk, qk, qk, plain, plain, plain, band, sink_col)


def _gdn_prep_kernel(x_ref, hp_ref, hn_ref, cw_ref, ba_ref, alog_ref, dtb_ref, bd_ref,
                     qkv_ref, gb_ref, gbt_ref, *, nblk):
    i = pl.program_id(0)
    x = x_ref[...].astype(F32)
    rid = lax.broadcasted_iota(jnp.int32, (TM, 1), 0)
    hp = jnp.where(i > 0, hp_ref[HALO - 1:HALO, :].astype(F32), 0.0)
    hn = jnp.where(i < nblk - 1, hn_ref[0:1, :].astype(F32), 0.0)
    xprev = jnp.where(rid == 0, hp, pltpu.roll(x, 1, 0))
    xnext = jnp.where(rid == TM - 1, hn, pltpu.roll(x, TM - 1, 0))
    y = xprev * cw_ref[0:1, :] + x * cw_ref[1:2, :] + xnext * cw_ref[2:3, :]
    y = y * _sigmoid(y)
    bd = bd_ref[...]
    for c in range(4):
        yc = y[:, c * LANES:(c + 1) * LANES]
        n = yc * lax.rsqrt(_group_sum(yc * yc, bd) + NORM_EPS)
        if c < 2:
            n = n * (HEAD_DIM ** -0.5)
        qkv_ref[:, c * LANES:(c + 1) * LANES] = n.astype(BF16)
    qkv_ref[:, 4 * LANES:6 * LANES] = y[:, 4 * LANES:6 * LANES].astype(BF16)

    ba = ba_ref[...]
    lane = lax.broadcasted_iota(jnp.int32, (1, LANES), 1)
    beta = _sigmoid(ba)
    sp_in = ba + dtb_ref[...]
    softplus = jnp.maximum(sp_in, 0.0) + jnp.log(1.0 + jnp.exp(-jnp.abs(sp_in)))
    g = jnp.where((lane >= 8) & (lane < 16), -jnp.exp(alog_ref[...]) * softplus, 0.0)
    rc = rid % GDN_CHUNK
    pre = g
    suf = g
    s = 1
    while s < GDN_CHUNK:
        pre = pre + jnp.where(rc >= s, pltpu.roll(pre, s, 0), 0.0)
        suf = suf + jnp.where(rc < GDN_CHUNK - s, pltpu.roll(suf, TM - s, 0), 0.0)
        s *= 2
    tot = pre + suf - g
    gc = jnp.where(lane < 12, pre, suf)
    slab = jnp.where(lane < 8, beta, jnp.where(lane < 16, gc, jnp.where(lane < 24, pltpu.roll(tot, 8, 1), 0.0)))
    gb_ref[...] = slab
    gbt_ref[...] = slab.T[0:24, :]


def _gdn_prep(plain, ba, conv_w, alog_row, dtb_row, bd):
    L = plain.shape[0]
    nblk = L // TM
    hb = TM // HALO
    const = lambda a: pl.BlockSpec(a.shape, lambda i: (0,) * a.ndim)
    return pl.pallas_call(
        functools.partial(_gdn_prep_kernel, nblk=nblk),
        grid=(nblk,),
        in_specs=[pl.BlockSpec((TM, 768), lambda i: (i, 0)),
                  pl.BlockSpec((HALO, 768), lambda i: (jnp.maximum(i * hb - 1, 0), 0)),
                  pl.BlockSpec((HALO, 768), lambda i: (jnp.minimum((i + 1) * hb, L // HALO - 1), 0)),
                  const(conv_w),
                  pl.BlockSpec((TM, LANES), lambda i: (i, 0)),
                  const(alog_row), const(dtb_row), const(bd)],
        out_specs=[pl.BlockSpec((TM, 768), lambda i: (i, 0)),
                   pl.BlockSpec((TM, LANES), lambda i: (i, 0)),
                   pl.BlockSpec((24, TM), lambda i: (0, i))],
        out_shape=[jax.ShapeDtypeStruct((L, 768), BF16),
                   jax.ShapeDtypeStruct((L, LANES), F32),
                   jax.ShapeDtypeStruct((24, L), F32)],
        compiler_params=_cparams(("parallel",)),
        name="gdn_prep",
    )(plain, plain, plain, conv_w, ba, alog_row, dtb_row, bd)


def _bmm(a, b):
    return jnp.einsum('bij,bjk->bik', a.astype(BF16), b.astype(BF16), preferred_element_type=F32)


def _bmm_nt(a, b):
    return jnp.einsum('bid,bjd->bij', a.astype(BF16), b.astype(BF16), preferred_element_type=F32)


def _bmm_tn(a, b):
    return jnp.einsum('bck,bcv->bkv', a.astype(BF16), b.astype(BF16), preferred_element_type=F32)


def _gdn_dir_kernel(qkv_ref, gb_ref, gbt_ref, o_ref, s_ref, *, rev):
    G, C, H = GDN_G, GDN_CHUNK, GDN_HEADS
    step = pl.program_id(0)

    @pl.when(step == 0)
    def _():
        s_ref[...] = jnp.zeros_like(s_ref)

    qkv = qkv_ref[...]

    def heads(x):
        return jnp.concatenate([x[:, h * HEAD_DIM:(h + 1) * HEAD_DIM].reshape(G, C, HEAD_DIM)
                                for h in range(H)], axis=0)

    q = heads(qkv[:, 0:256])
    k = heads(qkv[:, 256:512])
    v = heads(qkv[:, 512:768]).astype(F32)
    kf = k.astype(F32)
    qf = q.astype(F32)
    gb = gb_ref[...]
    gbt = gbt_ref[...]
    d0 = H if rev else 0

    def colv(base):
        return jnp.concatenate([gb[:, base + h:base + h + 1].reshape(G, C, 1) for h in range(H)], axis=0)

    beta = colv(d0)
    gcc = colv(8 + d0)
    gl = colv(16 + d0)
    grow = jnp.concatenate([gbt[8 + d0 + h:9 + d0 + h, c * C:(c + 1) * C].reshape(1, 1, C)
                            for h in range(H) for c in range(G)], axis=0)
    ii = lax.broadcasted_iota(jnp.int32, (1, C, C), 1)
    jj = lax.broadcasted_iota(jnp.int32, (1, C, C), 2)
    incl = (jj >= ii) if rev else (jj <= ii)
    strict = (jj > ii) if rev else (jj < ii)
    decay = jnp.exp(jnp.where(incl, gcc - grow, NEG_BIG))
    kk = _bmm_nt(k, k)
    qk = _bmm_nt(q, k)
    nmat = jnp.where(strict, kk * decay, 0.0) * beta
    eg = jnp.exp(gcc)
    rhs = jnp.concatenate([v * beta, kf * (beta * eg)], axis=-1)
    x = rhs - _bmm(nmat, rhs)
    p = nmat
    for _ in range(5):
        p = _bmm(p, p)
        x = x + _bmm(p, x)
    u = x[:, :, 0:HEAD_DIM]
    w = x[:, :, HEAD_DIM:2 * HEAD_DIM]
    qkm = jnp.where(incl, qk * decay, 0.0)
    q_dec = qf * eg
    k_tail = kf * jnp.exp(gl - gcc)
    dch = jnp.exp(gl[:, 0:1, :])

    def sel(a, c):
        return a.reshape((H, G) + a.shape[1:])[:, c]

    S = s_ref[...]
    for c in (range(G - 1, -1, -1) if rev else range(G)):
        vn = sel(u, c) - _bmm(sel(w, c), S)
        oc = _bmm(sel(q_dec, c), S) + _bmm(sel(qkm, c), vn)
        S = S * sel(dch, c) + _bmm_tn(sel(k_tail, c), vn)
        o_ref[c * C:(c + 1) * C, :] = jnp.concatenate([oc[h] for h in range(H)], axis=-1)
    s_ref[...] = S


def _gdn_dir(qkv, gb, gbt, rev):
    L = qkv.shape[0]
    T = GDN_G * GDN_CHUNK
    n = L // T
    idx = (lambda i: n - 1 - i) if rev else (lambda i: i)
    return pl.pallas_call(
        functools.partial(_gdn_dir_kernel, rev=rev),
        grid=(n,),
        in_specs=[pl.BlockSpec((T, 768), lambda i: (idx(i), 0)),
                  pl.BlockSpec((T, LANES), lambda i: (idx(i), 0)),
                  pl.BlockSpec((24, T), lambda i: (0, idx(i)))],
        out_specs=pl.BlockSpec((T, 256), lambda i: (idx(i), 0)),
        out_shape=jax.ShapeDtypeStruct((L, 256), F32),
        scratch_shapes=[pltpu.VMEM((GDN_HEADS, HEAD_DIM, HEAD_DIM), F32)],
        compiler_params=_cparams(("arbitrary",)),
        name="gdn_bwd" if rev else "gdn_fwd",
    )(qkv, gb, gbt)


def _merge_kernel(x_ref, yna_ref, yswa_ref, of_ref, ob_ref, z_ref, gate_ref, gn_ref, bd_ref,
                  wna_ref, wswa_ref, wgdn_ref, wout_ref, o_ref):
    bd = bd_ref[...]
    o = of_ref[...] + ob_ref[...]
    z = z_ref[...].astype(F32)
    parts = []
    for c in range(2):
        oc = o[:, c * LANES:(c + 1) * LANES]
        ms = _group_sum(oc * oc, bd) * (1.0 / HEAD_DIM)
        parts.append(oc * lax.rsqrt(ms + NORM_EPS))
    ygdn = jnp.concatenate(parts, axis=-1) * gn_ref[...] * (z * _sigmoid(z))
    m = (gate_ref[:, 0:D_MODEL].astype(F32) * _dot(yna_ref[...], wna_ref[...])
         + gate_ref[:, D_MODEL:2 * D_MODEL].astype(F32) * _dot(yswa_ref[...], wswa_ref[...])
         + gate_ref[:, 2 * D_MODEL:3 * D_MODEL].astype(F32) * _dot(ygdn.astype(BF16), wgdn_ref[...]))
    o_ref[...] = x_ref[...] + _dot(m.astype(BF16), wout_ref[...])


def _merge(x, yna, yswa, of, ob, plain, gate, gn, bd, wna, wswa, wgdn, wout):
    L = x.shape[0]
    row = lambda w_, c=0: pl.BlockSpec((TM, w_), lambda i: (i, c))
    const = lambda a: pl.BlockSpec(a.shape, lambda i: (0,) * a.ndim)
    return pl.pallas_call(
        _merge_kernel,
        grid=(L // TM,),
        in_specs=[row(D_MODEL), row(256), row(512), row(256), row(256), row(256, 4), row(SEG_GATE),
                  const(gn), const(bd), const(wna), const(wswa), const(wgdn), const(wout)],
        out_specs=row(D_MODEL),
        out_shape=jax.ShapeDtypeStruct((L, D_MODEL), F32),
        compiler_params=_cparams(("parallel",)),
        name="merge",
    )(x, yna, yswa, of, ob, plain, gate, gn, bd, wna, wswa, wgdn, wout)


def _ffn_kernel(x_ref, xp_ref, xn_ref, g_ref, wa_ref, wb_ref, cwa_ref, cwb_ref, ba_ref, bb_ref, wd_ref,
                o_ref, hbuf, acc, *, nblk):
    i = pl.program_id(0)
    j = pl.program_id(1)

    def normed(xv):
        ms = jnp.mean(xv * xv, axis=-1, keepdims=True)
        return xv * lax.rsqrt(ms + NORM_EPS) * g_ref[...]

    @pl.when(j == 0)
    def _():
        hbuf[0:HALO, :] = jnp.where(i > 0, normed(xp_ref[...]), 0.0).astype(BF16)
        hbuf[HALO:HALO + TM, :] = normed(x_ref[...]).astype(BF16)
        hbuf[HALO + TM:, :] = jnp.where(i < nblk - 1, normed(xn_ref[...]), 0.0).astype(BF16)
        acc[...] = jnp.zeros_like(acc)

    h = hbuf[...]
    n = TM + 2 * HALO

    def conv(w_ref, cw_ref, b_ref):
        u = _dot(h, w_ref[...])
        y = pltpu.roll(u, 1, 0) * cw_ref[0:1, :] + u * cw_ref[1:2, :] + pltpu.roll(u, n - 1, 0) * cw_ref[2:3, :]
        return y[HALO:HALO + TM] + b_ref[...]

    a = conv(wa_ref, cwa_ref, ba_ref)
    b = conv(wb_ref, cwb_ref, bb_ref)
    gated = (a * _sigmoid(a) * b).astype(BF16)
    acc[...] += _dot(gated, wd_ref[...])

    @pl.when(j == pl.num_programs(1) - 1)
    def _():
        o_ref[...] = x_ref[...] + acc[...]


def _ffn(x, g, w_up, conv_w, conv_b, w_down):
    L = x.shape[0]
    nblk = L // TM
    nj = D_FF // FFN_NC
    hb = TM // HALO
    return pl.pallas_call(
        functools.partial(_ffn_kernel, nblk=nblk),
        grid=(nblk, nj),
        in_specs=[pl.BlockSpec((TM, D_MODEL), lambda i, j: (i, 0)),
                  pl.BlockSpec((HALO, D_MODEL), lambda i, j: (jnp.maximum(i * hb - 1, 0), 0)),
                  pl.BlockSpec((HALO, D_MODEL), lambda i, j: (jnp.minimum((i + 1) * hb, L // HALO - 1), 0)),
                  pl.BlockSpec((1, D_MODEL), lambda i, j: (0, 0)),
                  pl.BlockSpec((D_MODEL, FFN_NC), lambda i, j: (0, j)),
                  pl.BlockSpec((D_MODEL, FFN_NC), lambda i, j: (0, nj + j)),
                  pl.BlockSpec((3, FFN_NC), lambda i, j: (0, j)),
                  pl.BlockSpec((3, FFN_NC), lambda i, j: (0, nj + j)),
                  pl.BlockSpec((1, FFN_NC), lambda i, j: (0, j)),
                  pl.BlockSpec((1, FFN_NC), lambda i, j: (0, nj + j)),
                  pl.BlockSpec((FFN_NC, D_MODEL), lambda i, j: (j, 0))],
        out_specs=pl.BlockSpec((TM, D_MODEL), lambda i, j: (i, 0)),
        out_shape=jax.ShapeDtypeStruct((L, D_MODEL), F32),
        scratch_shapes=[pltpu.VMEM((TM + 2 * HALO, D_MODEL), BF16), pltpu.VMEM((TM, D_MODEL), F32)],
        compiler_params=_cparams(("parallel", "arbitrary")),
        name="ffn",
    )(x, x, x, g, w_up, w_up, conv_w, conv_w, conv_b, conv_b, w_down)


def _pack_w_in(w):
    qa, ka, va = w[:, 0:256], w[:, 256:512], w[:, 512:768]
    qs, ks, vs = w[:, 768:1280], w[:, 1280:1408], w[:, 1408:1536]
    qkv_c, z_c = w[:, 1536:2304], w[:, 2304:2560]
    ba, gate = w[:, 2560:2576], w[:, 2576:5648]
    pad = jnp.zeros((w.shape[0], SEG_BA - 16), w.dtype)
    return jnp.concatenate([qa, ka, qs, ks, qkv_c, va, z_c, vs, gate, ba, pad], axis=1).astype(BF16)


def _rope_tables(L):
    inv = 1.0 / (ROPE_THETA ** (jnp.arange(0, HEAD_DIM, 2, dtype=F32) / HEAD_DIM))
    ang = jnp.arange(L, dtype=F32)[:, None] * inv[None, :]
    cos, sin = jnp.cos(ang), jnp.sin(ang)
    cos128 = jnp.concatenate([cos, cos, cos, cos], axis=1)
    sin128 = jnp.concatenate([-sin, sin, -sin, sin], axis=1)
    return cos128, sin128


def kernel(x, attn_norm, w_in, qk_norm, na_rpb, swa_sink, gdn_conv_w, gdn_a_log, gdn_dt_bias, gdn_norm,
           w_branch_na, w_branch_swa, w_branch_gdn, w_out, ffn_norm, w_up, ffn_conv_w, ffn_conv_b, w_down):
    B, L, D = x.shape
    assert B == 1 and D == D_MODEL and L % (NA_ROWS * GRID_W) == 0 and L // GRID_W >= 2 * NA_ROWS
    depth = w_in.shape[0]
    cos128, sin128 = _rope_tables(L)
    blockdiag = jnp.asarray(np.kron(np.eye(2), np.ones((HEAD_DIM, HEAD_DIM))), BF16)
    bias_tabs = _na_bias_tables(na_rpb, L // GRID_W)
    swa_band = _swa_band_tables()
    scale = HEAD_DIM ** -0.5
    G = SWA_Q_HEADS // SWA_KV_HEADS
    xs = x[0]
    for l in range(depth):
        gain = jnp.concatenate([jnp.tile(qk_norm[l, 0] * scale, NA_HEADS), jnp.tile(qk_norm[l, 1], NA_HEADS),
                                jnp.tile(qk_norm[l, 2] * scale, SWA_Q_HEADS), jnp.tile(qk_norm[l, 3], SWA_KV_HEADS)])[None]
        qk, plain, gate, ba = _inproj(xs, attn_norm[l][None], _pack_w_in(w_in[l]), gain, cos128, sin128, blockdiag)
        y_na = _na_attention(qk, plain, bias_tabs[l])
        sink_col = jnp.repeat(swa_sink[l].reshape(SWA_KV_HEADS, G), SWA_BLOCK, axis=1)[:, None, :]
        y_swa = _swa_attention(qk, plain, swa_band, sink_col)
        alog_row = jnp.zeros((1, LANES), F32).at[0, 8:16].set(gdn_a_log[l].reshape(-1))
        dtb_row = jnp.zeros((1, LANES), F32).at[0, 8:16].set(gdn_dt_bias[l].reshape(-1))
        qkv_n, gb, gbt = _gdn_prep(plain, ba, gdn_conv_w[l], alog_row, dtb_row, blockdiag)
        o_f = _gdn_dir(qkv_n, gb, gbt, rev=False)
        o_b = _gdn_dir(qkv_n, gb, gbt, rev=True)
        xs = _merge(xs, y_na, y_swa, o_f, o_b, plain, gate, jnp.tile(gdn_norm[l], GDN_HEADS)[None], blockdiag,
                    w_branch_na[l].astype(BF16), w_branch_swa[l].astype(BF16), w_branch_gdn[l].astype(BF16),
                    w_out[l].astype(BF16))
        xs = _ffn(xs, ffn_norm[l][None], w_up[l].astype(BF16), ffn_conv_w[l], ffn_conv_b[l][None],
                  w_down[l].astype(BF16))
    return xs[None]
```

```python
import functools

import numpy as np
import jax
import jax.numpy as jnp
from jax import lax
from jax.experimental import pallas as pl
from jax.experimental.pallas import tpu as pltpu

F32 = jnp.float32
BF16 = jnp.bfloat16

D_MODEL = 1024
HEAD_DIM = 64
GRID_W = 64
NORM_EPS = 1e-6
NA_HEADS = 4
NA_KH = 8
NA_KW = 16
SWA_Q_HEADS = 8
SWA_KV_HEADS = 2
SWA_BLOCK = 128
ROPE_THETA = 10000.0
GDN_HEADS = 4
GDN_CHUNK = 64
D_FF = 2816

LANES = 128
NEG_BIG = -1e30
VMEM_LIMIT = 56 * 1024 * 1024

SEG_QK = 1152
SEG_PLAIN = 1408
SEG_GATE = 3072
SEG_BA = 128
IN_PACKED = SEG_QK + SEG_PLAIN + SEG_GATE + SEG_BA

TM = 512
NA_ROWS = 8
SWA_QB = 4
GDN_G = 4
FFN_NC = 1408
HALO = 8


def _cparams(sem):
    return pltpu.CompilerParams(dimension_semantics=sem, vmem_limit_bytes=VMEM_LIMIT)


def _dot(a, b):
    return jnp.dot(a, b, preferred_element_type=F32)


def _sigmoid(x):
    return 1.0 / (1.0 + jnp.exp(-x))


def _group_sum(sq, bd):
    hi = sq.astype(BF16)
    lo = (sq - hi.astype(F32)).astype(BF16)
    return _dot(hi, bd) + _dot(lo, bd)


def _inproj_kernel(x_ref, g_ref, w_ref, gain_ref, cos_ref, sin_ref, bd_ref,
                   oqk_ref, oplain_ref, ogate_ref, oba_ref):
    x = x_ref[...]
    ms = jnp.mean(x * x, axis=-1, keepdims=True)
    h = (x * lax.rsqrt(ms + NORM_EPS) * g_ref[...]).astype(BF16)
    bd = bd_ref[...]
    lane = lax.broadcasted_iota(jnp.int32, (1, LANES), 1)
    first_half = (lane % HEAD_DIM) < (HEAD_DIM // 2)
    cos = cos_ref[...]
    sin = sin_ref[...]

    t = _dot(h, w_ref[:, 0:SEG_QK])
    for c in range(SEG_QK // LANES):
        tc = t[:, c * LANES:(c + 1) * LANES]
        ss = _group_sum(tc * tc, bd)
        y = tc * lax.rsqrt(ss * (1.0 / HEAD_DIM) + NORM_EPS) * gain_ref[:, c * LANES:(c + 1) * LANES]
        if c >= 4:
            rot = jnp.where(first_half, pltpu.roll(y, LANES - HEAD_DIM // 2, 1), pltpu.roll(y, HEAD_DIM // 2, 1))
            y = y * cos + rot * sin
        oqk_ref[:, c * LANES:(c + 1) * LANES] = y.astype(BF16)

    o = SEG_QK
    oplain_ref[...] = _dot(h, w_ref[:, o:o + SEG_PLAIN]).astype(BF16)
    o += SEG_PLAIN
    for c in range(3):
        tg = _dot(h, w_ref[:, o + c * D_MODEL:o + (c + 1) * D_MODEL])
        ogate_ref[:, c * D_MODEL:(c + 1) * D_MODEL] = _sigmoid(tg).astype(BF16)
    o += SEG_GATE
    oba_ref[...] = _dot(h, w_ref[:, o:o + SEG_BA])


def _layer_spec(a, l, **kw):
    nd = a.ndim - 1
    return pl.BlockSpec((None,) + a.shape[1:], lambda *_: (l,) + (0,) * nd, **kw)


def _const_spec(a):
    return pl.BlockSpec(a.shape, lambda *_: (0,) * a.ndim)


def _inproj(x, l, g, w, gain, cos, sin, bd):
    L = x.shape[0]
    row = lambda w_: pl.BlockSpec((TM, w_), lambda i: (i, 0))
    return pl.pallas_call(
        _inproj_kernel,
        grid=(L // TM,),
        in_specs=[row(D_MODEL), _layer_spec(g, l), _layer_spec(w, l, pipeline_mode=pl.Buffered(1)),
                  _layer_spec(gain, l), row(LANES), row(LANES), _const_spec(bd)],
        out_specs=[row(SEG_QK), row(SEG_PLAIN), row(SEG_GATE), row(SEG_BA)],
        out_shape=[jax.ShapeDtypeStruct((L, SEG_QK), BF16),
                   jax.ShapeDtypeStruct((L, SEG_PLAIN), BF16),
                   jax.ShapeDtypeStruct((L, SEG_GATE), BF16),
                   jax.ShapeDtypeStruct((L, SEG_BA), F32)],
        compiler_params=_cparams(("parallel",)),
        name="inproj",
    )(x, g, w, gain, cos, sin, bd)


def _toeplitz_kernel(rpb_ref, onehot_ref, colmask_ref, o_ref):
    r = rpb_ref[...]
    hi = r.astype(BF16)
    r1 = r - hi.astype(F32)
    mid = r1.astype(BF16)
    lo = (r1 - mid.astype(F32)).astype(BF16)
    oh = onehot_ref[...]
    w = _dot(hi, oh) + _dot(mid, oh) + _dot(lo, oh)
    o_ref[...] = jnp.where(colmask_ref[...] > 0.0, w, NEG_BIG)


def _na_assemble_kernel(tiles_ref, o_ref, *, a_idx):
    neg = jnp.full((GRID_W, GRID_W), NEG_BIG, F32)
    for t in range(a_idx.shape[0]):
        for j in range(NA_KH + 1):
            for e in range(2):
                a = int(a_idx[t, j, e])
                o_ref[t, j * GRID_W:(j + 1) * GRID_W, e * GRID_W:(e + 1) * GRID_W] = neg if a < 0 else tiles_ref[a]


def _na_bias_tables(na_rpb, rows):
    depth, H, nr, nc = na_rpb.shape
    kc = np.arange(GRID_W)[:, None]
    qc = np.arange(GRID_W)[None, :]
    dc = np.clip(kc - qc + (NA_KW - 1), 0, 2 * NA_KW - 2).reshape(-1)
    onehot = np.zeros((32, GRID_W * GRID_W), np.float32)
    onehot[dc, np.arange(GRID_W * GRID_W)] = 1.0
    col_start = np.clip(qc - NA_KW // 2, 0, GRID_W - NA_KW)
    colmask = ((kc >= col_start) & (kc < col_start + NA_KW)).astype(np.float32).reshape(1, -1)
    rpb_rows = jnp.pad(na_rpb, ((0, 0), (0, 0), (0, 16 - nr), (0, 32 - nc))).reshape(depth * H * 16, 32)
    tiles = pl.pallas_call(
        _toeplitz_kernel,
        out_shape=jax.ShapeDtypeStruct((depth * H * 16, GRID_W * GRID_W), F32),
        name="na_bias_tiles",
    )(rpb_rows, jnp.asarray(onehot, BF16), jnp.asarray(colmask))
    tiles = tiles.reshape(depth * H, 16, GRID_W, GRID_W)
    a_idx = np.stack([_na_pair_structure(r0, rows)[1] for r0 in _na_pair_type_rows(rows)])
    nt = a_idx.shape[0]
    tabs = pl.pallas_call(
        functools.partial(_na_assemble_kernel, a_idx=a_idx),
        grid=(depth * H,),
        in_specs=[pl.BlockSpec((None, 16, GRID_W, GRID_W), lambda i: (i, 0, 0, 0))],
        out_specs=pl.BlockSpec((None, nt, (NA_KH + 1) * GRID_W, 2 * GRID_W), lambda i: (i, 0, 0, 0)),
        out_shape=jax.ShapeDtypeStruct((depth * H, nt, (NA_KH + 1) * GRID_W, 2 * GRID_W), F32),
        compiler_params=_cparams(("parallel",)),
        name="na_bias_tables",
    )(tiles)
    return tabs.reshape(depth, H, nt, (NA_KH + 1) * GRID_W, 2 * GRID_W)


def _na_pair_structure(r0, rows):
    wstart = int(np.clip(r0 - NA_KH // 2, 0, rows - NA_KH - 1))
    a_idx = np.full((NA_KH + 1, 2), -1, np.int64)
    for e in range(2):
        rr = r0 + e
        rs = int(np.clip(rr - NA_KH // 2, 0, rows - NA_KH))
        for j in range(NA_KH + 1):
            krow = wstart + j
            if rs <= krow < rs + NA_KH:
                a_idx[j, e] = krow - rr + NA_KH - 1
    return wstart, a_idx


def _na_pair_type_rows(rows):
    reps = [0, 2, 4, rows - 4, rows - 2]
    for r0 in range(0, rows, 2):
        t = 3 + (r0 - (rows - 4)) // 2 if r0 >= rows - 4 else min(r0 // 2, 2)
        ws, a = _na_pair_structure(r0, rows)
        ws_t, a_t = _na_pair_structure(reps[t], rows)
        assert (a == a_t).all() and r0 - ws == reps[t] - ws_t
    return reps


def _na_kernel(q_ref, kp_ref, kc_ref, kn_ref, vp_ref, vc_ref, vn_ref, bias_ref, o_ref, kbuf, vbuf, *, rows):
    i = pl.program_id(0)
    blk = NA_ROWS * GRID_W
    kbuf[0:blk, :] = kp_ref[...]
    kbuf[blk:2 * blk, :] = kc_ref[...]
    kbuf[2 * blk:3 * blk, :] = kn_ref[...]
    vbuf[0:blk, :] = vp_ref[...]
    vbuf[blk:2 * blk, :] = vc_ref[...]
    vbuf[2 * blk:3 * blk, :] = vn_ref[...]
    win = (NA_KH + 1) * GRID_W
    pair = 2 * GRID_W

    for pp in range(NA_ROWS // 2):
        r0 = i * NA_ROWS + 2 * pp
        wstart = jnp.clip(r0 - NA_KH // 2, 0, rows - NA_KH - 1)
        t = jnp.where(r0 >= rows - 4, 3 + (r0 - (rows - 4)) // 2, jnp.minimum(r0 // 2, 2))
        off = pl.multiple_of((wstart - (i - 1) * NA_ROWS) * GRID_W, GRID_W)
        kw = kbuf[pl.ds(off, win), :]
        vw = vbuf[pl.ds(off, win), :]
        q = q_ref[pp * pair:(pp + 1) * pair, :]
        outs = []
        for h in range(NA_HEADS):
            sl = slice(h * HEAD_DIM, (h + 1) * HEAD_DIM)
            s = lax.dot_general(kw[:, sl], q[:, sl], (((1,), (1,)), ((), ())), preferred_element_type=F32)
            s = s + bias_ref[h, t]
            m = jnp.max(s, axis=0, keepdims=True)
            p = jnp.exp(s - m)
            l = jnp.sum(p, axis=0, keepdims=True)
            pn = (p * (1.0 / l)).astype(BF16)
            outs.append(lax.dot_general(pn, vw[:, sl], (((0,), (0,)), ((), ())), preferred_element_type=F32))
        o_ref[pp * pair:(pp + 1) * pair, :] = jnp.concatenate(outs, axis=-1).astype(BF16)


def _na_attention(qk, plain, l, bias):
    L = qk.shape[0]
    rows = L // GRID_W
    nblk = rows // NA_ROWS
    blk = NA_ROWS * GRID_W
    w = NA_HEADS * HEAD_DIM
    prev = lambda i: jnp.maximum(i - 1, 0)
    nxt = lambda i: jnp.minimum(i + 1, nblk - 1)
    spec = lambda f, c: pl.BlockSpec((blk, w), lambda i: (f(i), c))
    same = lambda i: i
    return pl.pallas_call(
        functools.partial(_na_kernel, rows=rows),
        grid=(nblk,),
        in_specs=[spec(same, 0),
                  spec(prev, 1), spec(same, 1), spec(nxt, 1),
                  spec(prev, 3), spec(same, 3), spec(nxt, 3),
                  _layer_spec(bias, l)],
        out_specs=pl.BlockSpec((blk, w), lambda i: (i, 0)),
        out_shape=jax.ShapeDtypeStruct((L, w), BF16),
        scratch_shapes=[pltpu.VMEM((3 * blk, w), BF16), pltpu.VMEM((3 * blk, w), BF16)],
        compiler_params=_cparams(("parallel",)),
        name="na_attn",
    )(qk, qk, qk, qk, plain, plain, plain, bias)


def _swa_kernel(q_ref, kp_ref, kc_ref, kn_ref, vp_ref, vc_ref, vn_ref, band_ref, sink_ref, o_ref, *, nstep):
    i = pl.program_id(0)
    B = SWA_BLOCK
    G = SWA_Q_HEADS // SWA_KV_HEADS
    k_all = jnp.concatenate([kp_ref[...], kc_ref[...], kn_ref[...]], axis=0)
    v_all = jnp.concatenate([vp_ref[...], vc_ref[...], vn_ref[...]], axis=0)
    for b in range(SWA_QB):
        if b == 0:
            t = jnp.where(i == 0, 1, 0)
        elif b == SWA_QB - 1:
            t = jnp.where(i == nstep - 1, 2, 0)
        else:
            t = 0
        band = band_ref[t]
        for g in range(SWA_KV_HEADS):
            ks = k_all[b * B:(b + 3) * B, g * HEAD_DIM:(g + 1) * HEAD_DIM]
            vs = v_all[b * B:(b + 3) * B, g * HEAD_DIM:(g + 1) * HEAD_DIM]
            qs = jnp.concatenate([q_ref[b * B:(b + 1) * B, (g * G + hh) * HEAD_DIM:(g * G + hh + 1) * HEAD_DIM]
                                  for hh in range(G)], axis=0)
            s = lax.dot_general(ks, qs, (((1,), (1,)), ((), ())), preferred_element_type=F32) + band
            sink = sink_ref[g]
            m = jnp.maximum(jnp.max(s, axis=0, keepdims=True), sink)
            p = jnp.exp(s - m)
            l = jnp.sum(p, axis=0, keepdims=True) + jnp.exp(sink - m)
            pn = (p * (1.0 / l)).astype(BF16)
            o = lax.dot_general(pn, vs, (((0,), (0,)), ((), ())), preferred_element_type=F32)
            for hh in range(G):
                h = g * G + hh
                o_ref[b * B:(b + 1) * B, h * HEAD_DIM:(h + 1) * HEAD_DIM] = o[hh * B:(hh + 1) * B].astype(BF16)


def _swa_band_tables():
    B = SWA_BLOCK
    G = SWA_Q_HEADS // SWA_KV_HEADS
    qi = np.arange(G * B)[None, :] % B
    kj = np.arange(3 * B)[:, None]
    band = np.abs(kj - B - qi) <= B
    tabs = [band, band & (kj >= B), band & (kj < 2 * B)]
    return jnp.asarray(np.where(np.stack(tabs), 0.0, NEG_BIG), F32)


def _swa_attention(qk, plain, l, band, sink_row):
    L = qk.shape[0]
    B = SWA_BLOCK
    T = SWA_QB * B
    nstep = L // T
    nblk = L // B
    prev = lambda i: jnp.maximum(i * SWA_QB - 1, 0)
    nxt = lambda i: jnp.minimum((i + 1) * SWA_QB, nblk - 1)
    return pl.pallas_call(
        functools.partial(_swa_kernel, nstep=nstep),
        grid=(nstep,),
        in_specs=[pl.BlockSpec((T, 512), lambda i: (i, 1)),
                  pl.BlockSpec((B, LANES), lambda i: (prev(i), 8)),
                  pl.BlockSpec((T, LANES), lambda i: (i, 8)),
                  pl.BlockSpec((B, LANES), lambda i: (nxt(i), 8)),
                  pl.BlockSpec((B, LANES), lambda i: (prev(i), 10)),
                  pl.BlockSpec((T, LANES), lambda i: (i, 10)),
                  pl.BlockSpec((B, LANES), lambda i: (nxt(i), 10)),
                  _const_spec(band), _layer_spec(sink_row, l)],
        out_specs=pl.BlockSpec((T, 512), lambda i: (i, 0)),
        out_shape=jax.ShapeDtypeStruct((L, 512), BF16),
        compiler_params=_cparams(("parallel",)),
        name="swa_attn",
    )(qk, qk, qk, qk, plain, plain, plain, band, sink_row)


def _gdn_prep_kernel(x_ref, hp_ref, hn_ref, cw_ref, ba_ref, alog_ref, dtb_ref, bd_ref,
                     qkv_ref, gb_ref, gbt_ref, *, nblk):
    i = pl.program_id(0)
    x = x_ref[...].astype(F32)
    rid = lax.broadcasted_iota(jnp.int32, (TM, 1), 0)
    hp = jnp.where(i > 0, hp_ref[HALO - 1:HALO, :].astype(F32), 0.0)
    hn = jnp.where(i < nblk - 1, hn_ref[0:1, :].astype(F32), 0.0)
    xprev = jnp.where(rid == 0, hp, pltpu.roll(x, 1, 0))
    xnext = jnp.where(rid == TM - 1, hn, pltpu.roll(x, TM - 1, 0))
    y = xprev * cw_ref[0:1, :] + x * cw_ref[1:2, :] + xnext * cw_ref[2:3, :]
    y = y * _sigmoid(y)
    bd = bd_ref[...]
    for c in range(4):
        yc = y[:, c * LANES:(c + 1) * LANES]
        n = yc * lax.rsqrt(_group_sum(yc * yc, bd) + NORM_EPS)
        if c < 2:
            n = n * (HEAD_DIM ** -0.5)
        qkv_ref[:, c * LANES:(c + 1) * LANES] = n.astype(BF16)
    qkv_ref[:, 4 * LANES:6 * LANES] = y[:, 4 * LANES:6 * LANES].astype(BF16)

    ba = ba_ref[...]
    lane = lax.broadcasted_iota(jnp.int32, (1, LANES), 1)
    beta = _sigmoid(ba)
    sp_in = ba + dtb_ref[...]
    softplus = jnp.maximum(sp_in, 0.0) + jnp.log(1.0 + jnp.exp(-jnp.abs(sp_in)))
    g = jnp.where((lane >= 8) & (lane < 16), -jnp.exp(alog_ref[...]) * softplus, 0.0)
    rc = rid % GDN_CHUNK
    pre = g
    suf = g
    s = 1
    while s < GDN_CHUNK:
        pre = pre + jnp.where(rc >= s, pltpu.roll(pre, s, 0), 0.0)
        suf = suf + jnp.where(rc < GDN_CHUNK - s, pltpu.roll(suf, TM - s, 0), 0.0)
        s *= 2
    tot = pre + suf - g
    gc = jnp.where(lane < 12, pre, suf)
    slab = jnp.where(lane < 8, beta, jnp.where(lane < 16, gc, jnp.where(lane < 24, pltpu.roll(tot, 8, 1), 0.0)))
    gb_ref[...] = slab
    gbt_ref[...] = slab.T[0:24, :]


def _gdn_prep(plain, ba, l, conv_w, alog_row, dtb_row, bd):
    L = plain.shape[0]
    nblk = L // TM
    hb = TM // HALO
    return pl.pallas_call(
        functools.partial(_gdn_prep_kernel, nblk=nblk),
        grid=(nblk,),
        in_specs=[pl.BlockSpec((TM, 768), lambda i: (i, 0)),
                  pl.BlockSpec((HALO, 768), lambda i: (jnp.maximum(i * hb - 1, 0), 0)),
                  pl.BlockSpec((HALO, 768), lambda i: (jnp.minimum((i + 1) * hb, L // HALO - 1), 0)),
                  _layer_spec(conv_w, l),
                  pl.BlockSpec((TM, LANES), lambda i: (i, 0)),
                  _layer_spec(alog_row, l), _layer_spec(dtb_row, l), _const_spec(bd)],
        out_specs=[pl.BlockSpec((TM, 768), lambda i: (i, 0)),
                   pl.BlockSpec((TM, LANES), lambda i: (i, 0)),
                   pl.BlockSpec((24, TM), lambda i: (0, i))],
        out_shape=[jax.ShapeDtypeStruct((L, 768), BF16),
                   jax.ShapeDtypeStruct((L, LANES), F32),
                   jax.ShapeDtypeStruct((24, L), F32)],
        compiler_params=_cparams(("parallel",)),
        name="gdn_prep",
    )(plain, plain, plain, conv_w, ba, alog_row, dtb_row, bd)


def _bmm(a, b):
    return jnp.einsum('bij,bjk->bik', a.astype(BF16), b.astype(BF16), preferred_element_type=F32)


def _bmm_nt(a, b):
    return jnp.einsum('bid,bjd->bij', a.astype(BF16), b.astype(BF16), preferred_element_type=F32)


def _gdn_chunk_terms(qkv, gb, gbt, rev):
    G, C, H = GDN_G, GDN_CHUNK, GDN_HEADS

    def heads(x):
        return jnp.concatenate([x[:, h * HEAD_DIM:(h + 1) * HEAD_DIM].reshape(G, C, HEAD_DIM)
                                for h in range(H)], axis=0)

    q = heads(qkv[:, 0:256])
    k = heads(qkv[:, 256:512])
    v = heads(qkv[:, 512:768]).astype(F32)
    kf = k.astype(F32)
    qf = q.astype(F32)
    d0 = H if rev else 0

    def colv(base):
        return jnp.concatenate([gb[:, base + h:base + h + 1].reshape(G, C, 1) for h in range(H)], axis=0)

    beta = colv(d0)
    gcc = colv(8 + d0)
    gl = colv(16 + d0)
    grow = jnp.concatenate([gbt[8 + d0 + h:9 + d0 + h, c * C:(c + 1) * C].reshape(1, 1, C)
                            for h in range(H) for c in range(G)], axis=0)
    ii = lax.broadcasted_iota(jnp.int32, (1, C, C), 1)
    jj = lax.broadcasted_iota(jnp.int32, (1, C, C), 2)
    incl = (jj >= ii) if rev else (jj <= ii)
    strict = (jj > ii) if rev else (jj < ii)
    decay = jnp.exp(jnp.where(incl, gcc - grow, NEG_BIG))
    kk = _bmm_nt(k, k)
    qk = _bmm_nt(q, k)
    nmat = jnp.where(strict, kk * decay, 0.0) * beta
    eg = jnp.exp(gcc)
    rhs = jnp.concatenate([v * beta, kf * (beta * eg)], axis=-1)
    m = _bmm(nmat, jnp.concatenate([rhs, nmat], axis=-1))
    x = rhs - m[:, :, 0:2 * HEAD_DIM]
    p = m[:, :, 2 * HEAD_DIM:]
    for _ in range(4):
        m = _bmm(p, jnp.concatenate([x, p], axis=-1))
        x = x + m[:, :, 0:2 * HEAD_DIM]
        p = m[:, :, 2 * HEAD_DIM:]
    x = x + _bmm(p, x)
    u = x[:, :, 0:HEAD_DIM]
    w = x[:, :, HEAD_DIM:2 * HEAD_DIM]
    qkm = jnp.where(incl, qk * decay, 0.0)
    k_tail_t = jnp.swapaxes(kf * jnp.exp(gl - gcc), 1, 2)
    lhs1 = jnp.concatenate([w, qf * eg], axis=1).astype(BF16)
    lhs2 = jnp.concatenate([qkm, k_tail_t], axis=1).astype(BF16)
    dch = jnp.exp(gl[:, 0:1, :])
    return u, lhs1, lhs2, dch


def _gdn_scan_group(u_ref, l1_ref, l2_ref, d_ref, s_ref, o_ref, rev):
    G, C, H = GDN_G, GDN_CHUNK, GDN_HEADS
    S = s_ref[...]
    for c in (range(G - 1, -1, -1) if rev else range(G)):
        r1 = _bmm(l1_ref[:, c], S)
        vn = u_ref[:, c] - r1[:, 0:C]
        r2 = _bmm(l2_ref[:, c], vn)
        oc = r1[:, C:2 * C] + r2[:, 0:C]
        S = S * d_ref[:, c][:, 0:1, 0:HEAD_DIM] + r2[:, C:2 * C]
        o_ref[c * C:(c + 1) * C, :] = jnp.concatenate([oc[h] for h in range(H)], axis=-1)
    s_ref[...] = S


def _gdn_kernel(qkvf_ref, gbf_ref, gbtf_ref, qkvb_ref, gbb_ref, gbtb_ref, of_ref, ob_ref,
                sf_ref, sb_ref, uf_ref, l1f_ref, l2f_ref, df_ref, ub_ref, l1b_ref, l2b_ref, db_ref):
    G, C, H = GDN_G, GDN_CHUNK, GDN_HEADS
    step = pl.program_id(0)

    @pl.when(step == 0)
    def _():
        for r in (sf_ref, sb_ref, uf_ref, l1f_ref, l2f_ref, df_ref, ub_ref, l1b_ref, l2b_ref, db_ref):
            r[...] = jnp.zeros_like(r)

    _gdn_scan_group(uf_ref, l1f_ref, l2f_ref, df_ref, sf_ref, of_ref, False)
    _gdn_scan_group(ub_ref, l1b_ref, l2b_ref, db_ref, sb_ref, ob_ref, True)
    terms_f = _gdn_chunk_terms(qkvf_ref[...], gbf_ref[...], gbtf_ref[...], False)
    terms_b = _gdn_chunk_terms(qkvb_ref[...], gbb_ref[...], gbtb_ref[...], True)
    for (u, l1, l2, dch), (u_ref, l1_ref, l2_ref, d_ref) in ((terms_f, (uf_ref, l1f_ref, l2f_ref, df_ref)),
                                                          (terms_b, (ub_ref, l1b_ref, l2b_ref, db_ref))):
        u_ref[...] = u.reshape(H, G, C, HEAD_DIM)
        l1_ref[...] = l1.reshape(H, G, 2 * C, HEAD_DIM)
        l2_ref[...] = l2.reshape(H, G, 2 * C, C)
        d_ref[...] = jnp.broadcast_to(dch, (H * G, 8, LANES)).reshape(H, G, 8, LANES)


def _gdn(qkv, gb, gbt):
    L = qkv.shape[0]
    G, C, H = GDN_G, GDN_CHUNK, GDN_HEADS
    T = G * C
    n = L // T
    fin = lambda s: jnp.minimum(s, n - 1)
    bin_ = lambda s: jnp.maximum(n - 1 - s, 0)
    fout = lambda s: jnp.maximum(s - 1, 0)
    bout = lambda s: jnp.minimum(n - s, n - 1)
    term_scratch = [pltpu.VMEM((H, G, C, HEAD_DIM), F32), pltpu.VMEM((H, G, 2 * C, HEAD_DIM), BF16),
                    pltpu.VMEM((H, G, 2 * C, C), BF16), pltpu.VMEM((H, G, 8, LANES), F32)]
    return pl.pallas_call(
        _gdn_kernel,
        grid=(n + 1,),
        in_specs=[pl.BlockSpec((T, 768), lambda s: (fin(s), 0)),
                  pl.BlockSpec((T, LANES), lambda s: (fin(s), 0)),
                  pl.BlockSpec((24, T), lambda s: (0, fin(s))),
                  pl.BlockSpec((T, 768), lambda s: (bin_(s), 0)),
                  pl.BlockSpec((T, LANES), lambda s: (bin_(s), 0)),
                  pl.BlockSpec((24, T), lambda s: (0, bin_(s)))],
        out_specs=[pl.BlockSpec((T, 256), lambda s: (fout(s), 0)),
                   pl.BlockSpec((T, 256), lambda s: (bout(s), 0))],
        out_shape=[jax.ShapeDtypeStruct((L, 256), F32), jax.ShapeDtypeStruct((L, 256), F32)],
        scratch_shapes=[pltpu.VMEM((H, HEAD_DIM, HEAD_DIM), F32), pltpu.VMEM((H, HEAD_DIM, HEAD_DIM), F32)]
                       + term_scratch + term_scratch,
        compiler_params=_cparams(("arbitrary",)),
        name="gdn_scan",
    )(qkv, gb, gbt, qkv, gb, gbt)


def _merge_kernel(x_ref, yna_ref, yswa_ref, of_ref, ob_ref, z_ref, gate_ref, gn_ref, bd_ref,
                  wna_ref, wswa_ref, wgdn_ref, wout_ref, o_ref):
    bd = bd_ref[...]
    o = of_ref[...] + ob_ref[...]
    z = z_ref[...].astype(F32)
    parts = []
    for c in range(2):
        oc = o[:, c * LANES:(c + 1) * LANES]
        ms = _group_sum(oc * oc, bd) * (1.0 / HEAD_DIM)
        parts.append(oc * lax.rsqrt(ms + NORM_EPS))
    ygdn = jnp.concatenate(parts, axis=-1) * gn_ref[...] * (z * _sigmoid(z))
    m = (gate_ref[:, 0:D_MODEL].astype(F32) * _dot(yna_ref[...], wna_ref[...])
         + gate_ref[:, D_MODEL:2 * D_MODEL].astype(F32) * _dot(yswa_ref[...], wswa_ref[...])
         + gate_ref[:, 2 * D_MODEL:3 * D_MODEL].astype(F32) * _dot(ygdn.astype(BF16), wgdn_ref[...]))
    o_ref[...] = x_ref[...] + _dot(m.astype(BF16), wout_ref[...])


def _merge(x, yna, yswa, of, ob, plain, gate, l, gn, bd, wna, wswa, wgdn, wout):
    L = x.shape[0]
    row = lambda w_, c=0: pl.BlockSpec((TM, w_), lambda i: (i, c))
    return pl.pallas_call(
        _merge_kernel,
        grid=(L // TM,),
        in_specs=[row(D_MODEL), row(256), row(512), row(256), row(256), row(256, 4), row(SEG_GATE),
                  _layer_spec(gn, l), _const_spec(bd), _layer_spec(wna, l), _layer_spec(wswa, l),
                  _layer_spec(wgdn, l), _layer_spec(wout, l)],
        out_specs=row(D_MODEL),
        out_shape=jax.ShapeDtypeStruct((L, D_MODEL), F32),
        compiler_params=_cparams(("parallel",)),
        name="merge",
    )(x, yna, yswa, of, ob, plain, gate, gn, bd, wna, wswa, wgdn, wout)


def _ffn_kernel(x_ref, xp_ref, xn_ref, g_ref, wa_ref, wb_ref, cwa_ref, cwb_ref, ba_ref, bb_ref, wd_ref,
                o_ref, hbuf, acc, *, nblk):
    i = pl.program_id(0)
    j = pl.program_id(1)

    def normed(xv):
        ms = jnp.mean(xv * xv, axis=-1, keepdims=True)
        return xv * lax.rsqrt(ms + NORM_EPS) * g_ref[...]

    @pl.when(j == 0)
    def _():
        hbuf[0:HALO, :] = jnp.where(i > 0, normed(xp_ref[...]), 0.0).astype(BF16)
        hbuf[HALO:HALO + TM, :] = normed(x_ref[...]).astype(BF16)
        hbuf[HALO + TM:, :] = jnp.where(i < nblk - 1, normed(xn_ref[...]), 0.0).astype(BF16)
        acc[...] = jnp.zeros_like(acc)

    h = hbuf[...]
    n = TM + 2 * HALO

    def conv(w_ref, cw_ref, b_ref):
        u = _dot(h, w_ref[...])
        y = pltpu.roll(u, 1, 0) * cw_ref[0:1, :] + u * cw_ref[1:2, :] + pltpu.roll(u, n - 1, 0) * cw_ref[2:3, :]
        return y[HALO:HALO + TM] + b_ref[...]

    a = conv(wa_ref, cwa_ref, ba_ref)
    b = conv(wb_ref, cwb_ref, bb_ref)
    gated = (a * _sigmoid(a) * b).astype(BF16)
    acc[...] += _dot(gated, wd_ref[...])

    @pl.when(j == pl.num_programs(1) - 1)
    def _():
        o_ref[...] = x_ref[...] + acc[...]


def _ffn(x, l, g, w_up, conv_w, conv_b, w_down):
    L = x.shape[0]
    nblk = L // TM
    nj = D_FF // FFN_NC
    hb = TM // HALO
    return pl.pallas_call(
        functools.partial(_ffn_kernel, nblk=nblk),
        grid=(nblk, nj),
        in_specs=[pl.BlockSpec((TM, D_MODEL), lambda i, j: (i, 0)),
                  pl.BlockSpec((HALO, D_MODEL), lambda i, j: (jnp.maximum(i * hb - 1, 0), 0)),
                  pl.BlockSpec((HALO, D_MODEL), lambda i, j: (jnp.minimum((i + 1) * hb, L // HALO - 1), 0)),
                  _layer_spec(g, l),
                  pl.BlockSpec((None, D_MODEL, FFN_NC), lambda i, j: (l, 0, j)),
                  pl.BlockSpec((None, D_MODEL, FFN_NC), lambda i, j: (l, 0, nj + j)),
                  pl.BlockSpec((None, 3, FFN_NC), lambda i, j: (l, 0, j)),
                  pl.BlockSpec((None, 3, FFN_NC), lambda i, j: (l, 0, nj + j)),
                  pl.BlockSpec((None, 1, FFN_NC), lambda i, j: (l, 0, j)),
                  pl.BlockSpec((None, 1, FFN_NC), lambda i, j: (l, 0, nj + j)),
                  pl.BlockSpec((None, FFN_NC, D_MODEL), lambda i, j: (l, j, 0))],
        out_specs=pl.BlockSpec((TM, D_MODEL), lambda i, j: (i, 0)),
        out_shape=jax.ShapeDtypeStruct((L, D_MODEL), F32),
        scratch_shapes=[pltpu.VMEM((TM + 2 * HALO, D_MODEL), BF16), pltpu.VMEM((TM, D_MODEL), F32)],
        compiler_params=_cparams(("parallel", "arbitrary")),
        name="ffn",
    )(x, x, x, g, w_up, w_up, conv_w, conv_w, conv_b, conv_b, w_down)


def _pack_w_in(w):
    qa, ka, va = w[..., 0:256], w[..., 256:512], w[..., 512:768]
    qs, ks, vs = w[..., 768:1280], w[..., 1280:1408], w[..., 1408:1536]
    qkv_c, z_c = w[..., 1536:2304], w[..., 2304:2560]
    ba, gate = w[..., 2560:2576], w[..., 2576:5648]
    pad = jnp.zeros(w.shape[:-1] + (SEG_BA - 16,), w.dtype)
    return jnp.concatenate([qa, ka, qs, ks, qkv_c, va, z_c, vs, gate, ba, pad], axis=-1).astype(BF16)


def _rope_tables(L):
    inv = 1.0 / (ROPE_THETA ** (jnp.arange(0, HEAD_DIM, 2, dtype=F32) / HEAD_DIM))
    ang = jnp.arange(L, dtype=F32)[:, None] * inv[None, :]
    cos, sin = jnp.cos(ang), jnp.sin(ang)
    cos128 = jnp.concatenate([cos, cos, cos, cos], axis=1)
    sin128 = jnp.concatenate([-sin, sin, -sin, sin], axis=1)
    return cos128, sin128


def kernel(x, attn_norm, w_in, qk_norm, na_rpb, swa_sink, gdn_conv_w, gdn_a_log, gdn_dt_bias, gdn_norm,
           w_branch_na, w_branch_swa, w_branch_gdn, w_out, ffn_norm, w_up, ffn_conv_w, ffn_conv_b, w_down):
    B, L, D = x.shape
    assert B == 1 and D == D_MODEL and L % (NA_ROWS * GRID_W) == 0 and L // GRID_W >= 2 * NA_ROWS
    depth = w_in.shape[0]
    cos128, sin128 = _rope_tables(L)
    blockdiag = jnp.asarray(np.kron(np.eye(2), np.ones((HEAD_DIM, HEAD_DIM))), BF16)
    bias_tabs = _na_bias_tables(na_rpb, L // GRID_W)
    swa_band = _swa_band_tables()
    scale = HEAD_DIM ** -0.5
    G = SWA_Q_HEADS // SWA_KV_HEADS
    gain = jnp.concatenate([jnp.tile(qk_norm[:, 0] * scale, (1, NA_HEADS)), jnp.tile(qk_norm[:, 1], (1, NA_HEADS)),
                            jnp.tile(qk_norm[:, 2] * scale, (1, SWA_Q_HEADS)),
                            jnp.tile(qk_norm[:, 3], (1, SWA_KV_HEADS))], axis=1)[:, None, :]
    sink_row = jnp.repeat(swa_sink.reshape(depth, SWA_KV_HEADS, G), SWA_BLOCK, axis=2)[:, :, None, :]
    alog_row = jnp.pad(gdn_a_log.reshape(depth, 1, 8), ((0, 0), (0, 0), (8, LANES - 16)))
    dtb_row = jnp.pad(gdn_dt_bias.reshape(depth, 1, 8), ((0, 0), (0, 0), (8, LANES - 16)))
    gdn_gain = jnp.tile(gdn_norm, (1, GDN_HEADS))[:, None, :]
    attn_g = attn_norm[:, None, :]
    ffn_g = ffn_norm[:, None, :]
    ffn_b = ffn_conv_b[:, None, :]
    w_in_p = _pack_w_in(w_in)
    w_na, w_swa, w_gdn = w_branch_na.astype(BF16), w_branch_swa.astype(BF16), w_branch_gdn.astype(BF16)
    w_o, w_u, w_d = w_out.astype(BF16), w_up.astype(BF16), w_down.astype(BF16)
    xs = x[0]
    for l in range(depth):
        qk, plain, gate, ba = _inproj(xs, l, attn_g, w_in_p, gain, cos128, sin128, blockdiag)
        y_na = _na_attention(qk, plain, l, bias_tabs)
        y_swa = _swa_attention(qk, plain, l, swa_band, sink_row)
        qkv_n, gb, gbt = _gdn_prep(plain, ba, l, gdn_conv_w, alog_row, dtb_row, blockdiag)
        o_f, o_b = _gdn(qkv_n, gb, gbt)
        xs = _merge(xs, y_na, y_swa, o_f, o_b, plain, gate, l, gdn_gain, blockdiag, w_na, w_swa, w_gdn, w_o)
        xs = _ffn(xs, l, ffn_g, w_u, ffn_conv_w, ffn_b, w_d)
    return xs[None]
```

```python
import functools

import numpy as np
import jax
import jax.numpy as jnp
from jax import lax
from jax.experimental import pallas as pl
from jax.experimental.pallas import tpu as pltpu

F32 = jnp.float32
BF16 = jnp.bfloat16

D_MODEL = 1024
HEAD_DIM = 64
GRID_W = 64
NORM_EPS = 1e-6
NA_HEADS = 4
NA_KH = 8
NA_KW = 16
SWA_Q_HEADS = 8
SWA_KV_HEADS = 2
SWA_BLOCK = 128
ROPE_THETA = 10000.0
GDN_HEADS = 4
GDN_CHUNK = 64
D_FF = 2816

LANES = 128
NEG_BIG = -1e30
LOG2E = 1.4426950408889634
VMEM_LIMIT = 56 * 1024 * 1024

SEG_QK = 1152
SEG_PLAIN = 1408
PLAIN_VA = 768
PLAIN_VS = 1280
SEG_GATE = 3072
SEG_BA = 128
IN_PACKED = SEG_QK + SEG_PLAIN + SEG_GATE + SEG_BA

TM = 512
NA_ROWS = 8
NA_WIN = NA_KH + 2
SWA_QB = 4
GDN_G = 4
FFN_NC = 1408
HALO = 8


def _cparams(sem):
    return pltpu.CompilerParams(dimension_semantics=sem, vmem_limit_bytes=VMEM_LIMIT)


def _dot(a, b):
    return jnp.dot(a, b, preferred_element_type=F32)


def _sigmoid(x):
    return 1.0 / (1.0 + jnp.exp(-x))


def _group_sum(sq, bd):
    hi = sq.astype(BF16)
    lo = (sq - hi.astype(F32)).astype(BF16)
    return _dot(hi, bd) + _dot(lo, bd)


def _inproj_kernel(x_ref, g_ref, w_ref, gain_ref, cos_ref, sin_ref, bd_ref,
                   oqk_ref, oplain_ref, ogate_ref, oba_ref, ovat_ref, ovst_ref):
    x = x_ref[...]
    ms = jnp.mean(x * x, axis=-1, keepdims=True)
    h = (x * lax.rsqrt(ms + NORM_EPS) * g_ref[...]).astype(BF16)
    bd = bd_ref[...]
    lane = lax.broadcasted_iota(jnp.int32, (1, LANES), 1)
    first_half = (lane % HEAD_DIM) < (HEAD_DIM // 2)
    cos = cos_ref[...]
    sin = sin_ref[...]

    t = _dot(h, w_ref[:, 0:SEG_QK])
    for c in range(SEG_QK // LANES):
        tc = t[:, c * LANES:(c + 1) * LANES]
        ss = _group_sum(tc * tc, bd)
        y = tc * lax.rsqrt(ss * (1.0 / HEAD_DIM) + NORM_EPS) * gain_ref[:, c * LANES:(c + 1) * LANES]
        if c >= 4:
            rot = jnp.where(first_half, pltpu.roll(y, LANES - HEAD_DIM // 2, 1), pltpu.roll(y, HEAD_DIM // 2, 1))
            y = y * cos + rot * sin
        oqk_ref[:, c * LANES:(c + 1) * LANES] = y.astype(BF16)

    o = SEG_QK
    tp = _dot(h, w_ref[:, o:o + SEG_PLAIN])
    oplain_ref[...] = tp.astype(BF16)
    for grp in range(TM // LANES):
        rows = slice(grp * LANES, (grp + 1) * LANES)
        ovat_ref[grp] = tp[rows, PLAIN_VA:PLAIN_VA + 256].T.astype(BF16)
        ovst_ref[grp] = tp[rows, PLAIN_VS:PLAIN_VS + 128].T.astype(BF16)
    o += SEG_PLAIN
    for c in range(3):
        tg = _dot(h, w_ref[:, o + c * D_MODEL:o + (c + 1) * D_MODEL])
        ogate_ref[:, c * D_MODEL:(c + 1) * D_MODEL] = _sigmoid(tg).astype(BF16)
    o += SEG_GATE
    oba_ref[...] = _dot(h, w_ref[:, o:o + SEG_BA])


def _layer_spec(a, l, **kw):
    nd = a.ndim - 1
    return pl.BlockSpec((None,) + a.shape[1:], lambda *_: (l,) + (0,) * nd, **kw)


def _const_spec(a):
    return pl.BlockSpec(a.shape, lambda *_: (0,) * a.ndim)


def _inproj(x, l, g, w, gain, cos, sin, bd):
    L = x.shape[0]
    row = lambda w_: pl.BlockSpec((TM, w_), lambda i: (i, 0))
    return pl.pallas_call(
        _inproj_kernel,
        grid=(L // TM,),
        in_specs=[row(D_MODEL), _layer_spec(g, l), _layer_spec(w, l, pipeline_mode=pl.Buffered(1)),
                  _layer_spec(gain, l), row(LANES), row(LANES), _const_spec(bd)],
        out_specs=[row(SEG_QK), row(SEG_PLAIN), row(SEG_GATE), row(SEG_BA),
                   pl.BlockSpec((TM // LANES, 256, LANES), lambda i: (i, 0, 0)),
                   pl.BlockSpec((TM // LANES, 128, LANES), lambda i: (i, 0, 0))],
        out_shape=[jax.ShapeDtypeStruct((L, SEG_QK), BF16),
                   jax.ShapeDtypeStruct((L, SEG_PLAIN), BF16),
                   jax.ShapeDtypeStruct((L, SEG_GATE), BF16),
                   jax.ShapeDtypeStruct((L, SEG_BA), F32),
                   jax.ShapeDtypeStruct((L // LANES, 256, LANES), BF16),
                   jax.ShapeDtypeStruct((L // LANES, 128, LANES), BF16)],
        compiler_params=_cparams(("parallel",)),
        name="inproj",
    )(x, g, w, gain, cos, sin, bd)


def _toeplitz_kernel(rpb_ref, onehot_ref, colmask_ref, o_ref):
    r = rpb_ref[...]
    hi = r.astype(BF16)
    r1 = r - hi.astype(F32)
    mid = r1.astype(BF16)
    lo = (r1 - mid.astype(F32)).astype(BF16)
    oh = onehot_ref[...]
    w = _dot(hi, oh) + _dot(mid, oh) + _dot(lo, oh)
    o_ref[...] = jnp.where(colmask_ref[...] > 0.0, w * LOG2E, NEG_BIG)


def _na_assemble_kernel(tiles_ref, o_ref, *, a_idx):
    neg = jnp.full((GRID_W, GRID_W), NEG_BIG, F32)
    for t in range(a_idx.shape[0]):
        for j in range(NA_WIN):
            for e in range(2):
                a = int(a_idx[t, j, e])
                o_ref[t, j * GRID_W:(j + 1) * GRID_W, e * GRID_W:(e + 1) * GRID_W] = neg if a < 0 else tiles_ref[a]


def _na_bias_tables(na_rpb, rows):
    depth, H, nr, nc = na_rpb.shape
    kc = np.arange(GRID_W)[:, None]
    qc = np.arange(GRID_W)[None, :]
    dc = np.clip(kc - qc + (NA_KW - 1), 0, 2 * NA_KW - 2).reshape(-1)
    onehot = np.zeros((32, GRID_W * GRID_W), np.float32)
    onehot[dc, np.arange(GRID_W * GRID_W)] = 1.0
    col_start = np.clip(qc - NA_KW // 2, 0, GRID_W - NA_KW)
    colmask = ((kc >= col_start) & (kc < col_start + NA_KW)).astype(np.float32).reshape(1, -1)
    rpb_rows = jnp.pad(na_rpb, ((0, 0), (0, 0), (0, 16 - nr), (0, 32 - nc))).reshape(depth * H * 16, 32)
    tiles = pl.pallas_call(
        _toeplitz_kernel,
        out_shape=jax.ShapeDtypeStruct((depth * H * 16, GRID_W * GRID_W), F32),
        name="na_bias_tiles",
    )(rpb_rows, jnp.asarray(onehot, BF16), jnp.asarray(colmask))
    tiles = tiles.reshape(depth * H, 16, GRID_W, GRID_W)
    a_idx = np.stack([_na_pair_structure(r0, rows)[1] for r0 in _na_pair_type_rows(rows)])
    nt = a_idx.shape[0]
    tabs = pl.pallas_call(
        functools.partial(_na_assemble_kernel, a_idx=a_idx),
        grid=(depth * H,),
        in_specs=[pl.BlockSpec((None, 16, GRID_W, GRID_W), lambda i: (i, 0, 0, 0))],
        out_specs=pl.BlockSpec((None, nt, NA_WIN * GRID_W, 2 * GRID_W), lambda i: (i, 0, 0, 0)),
        out_shape=jax.ShapeDtypeStruct((depth * H, nt, NA_WIN * GRID_W, 2 * GRID_W), F32),
        compiler_params=_cparams(("parallel",)),
        name="na_bias_tables",
    )(tiles)
    return tabs.reshape(depth, H, nt, NA_WIN * GRID_W, 2 * GRID_W)


def _na_pair_structure(r0, rows):
    wstart = int(np.clip(r0 - NA_KH // 2, 0, rows - NA_KH - 1)) // 2 * 2
    a_idx = np.full((NA_WIN, 2), -1, np.int64)
    for e in range(2):
        rr = r0 + e
        rs = int(np.clip(rr - NA_KH // 2, 0, rows - NA_KH))
        for j in range(NA_WIN):
            krow = wstart + j
            if rs <= krow < rs + NA_KH:
                a_idx[j, e] = krow - rr + NA_KH - 1
    return wstart, a_idx


def _na_pair_type_rows(rows):
    reps = [0, 2, 4, rows - 4, rows - 2]
    for r0 in range(0, rows, 2):
        t = 3 + (r0 - (rows - 4)) // 2 if r0 >= rows - 4 else min(r0 // 2, 2)
        ws, a = _na_pair_structure(r0, rows)
        ws_t, a_t = _na_pair_structure(reps[t], rows)
        assert (a == a_t).all() and r0 - ws == reps[t] - ws_t
    return reps


def _na_kernel(q_ref, kp_ref, kc_ref, kn_ref, vp_ref, vc_ref, vn_ref, bias_ref, o_ref, kbuf, vbuf, *, rows):
    i = pl.program_id(0)
    blk = NA_ROWS * GRID_W
    grp = blk // LANES
    kbuf[0:blk, :] = kp_ref[...]
    kbuf[blk:2 * blk, :] = kc_ref[...]
    kbuf[2 * blk:3 * blk, :] = kn_ref[...]
    vbuf[0:grp] = vp_ref[...]
    vbuf[grp:2 * grp] = vc_ref[...]
    vbuf[2 * grp:3 * grp] = vn_ref[...]
    win = NA_WIN * GRID_W
    wgrp = win // LANES
    pair = 2 * GRID_W

    npair = NA_ROWS // 2
    goffs, types = [], []
    for pp in range(npair):
        r0 = i * NA_ROWS + 2 * pp
        wstart = jnp.clip(r0 - NA_KH // 2, 0, rows - NA_KH - 1) // 2 * 2
        types.append(jnp.where(r0 >= rows - 4, 3 + (r0 - (rows - 4)) // 2, jnp.minimum(r0 // 2, 2)))
        goffs.append((wstart - (i - 1) * NA_ROWS) // 2)

    def logits(pp, h):
        sl = slice(h * HEAD_DIM, (h + 1) * HEAD_DIM)
        kw = kbuf[pl.ds(pl.multiple_of(goffs[pp] * LANES, LANES), win), sl]
        s = lax.dot_general(kw, q_ref[pp * pair:(pp + 1) * pair, sl],
                            (((1,), (1,)), ((), ())), preferred_element_type=F32)
        return s + bias_ref[h, types[pp]]

    def softmax(s):
        m = jnp.max(s, axis=0, keepdims=True)
        p = jnp.exp2(s - m)
        return p.astype(BF16), 1.0 / jnp.sum(p, axis=0, keepdims=True)

    def weighted(p_inv, pp, h):
        p, inv_l = p_inv
        vt = jnp.concatenate([vbuf[goffs[pp] + j, h * HEAD_DIM:(h + 1) * HEAD_DIM, :] for j in range(wgrp)], axis=-1)
        return _dot(vt, p) * inv_l

    tiles = [(pp, h) for pp in range(npair) for h in range(NA_HEADS)]
    s_q, p_q, outs = {}, {}, {}
    for n in range(len(tiles) + 2):
        if n < len(tiles):
            s_q[n] = logits(*tiles[n])
        if 1 <= n <= len(tiles):
            p_q[n - 1] = softmax(s_q.pop(n - 1))
        if n >= 2:
            pp, h = tiles[n - 2]
            outs[(pp, h)] = weighted(p_q.pop(n - 2), pp, h)
            if h == NA_HEADS - 1:
                o_t = jnp.concatenate([outs.pop((pp, hh)) for hh in range(NA_HEADS)], axis=0)
                o_ref[pp * pair:(pp + 1) * pair, :] = o_t.T.astype(BF16)


def _na_attention(qk, va_t, l, bias):
    L = qk.shape[0]
    rows = L // GRID_W
    nblk = rows // NA_ROWS
    blk = NA_ROWS * GRID_W
    grp = blk // LANES
    w = NA_HEADS * HEAD_DIM
    prev = lambda i: jnp.maximum(i - 1, 0)
    nxt = lambda i: jnp.minimum(i + 1, nblk - 1)
    spec = lambda f, c: pl.BlockSpec((blk, w), lambda i: (f(i), c))
    vspec = lambda f: pl.BlockSpec((grp, w, LANES), lambda i: (f(i), 0, 0))
    same = lambda i: i
    return pl.pallas_call(
        functools.partial(_na_kernel, rows=rows),
        grid=(nblk,),
        in_specs=[spec(same, 0),
                  spec(prev, 1), spec(same, 1), spec(nxt, 1),
                  vspec(prev), vspec(same), vspec(nxt),
                  _layer_spec(bias, l)],
        out_specs=pl.BlockSpec((blk, w), lambda i: (i, 0)),
        out_shape=jax.ShapeDtypeStruct((L, w), BF16),
        scratch_shapes=[pltpu.VMEM((3 * blk, w), BF16), pltpu.VMEM((3 * grp, w, LANES), BF16)],
        compiler_params=_cparams(("parallel",)),
        name="na_attn",
    )(qk, qk, qk, qk, va_t, va_t, va_t, bias)


def _swa_kernel(q_ref, kp_ref, kc_ref, kn_ref, vp_ref, vc_ref, vn_ref, band_ref, sink_ref, o_ref, *, nstep):
    i = pl.program_id(0)
    B = SWA_BLOCK
    G = SWA_Q_HEADS // SWA_KV_HEADS
    k_all = jnp.concatenate([kp_ref[...], kc_ref[...], kn_ref[...]], axis=0)
    vt_all = [vp_ref[0]] + [vc_ref[j] for j in range(SWA_QB)] + [vn_ref[0]]

    def band_index(b):
        if b == 0:
            return jnp.where(i == 0, 1, 0)
        if b == SWA_QB - 1:
            return jnp.where(i == nstep - 1, 2, 0)
        return 0

    def logits(b, g):
        ks = k_all[b * B:(b + 3) * B, g * HEAD_DIM:(g + 1) * HEAD_DIM]
        qs = jnp.concatenate([q_ref[b * B:(b + 1) * B, (g * G + hh) * HEAD_DIM:(g * G + hh + 1) * HEAD_DIM]
                              for hh in range(G)], axis=0)
        return lax.dot_general(ks, qs, (((1,), (1,)), ((), ())), preferred_element_type=F32) + band_ref[band_index(b)]

    def softmax(s, g):
        sink = sink_ref[g]
        m = jnp.maximum(jnp.max(s, axis=0, keepdims=True), sink)
        p = jnp.exp2(s - m)
        l = jnp.sum(p, axis=0, keepdims=True) + jnp.exp2(sink - m)
        return p.astype(BF16), 1.0 / l

    def weighted(p_inv, b, g):
        p, inv_l = p_inv
        vt = jnp.concatenate([vt_all[b + j][g * HEAD_DIM:(g + 1) * HEAD_DIM, :] for j in range(3)], axis=-1)
        return _dot(vt, p) * inv_l

    tiles = [(b, g) for b in range(SWA_QB) for g in range(SWA_KV_HEADS)]
    s_q, p_q = {}, {}
    for n in range(len(tiles) + 2):
        if n < len(tiles):
            s_q[n] = logits(*tiles[n])
        if 1 <= n <= len(tiles):
            p_q[n - 1] = softmax(s_q.pop(n - 1), tiles[n - 1][1])
        if n >= 2:
            b, g = tiles[n - 2]
            o_t = weighted(p_q.pop(n - 2), b, g)
            for hh in range(0, G, 2):
                pair_t = jnp.concatenate([o_t[:, hh * B:(hh + 1) * B], o_t[:, (hh + 1) * B:(hh + 2) * B]], axis=0)
                h = g * G + hh
                o_ref[b * B:(b + 1) * B, h * HEAD_DIM:(h + 2) * HEAD_DIM] = pair_t.T.astype(BF16)


def _swa_band_tables():
    B = SWA_BLOCK
    G = SWA_Q_HEADS // SWA_KV_HEADS
    qi = np.arange(G * B)[None, :] % B
    kj = np.arange(3 * B)[:, None]
    band = np.abs(kj - B - qi) <= B
    tabs = [band, band & (kj >= B), band & (kj < 2 * B)]
    return jnp.asarray(np.where(np.stack(tabs), 0.0, NEG_BIG), F32)


def _swa_attention(qk, vs_t, l, band, sink_row):
    L = qk.shape[0]
    B = SWA_BLOCK
    T = SWA_QB * B
    nstep = L // T
    nblk = L // B
    prev = lambda i: jnp.maximum(i * SWA_QB - 1, 0)
    nxt = lambda i: jnp.minimum((i + 1) * SWA_QB, nblk - 1)
    return pl.pallas_call(
        functools.partial(_swa_kernel, nstep=nstep),
        grid=(nstep,),
        in_specs=[pl.BlockSpec((T, 512), lambda i: (i, 1)),
                  pl.BlockSpec((B, LANES), lambda i: (prev(i), 8)),
                  pl.BlockSpec((T, LANES), lambda i: (i, 8)),
                  pl.BlockSpec((B, LANES), lambda i: (nxt(i), 8)),
                  pl.BlockSpec((1, LANES, B), lambda i: (prev(i), 0, 0)),
                  pl.BlockSpec((SWA_QB, LANES, B), lambda i: (i, 0, 0)),
                  pl.BlockSpec((1, LANES, B), lambda i: (nxt(i), 0, 0)),
                  _const_spec(band), _layer_spec(sink_row, l)],
        out_specs=pl.BlockSpec((T, 512), lambda i: (i, 0)),
        out_shape=jax.ShapeDtypeStruct((L, 512), BF16),
        compiler_params=_cparams(("parallel",)),
        name="swa_attn",
    )(qk, qk, qk, qk, vs_t, vs_t, vs_t, band, sink_row)


def _gdn_prep_kernel(x_ref, hp_ref, hn_ref, cw_ref, ba_ref, alog_ref, dtb_ref, bd_ref,
                     qkv_ref, gb_ref, gbt_ref, *, nblk):
    i = pl.program_id(0)
    x = x_ref[...].astype(F32)
    rid = lax.broadcasted_iota(jnp.int32, (TM, 1), 0)
    hp = jnp.where(i > 0, hp_ref[HALO - 1:HALO, :].astype(F32), 0.0)
    hn = jnp.where(i < nblk - 1, hn_ref[0:1, :].astype(F32), 0.0)
    xprev = jnp.where(rid == 0, hp, pltpu.roll(x, 1, 0))
    xnext = jnp.where(rid == TM - 1, hn, pltpu.roll(x, TM - 1, 0))
    y = xprev * cw_ref[0:1, :] + x * cw_ref[1:2, :] + xnext * cw_ref[2:3, :]
    y = y * _sigmoid(y)
    bd = bd_ref[...]
    for c in range(6):
        yc = y[:, c * LANES:(c + 1) * LANES]
        if c < 4:
            yc = yc * lax.rsqrt(_group_sum(yc * yc, bd) + NORM_EPS)
        if c < 2:
            yc = yc * (HEAD_DIM ** -0.5)
        yc = yc.astype(BF16)
        qkv_ref[2 * c] = yc[:, 0:HEAD_DIM]
        qkv_ref[2 * c + 1] = yc[:, HEAD_DIM:2 * HEAD_DIM]

    ba = ba_ref[...]
    lane = lax.broadcasted_iota(jnp.int32, (1, LANES), 1)
    beta = _sigmoid(ba)
    sp_in = ba + dtb_ref[...]
    softplus = jnp.maximum(sp_in, 0.0) + jnp.log(1.0 + jnp.exp(-jnp.abs(sp_in)))
    g = jnp.where((lane >= 8) & (lane < 16), -jnp.exp(alog_ref[...]) * softplus, 0.0)
    rc = rid % GDN_CHUNK
    pre = g
    suf = g
    s = 1
    while s < GDN_CHUNK:
        pre = pre + jnp.where(rc >= s, pltpu.roll(pre, s, 0), 0.0)
        suf = suf + jnp.where(rc < GDN_CHUNK - s, pltpu.roll(suf, TM - s, 0), 0.0)
        s *= 2
    tot = pre + suf - g
    gc = jnp.where(lane < 12, pre, suf)
    slab = jnp.where(lane < 8, beta, jnp.where(lane < 16, gc, jnp.where(lane < 24, pltpu.roll(tot, 8, 1), 0.0)))
    gb_ref[...] = slab
    gbt_ref[...] = slab.T[0:24, :]


def _gdn_prep(plain, ba, l, conv_w, alog_row, dtb_row, bd):
    L = plain.shape[0]
    nblk = L // TM
    hb = TM // HALO
    return pl.pallas_call(
        functools.partial(_gdn_prep_kernel, nblk=nblk),
        grid=(nblk,),
        in_specs=[pl.BlockSpec((TM, 768), lambda i: (i, 0)),
                  pl.BlockSpec((HALO, 768), lambda i: (jnp.maximum(i * hb - 1, 0), 0)),
                  pl.BlockSpec((HALO, 768), lambda i: (jnp.minimum((i + 1) * hb, L // HALO - 1), 0)),
                  _layer_spec(conv_w, l),
                  pl.BlockSpec((TM, LANES), lambda i: (i, 0)),
                  _layer_spec(alog_row, l), _layer_spec(dtb_row, l), _const_spec(bd)],
        out_specs=[pl.BlockSpec((3 * GDN_HEADS, TM, HEAD_DIM), lambda i: (0, i, 0)),
                   pl.BlockSpec((TM, LANES), lambda i: (i, 0)),
                   pl.BlockSpec((24, TM), lambda i: (0, i))],
        out_shape=[jax.ShapeDtypeStruct((3 * GDN_HEADS, L, HEAD_DIM), BF16),
                   jax.ShapeDtypeStruct((L, LANES), F32),
                   jax.ShapeDtypeStruct((24, L), F32)],
        compiler_params=_cparams(("parallel",)),
        name="gdn_prep",
    )(plain, plain, plain, conv_w, ba, alog_row, dtb_row, bd)


def _bmm(a, b):
    return jnp.einsum('bij,bjk->bik', a.astype(BF16), b.astype(BF16), preferred_element_type=F32)


def _bmm_nt(a, b):
    return jnp.einsum('bid,bjd->bij', a.astype(BF16), b.astype(BF16), preferred_element_type=F32)


def _gdn_chunk_terms(qkv, gb, gbt, rev, out):
    G, C, H = GDN_G, GDN_CHUNK, GDN_HEADS

    def heads(part):
        return qkv[part * H:(part + 1) * H].reshape(H * G, C, HEAD_DIM)

    q = heads(0)
    k = heads(1)
    v = heads(2).astype(F32)
    kf = k.astype(F32)
    qf = q.astype(F32)
    d0 = H if rev else 0

    def colv(base):
        return jnp.concatenate([gb[:, base + h:base + h + 1].reshape(G, C, 1) for h in range(H)], axis=0)

    beta = jnp.broadcast_to(colv(d0), (H * G, C, HEAD_DIM))
    gcc = jnp.broadcast_to(colv(8 + d0), (H * G, C, HEAD_DIM))
    gl = jnp.broadcast_to(colv(16 + d0), (H * G, C, HEAD_DIM))
    grow = jnp.concatenate([gbt[8 + d0 + h:9 + d0 + h, c * C:(c + 1) * C].reshape(1, 1, C)
                            for h in range(H) for c in range(G)], axis=0)
    ii = lax.broadcasted_iota(jnp.int32, (1, C, C), 1)
    jj = lax.broadcasted_iota(jnp.int32, (1, C, C), 2)
    incl = (jj >= ii) if rev else (jj <= ii)
    strict = (jj > ii) if rev else (jj < ii)
    decay = jnp.exp(jnp.where(incl, gcc - grow, NEG_BIG))
    kk = _bmm_nt(k, k)
    qk = _bmm_nt(q, k)
    yield
    nmat = jnp.where(strict, kk * decay, 0.0) * beta
    eg = jnp.exp(gcc)
    rhs = jnp.concatenate([v * beta, kf * (beta * eg)], axis=-1)
    m = _bmm(nmat, jnp.concatenate([rhs, nmat], axis=-1))
    yield
    x = rhs - m[:, :, 0:2 * HEAD_DIM]
    p = m[:, :, 2 * HEAD_DIM:]
    for _ in range(4):
        m = _bmm(p, jnp.concatenate([x, p], axis=-1))
        yield
        x = x + m[:, :, 0:2 * HEAD_DIM]
        p = m[:, :, 2 * HEAD_DIM:]
    x = x + _bmm(p, x)
    yield
    u = x[:, :, 0:HEAD_DIM]
    w = x[:, :, HEAD_DIM:2 * HEAD_DIM]
    qkm = jnp.where(incl, qk * decay, 0.0)
    k_tail_t = jnp.swapaxes(kf * jnp.exp(gl - gcc), 1, 2)
    lhs1 = jnp.concatenate([w, qf * eg], axis=1).astype(BF16)
    lhs2 = jnp.concatenate([qkm, k_tail_t], axis=1).astype(BF16)
    dch = jnp.exp(gl[:, 0:8, :])
    out.extend([u, lhs1, lhs2, dch])


def _gdn_scan_group(u_ref, l1_ref, l2_ref, d_ref, s_ref, o_ref, rev):
    G, C, H = GDN_G, GDN_CHUNK, GDN_HEADS
    S = s_ref[...]
    for c in (range(G - 1, -1, -1) if rev else range(G)):
        r1 = _bmm(l1_ref[:, c], S)
        yield
        vn = u_ref[:, c] - r1[:, 0:C]
        r2 = _bmm(l2_ref[:, c], vn)
        yield
        oc = r1[:, C:2 * C] + r2[:, 0:C]
        S = S * d_ref[:, c][:, 0:1, :] + r2[:, C:2 * C]
        o_ref[c * C:(c + 1) * C, :] = jnp.concatenate([oc[h] for h in range(H)], axis=-1)
    s_ref[...] = S


def _gdn_kernel(qkvf_ref, gbf_ref, gbtf_ref, qkvb_ref, gbb_ref, gbtb_ref, of_ref, ob_ref,
                sf_ref, sb_ref, uf_ref, l1f_ref, l2f_ref, df_ref, ub_ref, l1b_ref, l2b_ref, db_ref):
    G, C, H = GDN_G, GDN_CHUNK, GDN_HEADS
    step = pl.program_id(0)

    @pl.when(step == 0)
    def _():
        for r in (sf_ref, sb_ref, uf_ref, l1f_ref, l2f_ref, df_ref, ub_ref, l1b_ref, l2b_ref, db_ref):
            r[...] = jnp.zeros_like(r)

    terms_f, terms_b = [], []
    strands = [_gdn_scan_group(uf_ref, l1f_ref, l2f_ref, df_ref, sf_ref, of_ref, False),
               _gdn_scan_group(ub_ref, l1b_ref, l2b_ref, db_ref, sb_ref, ob_ref, True),
               _gdn_chunk_terms(qkvf_ref[...], gbf_ref[...], gbtf_ref[...], False, terms_f),
               _gdn_chunk_terms(qkvb_ref[...], gbb_ref[...], gbtb_ref[...], True, terms_b)]
    while strands:
        strands = [s for s in strands if next(s, True) is None]
    for (u, l1, l2, dch), (u_ref, l1_ref, l2_ref, d_ref) in ((terms_f, (uf_ref, l1f_ref, l2f_ref, df_ref)),
                                                          (terms_b, (ub_ref, l1b_ref, l2b_ref, db_ref))):
        u_ref[...] = u.reshape(H, G, C, HEAD_DIM)
        l1_ref[...] = l1.reshape(H, G, 2 * C, HEAD_DIM)
        l2_ref[...] = l2.reshape(H, G, 2 * C, C)
        d_ref[...] = dch.reshape(H, G, 8, HEAD_DIM)


def _gdn(qkv, gb, gbt):
    L = qkv.shape[1]
    G, C, H = GDN_G, GDN_CHUNK, GDN_HEADS
    T = G * C
    n = L // T
    fin = lambda s: jnp.minimum(s, n - 1)
    bin_ = lambda s: jnp.maximum(n - 1 - s, 0)
    fout = lambda s: jnp.maximum(s - 1, 0)
    bout = lambda s: jnp.minimum(n - s, n - 1)
    term_scratch = [pltpu.VMEM((H, G, C, HEAD_DIM), F32), pltpu.VMEM((H, G, 2 * C, HEAD_DIM), BF16),
                    pltpu.VMEM((H, G, 2 * C, C), BF16), pltpu.VMEM((H, G, 8, HEAD_DIM), F32)]
    return pl.pallas_call(
        _gdn_kernel,
        grid=(n + 1,),
        in_specs=[pl.BlockSpec((3 * H, T, HEAD_DIM), lambda s: (0, fin(s), 0)),
                  pl.BlockSpec((T, LANES), lambda s: (fin(s), 0)),
                  pl.BlockSpec((24, T), lambda s: (0, fin(s))),
                  pl.BlockSpec((3 * H, T, HEAD_DIM), lambda s: (0, bin_(s), 0)),
                  pl.BlockSpec((T, LANES), lambda s: (bin_(s), 0)),
                  pl.BlockSpec((24, T), lambda s: (0, bin_(s)))],
        out_specs=[pl.BlockSpec((T, 256), lambda s: (fout(s), 0)),
                   pl.BlockSpec((T, 256), lambda s: (bout(s), 0))],
        out_shape=[jax.ShapeDtypeStruct((L, 256), F32), jax.ShapeDtypeStruct((L, 256), F32)],
        scratch_shapes=[pltpu.VMEM((H, HEAD_DIM, HEAD_DIM), F32), pltpu.VMEM((H, HEAD_DIM, HEAD_DIM), F32)]
                       + term_scratch + term_scratch,
        compiler_params=_cparams(("arbitrary",)),
        name="gdn_scan",
    )(qkv, gb, gbt, qkv, gb, gbt)


def _merge_kernel(x_ref, yna_ref, yswa_ref, of_ref, ob_ref, z_ref, gate_ref, gn_ref, bd_ref,
                  wna_ref, wswa_ref, wgdn_ref, wout_ref, o_ref):
    bd = bd_ref[...]
    o = of_ref[...] + ob_ref[...]
    z = z_ref[...].astype(F32)
    parts = []
    for c in range(2):
        oc = o[:, c * LANES:(c + 1) * LANES]
        ms = _group_sum(oc * oc, bd) * (1.0 / HEAD_DIM)
        parts.append(oc * lax.rsqrt(ms + NORM_EPS))
    ygdn = jnp.concatenate(parts, axis=-1) * gn_ref[...] * (z * _sigmoid(z))
    m = (gate_ref[:, 0:D_MODEL].astype(F32) * _dot(yna_ref[...], wna_ref[...])
         + gate_ref[:, D_MODEL:2 * D_MODEL].astype(F32) * _dot(yswa_ref[...], wswa_ref[...])
         + gate_ref[:, 2 * D_MODEL:3 * D_MODEL].astype(F32) * _dot(ygdn.astype(BF16), wgdn_ref[...]))
    o_ref[...] = x_ref[...] + _dot(m.astype(BF16), wout_ref[...])


def _merge(x, yna, yswa, of, ob, plain, gate, l, gn, bd, wna, wswa, wgdn, wout):
    L = x.shape[0]
    row = lambda w_, c=0: pl.BlockSpec((TM, w_), lambda i: (i, c))
    return pl.pallas_call(
        _merge_kernel,
        grid=(L // TM,),
        in_specs=[row(D_MODEL), row(256), row(512), row(256), row(256), row(256, 4), row(SEG_GATE),
                  _layer_spec(gn, l), _const_spec(bd), _layer_spec(wna, l), _layer_spec(wswa, l),
                  _layer_spec(wgdn, l), _layer_spec(wout, l)],
        out_specs=row(D_MODEL),
        out_shape=jax.ShapeDtypeStruct((L, D_MODEL), F32),
        compiler_params=_cparams(("parallel",)),
        name="merge",
    )(x, yna, yswa, of, ob, plain, gate, gn, bd, wna, wswa, wgdn, wout)


def _ffn_kernel(x_ref, xp_ref, xn_ref, g_ref, wa_ref, wb_ref, cwa_ref, cwb_ref, ba_ref, bb_ref, wd_ref,
                o_ref, hbuf, acc, *, nblk):
    i = pl.program_id(0)
    j = pl.program_id(1)

    def normed(xv):
        ms = jnp.mean(xv * xv, axis=-1, keepdims=True)
        return xv * lax.rsqrt(ms + NORM_EPS) * g_ref[...]

    @pl.when(j == 0)
    def _():
        hbuf[0:HALO, :] = jnp.where(i > 0, normed(xp_ref[...]), 0.0).astype(BF16)
        hbuf[HALO:HALO + TM, :] = normed(x_ref[...]).astype(BF16)
        hbuf[HALO + TM:, :] = jnp.where(i < nblk - 1, normed(xn_ref[...]), 0.0).astype(BF16)
        acc[...] = jnp.zeros_like(acc)

    h = hbuf[...]
    n = TM + 2 * HALO

    def conv(w_ref, cw_ref, b_ref):
        u = _dot(h, w_ref[...])
        y = pltpu.roll(u, 1, 0) * cw_ref[0:1, :] + u * cw_ref[1:2, :] + pltpu.roll(u, n - 1, 0) * cw_ref[2:3, :]
        return y[HALO:HALO + TM] + b_ref[...]

    a = conv(wa_ref, cwa_ref, ba_ref)
    b = conv(wb_ref, cwb_ref, bb_ref)
    gated = (a * _sigmoid(a) * b).astype(BF16)
    acc[...] += _dot(gated, wd_ref[...])

    @pl.when(j == pl.num_programs(1) - 1)
    def _():
        o_ref[...] = x_ref[...] + acc[...]


def _ffn(x, l, g, w_up, conv_w, conv_b, w_down):
    L = x.shape[0]
    nblk = L // TM
    nj = D_FF // FFN_NC
    hb = TM // HALO
    return pl.pallas_call(
        functools.partial(_ffn_kernel, nblk=nblk),
        grid=(nblk, nj),
        in_specs=[pl.BlockSpec((TM, D_MODEL), lambda i, j: (i, 0)),
                  pl.BlockSpec((HALO, D_MODEL), lambda i, j: (jnp.maximum(i * hb - 1, 0), 0)),
                  pl.BlockSpec((HALO, D_MODEL), lambda i, j: (jnp.minimum((i + 1) * hb, L // HALO - 1), 0)),
                  _layer_spec(g, l),
                  pl.BlockSpec((None, D_MODEL, FFN_NC), lambda i, j: (l, 0, j)),
                  pl.BlockSpec((None, D_MODEL, FFN_NC), lambda i, j: (l, 0, nj + j)),
                  pl.BlockSpec((None, 3, FFN_NC), lambda i, j: (l, 0, j)),
                  pl.BlockSpec((None, 3, FFN_NC), lambda i, j: (l, 0, nj + j)),
                  pl.BlockSpec((None, 1, FFN_NC), lambda i, j: (l, 0, j)),
                  pl.BlockSpec((None, 1, FFN_NC), lambda i, j: (l, 0, nj + j)),
                  pl.BlockSpec((None, FFN_NC, D_MODEL), lambda i, j: (l, j, 0))],
        out_specs=pl.BlockSpec((TM, D_MODEL), lambda i, j: (i, 0)),
        out_shape=jax.ShapeDtypeStruct((L, D_MODEL), F32),
        scratch_shapes=[pltpu.VMEM((TM + 2 * HALO, D_MODEL), BF16), pltpu.VMEM((TM, D_MODEL), F32)],
        compiler_params=_cparams(("parallel", "arbitrary")),
        name="ffn",
    )(x, x, x, g, w_up, w_up, conv_w, conv_w, conv_b, conv_b, w_down)


def _pack_w_in(w):
    qa, ka, va = w[..., 0:256], w[..., 256:512], w[..., 512:768]
    qs, ks, vs = w[..., 768:1280], w[..., 1280:1408], w[..., 1408:1536]
    qkv_c, z_c = w[..., 1536:2304], w[..., 2304:2560]
    ba, gate = w[..., 2560:2576], w[..., 2576:5648]
    pad = jnp.zeros(w.shape[:-1] + (SEG_BA - 16,), w.dtype)
    return jnp.concatenate([qa, ka, qs, ks, qkv_c, va, z_c, vs, gate, ba, pad], axis=-1).astype(BF16)


def _rope_tables(L):
    inv = 1.0 / (ROPE_THETA ** (jnp.arange(0, HEAD_DIM, 2, dtype=F32) / HEAD_DIM))
    ang = jnp.arange(L, dtype=F32)[:, None] * inv[None, :]
    cos, sin = jnp.cos(ang), jnp.sin(ang)
    cos128 = jnp.concatenate([cos, cos, cos, cos], axis=1)
    sin128 = jnp.concatenate([-sin, sin, -sin, sin], axis=1)
    return cos128, sin128


def kernel(x, attn_norm, w_in, qk_norm, na_rpb, swa_sink, gdn_conv_w, gdn_a_log, gdn_dt_bias, gdn_norm,
           w_branch_na, w_branch_swa, w_branch_gdn, w_out, ffn_norm, w_up, ffn_conv_w, ffn_conv_b, w_down):
    B, L, D = x.shape
    assert B == 1 and D == D_MODEL and L % (NA_ROWS * GRID_W) == 0 and L // GRID_W >= 2 * NA_ROWS
    depth = w_in.shape[0]
    cos128, sin128 = _rope_tables(L)
    blockdiag = jnp.asarray(np.kron(np.eye(2), np.ones((HEAD_DIM, HEAD_DIM))), BF16)
    bias_tabs = _na_bias_tables(na_rpb, L // GRID_W)
    swa_band = _swa_band_tables()
    scale = HEAD_DIM ** -0.5 * LOG2E
    G = SWA_Q_HEADS // SWA_KV_HEADS
    gain = jnp.concatenate([jnp.tile(qk_norm[:, 0] * scale, (1, NA_HEADS)), jnp.tile(qk_norm[:, 1], (1, NA_HEADS)),
                            jnp.tile(qk_norm[:, 2] * scale, (1, SWA_Q_HEADS)),
                            jnp.tile(qk_norm[:, 3], (1, SWA_KV_HEADS))], axis=1)[:, None, :]
    sink_row = jnp.repeat(swa_sink.reshape(depth, SWA_KV_HEADS, G) * LOG2E, SWA_BLOCK, axis=2)[:, :, None, :]
    alog_row = jnp.pad(gdn_a_log.reshape(depth, 1, 8), ((0, 0), (0, 0), (8, LANES - 16)))
    dtb_row = jnp.pad(gdn_dt_bias.reshape(depth, 1, 8), ((0, 0), (0, 0), (8, LANES - 16)))
    gdn_gain = jnp.tile(gdn_norm, (1, GDN_HEADS))[:, None, :]
    attn_g = attn_norm[:, None, :]
    ffn_g = ffn_norm[:, None, :]
    ffn_b = ffn_conv_b[:, None, :]
    w_in_p = _pack_w_in(w_in)
    w_na, w_swa, w_gdn = w_branch_na.astype(BF16), w_branch_swa.astype(BF16), w_branch_gdn.astype(BF16)
    w_o, w_u, w_d = w_out.astype(BF16), w_up.astype(BF16), w_down.astype(BF16)
    xs = x[0]
    for l in range(depth):
        qk, plain, gate, ba, va_t, vs_t = _inproj(xs, l, attn_g, w_in_p, gain, cos128, sin128, blockdiag)
        y_na = _na_attention(qk, va_t, l, bias_tabs)
        y_swa = _swa_attention(qk, vs_t, l, swa_band, sink_row)
        qkv_n, gb, gbt = _gdn_prep(plain, ba, l, gdn_conv_w, alog_row, dtb_row, blockdiag)
        o_f, o_b = _gdn(qkv_n, gb, gbt)
        xs = _merge(xs, y_na, y_swa, o_f, o_b, plain, gate, l, gdn_gain, blockdiag, w_na, w_swa, w_gdn, w_o)
        xs = _ffn(xs, l, ffn_g, w_u, ffn_conv_w, ffn_b, w_d)
    return xs[None]
```

```python
import functools

import numpy as np
import jax
import jax.numpy as jnp
from jax import lax
from jax.experimental import pallas as pl
from jax.experimental.pallas import tpu as pltpu

F32 = jnp.float32
BF16 = jnp.bfloat16

D_MODEL = 1024
HEAD_DIM = 64
GRID_W = 64
NORM_EPS = 1e-6
NA_HEADS = 4
NA_KH = 8
NA_KW = 16
SWA_Q_HEADS = 8
SWA_KV_HEADS = 2
SWA_BLOCK = 128
ROPE_THETA = 10000.0
GDN_HEADS = 4
GDN_CHUNK = 64
D_FF = 2816

LANES = 128
NEG_BIG = -1e30
LOG2E = 1.4426950408889634
VMEM_LIMIT = 56 * 1024 * 1024

SEG_QK = 1152
SEG_PLAIN = 1408
PLAIN_VA = 768
PLAIN_VS = 1280
SEG_GATE = 3072
SEG_BA = 128
IN_PACKED = SEG_QK + SEG_PLAIN + SEG_GATE + SEG_BA

TM = 512
NA_ROWS = 8
NA_WIN = NA_KH + 2
SWA_QB = 4
GDN_G = 4
FFN_NC = 1408
HALO = 8


def _cparams(sem):
    return pltpu.CompilerParams(dimension_semantics=sem, vmem_limit_bytes=VMEM_LIMIT)


def _dot(a, b):
    return jnp.dot(a, b, preferred_element_type=F32)


def _sigmoid(x):
    return 1.0 / (1.0 + jnp.exp(-x))


def _group_sum(sq, bd_ref):
    w = sq.shape[-1]
    return _dot(sq.astype(BF16), bd_ref[0:w, 0:w])


def _inproj_kernel(x_ref, g_ref, w_ref, gain_ref, cos_ref, sin_ref, bd_ref,
                   oqk_ref, oplain_ref, ogate_ref, oba_ref, ovat_ref, ovst_ref):
    x = x_ref[...]
    ms = jnp.mean(x * x, axis=-1, keepdims=True)
    h = (x * lax.rsqrt(ms + NORM_EPS) * g_ref[...]).astype(BF16)
    cw = 2 * LANES
    lane = lax.broadcasted_iota(jnp.int32, (1, cw), 1)
    first_half = (lane % HEAD_DIM) < (HEAD_DIM // 2)
    cos = jnp.concatenate([cos_ref[...], cos_ref[...]], axis=1)
    sin = jnp.concatenate([sin_ref[...], sin_ref[...]], axis=1)

    def head_norm(t, c):
        cols = slice(c * cw, min((c + 1) * cw, SEG_QK))
        w = cols.stop - cols.start
        tc = t[:, cols]
        ss = _group_sum(tc * tc, bd_ref)
        y = tc * lax.rsqrt(ss * (1.0 / HEAD_DIM) + NORM_EPS) * gain_ref[:, cols]
        if cols.start >= 512:
            rot = jnp.where(first_half[:, 0:w], pltpu.roll(y, w - HEAD_DIM // 2, 1), pltpu.roll(y, HEAD_DIM // 2, 1))
            y = y * cos[:, 0:w] + rot * sin[:, 0:w]
        oqk_ref[:, cols] = y.astype(BF16)

    o_plain, o_gate, o_ba = SEG_QK, SEG_QK + SEG_PLAIN, SEG_QK + SEG_PLAIN + SEG_GATE
    t = _dot(h, w_ref[:, 0:SEG_QK])
    tp = _dot(h, w_ref[:, o_plain:o_plain + SEG_PLAIN])
    for c in (0, 1):
        head_norm(t, c)
    tg = _dot(h, w_ref[:, o_gate:o_gate + D_MODEL])
    oplain_ref[...] = tp.astype(BF16)
    for grp in range(TM // LANES):
        rows = slice(grp * LANES, (grp + 1) * LANES)
        ovat_ref[grp] = tp[rows, PLAIN_VA:PLAIN_VA + 256].T.astype(BF16)
        ovst_ref[grp] = tp[rows, PLAIN_VS:PLAIN_VS + 128].T.astype(BF16)
    for c in (2, 3):
        head_norm(t, c)
    for c in range(3):
        nxt = (_dot(h, w_ref[:, o_gate + (c + 1) * D_MODEL:o_gate + (c + 2) * D_MODEL]) if c < 2
               else _dot(h, w_ref[:, o_ba:o_ba + SEG_BA]))
        ogate_ref[:, c * D_MODEL:(c + 1) * D_MODEL] = _sigmoid(tg).astype(BF16)
        if c == 0:
            head_norm(t, 4)
        tg = nxt
    oba_ref[...] = tg


def _layer_spec(a, l, **kw):
    nd = a.ndim - 1
    return pl.BlockSpec((None,) + a.shape[1:], lambda *_: (l,) + (0,) * nd, **kw)


def _const_spec(a):
    return pl.BlockSpec(a.shape, lambda *_: (0,) * a.ndim)


def _inproj(x, l, g, w, gain, cos, sin, bd):
    L = x.shape[0]
    row = lambda w_: pl.BlockSpec((TM, w_), lambda i: (i, 0))
    return pl.pallas_call(
        _inproj_kernel,
        grid=(L // TM,),
        in_specs=[row(D_MODEL), _layer_spec(g, l), _layer_spec(w, l, pipeline_mode=pl.Buffered(1)),
                  _layer_spec(gain, l), row(LANES), row(LANES), _const_spec(bd)],
        out_specs=[row(SEG_QK), row(SEG_PLAIN), row(SEG_GATE), row(SEG_BA),
                   pl.BlockSpec((TM // LANES, 256, LANES), lambda i: (i, 0, 0)),
                   pl.BlockSpec((TM // LANES, 128, LANES), lambda i: (i, 0, 0))],
        out_shape=[jax.ShapeDtypeStruct((L, SEG_QK), BF16),
                   jax.ShapeDtypeStruct((L, SEG_PLAIN), BF16),
                   jax.ShapeDtypeStruct((L, SEG_GATE), BF16),
                   jax.ShapeDtypeStruct((L, SEG_BA), F32),
                   jax.ShapeDtypeStruct((L // LANES, 256, LANES), BF16),
                   jax.ShapeDtypeStruct((L // LANES, 128, LANES), BF16)],
        compiler_params=_cparams(("parallel",)),
        name="inproj",
    )(x, g, w, gain, cos, sin, bd)


def _toeplitz_kernel(rpb_ref, onehot_ref, colmask_ref, o_ref):
    r = rpb_ref[...]
    hi = r.astype(BF16)
    r1 = r - hi.astype(F32)
    mid = r1.astype(BF16)
    lo = (r1 - mid.astype(F32)).astype(BF16)
    oh = onehot_ref[...]
    w = _dot(hi, oh) + _dot(mid, oh) + _dot(lo, oh)
    o_ref[...] = jnp.where(colmask_ref[...] > 0.0, w * LOG2E, NEG_BIG)


def _na_assemble_kernel(tiles_ref, o_ref, *, a_idx):
    neg = jnp.full((GRID_W, GRID_W), NEG_BIG, F32)
    for t in range(a_idx.shape[0]):
        for j in range(NA_WIN):
            for e in range(2):
                a = int(a_idx[t, j, e])
                o_ref[t, j * GRID_W:(j + 1) * GRID_W, e * GRID_W:(e + 1) * GRID_W] = neg if a < 0 else tiles_ref[a]


def _na_bias_tables(na_rpb, rows):
    depth, H, nr, nc = na_rpb.shape
    kc = np.arange(GRID_W)[:, None]
    qc = np.arange(GRID_W)[None, :]
    dc = np.clip(kc - qc + (NA_KW - 1), 0, 2 * NA_KW - 2).reshape(-1)
    onehot = np.zeros((32, GRID_W * GRID_W), np.float32)
    onehot[dc, np.arange(GRID_W * GRID_W)] = 1.0
    col_start = np.clip(qc - NA_KW // 2, 0, GRID_W - NA_KW)
    colmask = ((kc >= col_start) & (kc < col_start + NA_KW)).astype(np.float32).reshape(1, -1)
    rpb_rows = jnp.pad(na_rpb, ((0, 0), (0, 0), (0, 16 - nr), (0, 32 - nc))).reshape(depth * H * 16, 32)
    tiles = pl.pallas_call(
        _toeplitz_kernel,
        out_shape=jax.ShapeDtypeStruct((depth * H * 16, GRID_W * GRID_W), F32),
        name="na_bias_tiles",
    )(rpb_rows, jnp.asarray(onehot, BF16), jnp.asarray(colmask))
    tiles = tiles.reshape(depth * H, 16, GRID_W, GRID_W)
    a_idx = np.stack([_na_pair_structure(r0, rows)[1] for r0 in _na_pair_type_rows(rows)])
    nt = a_idx.shape[0]
    tabs = pl.pallas_call(
        functools.partial(_na_assemble_kernel, a_idx=a_idx),
        grid=(depth * H,),
        in_specs=[pl.BlockSpec((None, 16, GRID_W, GRID_W), lambda i: (i, 0, 0, 0))],
        out_specs=pl.BlockSpec((None, nt, NA_WIN * GRID_W, 2 * GRID_W), lambda i: (i, 0, 0, 0)),
        out_shape=jax.ShapeDtypeStruct((depth * H, nt, NA_WIN * GRID_W, 2 * GRID_W), F32),
        compiler_params=_cparams(("parallel",)),
        name="na_bias_tables",
    )(tiles)
    return tabs.reshape(depth, H, nt, NA_WIN * GRID_W, 2 * GRID_W)


def _na_pair_structure(r0, rows):
    wstart = int(np.clip(r0 - NA_KH // 2, 0, rows - NA_KH - 1)) // 2 * 2
    a_idx = np.full((NA_WIN, 2), -1, np.int64)
    for e in range(2):
        rr = r0 + e
        rs = int(np.clip(rr - NA_KH // 2, 0, rows - NA_KH))
        for j in range(NA_WIN):
            krow = wstart + j
            if rs <= krow < rs + NA_KH:
                a_idx[j, e] = krow - rr + NA_KH - 1
    return wstart, a_idx


def _na_pair_type_rows(rows):
    reps = [0, 2, 4, rows - 4, rows - 2]
    for r0 in range(0, rows, 2):
        t = 3 + (r0 - (rows - 4)) // 2 if r0 >= rows - 4 else min(r0 // 2, 2)
        ws, a = _na_pair_structure(r0, rows)
        ws_t, a_t = _na_pair_structure(reps[t], rows)
        assert (a == a_t).all() and r0 - ws == reps[t] - ws_t
    return reps


def _na_kernel(q_ref, kp_ref, kc_ref, kn_ref, vp_ref, vc_ref, vn_ref, bias_ref, o_ref, kbuf, vbuf, *, rows):
    i = pl.program_id(0)
    blk = NA_ROWS * GRID_W
    grp = blk // LANES
    kbuf[0:blk, :] = kp_ref[...]
    kbuf[blk:2 * blk, :] = kc_ref[...]
    kbuf[2 * blk:3 * blk, :] = kn_ref[...]
    vbuf[0:grp] = vp_ref[...]
    vbuf[grp:2 * grp] = vc_ref[...]
    vbuf[2 * grp:3 * grp] = vn_ref[...]
    win = NA_WIN * GRID_W
    wgrp = win // LANES
    pair = 2 * GRID_W

    npair = NA_ROWS // 2
    goffs, types = [], []
    for pp in range(npair):
        r0 = i * NA_ROWS + 2 * pp
        wstart = jnp.clip(r0 - NA_KH // 2, 0, rows - NA_KH - 1) // 2 * 2
        types.append(jnp.where(r0 >= rows - 4, 3 + (r0 - (rows - 4)) // 2, jnp.minimum(r0 // 2, 2)))
        goffs.append((wstart - (i - 1) * NA_ROWS) // 2)

    def logits(pp, h):
        sl = slice(h * HEAD_DIM, (h + 1) * HEAD_DIM)
        kw = kbuf[pl.ds(pl.multiple_of(goffs[pp] * LANES, LANES), win), sl]
        s = lax.dot_general(kw, q_ref[pp * pair:(pp + 1) * pair, sl],
                            (((1,), (1,)), ((), ())), preferred_element_type=F32)
        return s + bias_ref[h, types[pp]]

    def softmax(s):
        m = jnp.max(s, axis=0, keepdims=True)
        p = jnp.exp2(s - m)
        return p.astype(BF16), 1.0 / jnp.sum(p, axis=0, keepdims=True)

    def weighted(p_inv, pp, h):
        p, inv_l = p_inv
        vt = jnp.concatenate([vbuf[goffs[pp] + j, h * HEAD_DIM:(h + 1) * HEAD_DIM, :] for j in range(wgrp)], axis=-1)
        return _dot(vt, p) * inv_l

    tiles = [(pp, h) for pp in range(npair) for h in range(NA_HEADS)]
    s_q, p_q, outs = {}, {}, {}
    for n in range(len(tiles) + 2):
        if n < len(tiles):
            s_q[n] = logits(*tiles[n])
        if 1 <= n <= len(tiles):
            p_q[n - 1] = softmax(s_q.pop(n - 1))
        if n >= 2:
            pp, h = tiles[n - 2]
            outs[(pp, h)] = weighted(p_q.pop(n - 2), pp, h)
            if h == NA_HEADS - 1:
                o_t = jnp.concatenate([outs.pop((pp, hh)) for hh in range(NA_HEADS)], axis=0)
                o_ref[pp * pair:(pp + 1) * pair, :] = o_t.T.astype(BF16)


def _na_attention(qk, va_t, l, bias):
    L = qk.shape[0]
    rows = L // GRID_W
    nblk = rows // NA_ROWS
    blk = NA_ROWS * GRID_W
    grp = blk // LANES
    w = NA_HEADS * HEAD_DIM
    prev = lambda i: jnp.maximum(i - 1, 0)
    nxt = lambda i: jnp.minimum(i + 1, nblk - 1)
    spec = lambda f, c: pl.BlockSpec((blk, w), lambda i: (f(i), c))
    vspec = lambda f: pl.BlockSpec((grp, w, LANES), lambda i: (f(i), 0, 0))
    same = lambda i: i
    return pl.pallas_call(
        functools.partial(_na_kernel, rows=rows),
        grid=(nblk,),
        in_specs=[spec(same, 0),
                  spec(prev, 1), spec(same, 1), spec(nxt, 1),
                  vspec(prev), vspec(same), vspec(nxt),
                  _layer_spec(bias, l)],
        out_specs=pl.BlockSpec((blk, w), lambda i: (i, 0)),
        out_shape=jax.ShapeDtypeStruct((L, w), BF16),
        scratch_shapes=[pltpu.VMEM((3 * blk, w), BF16), pltpu.VMEM((3 * grp, w, LANES), BF16)],
        compiler_params=_cparams(("parallel",)),
        name="na_attn",
    )(qk, qk, qk, qk, va_t, va_t, va_t, bias)


def _swa_kernel(q_ref, kp_ref, kc_ref, kn_ref, vp_ref, vc_ref, vn_ref, band_ref, sink_ref, o_ref, *, nstep):
    i = pl.program_id(0)
    B = SWA_BLOCK
    G = SWA_Q_HEADS // SWA_KV_HEADS
    k_all = jnp.concatenate([kp_ref[...], kc_ref[...], kn_ref[...]], axis=0)
    vt_all = [vp_ref[0]] + [vc_ref[j] for j in range(SWA_QB)] + [vn_ref[0]]

    def band_index(b):
        if b == 0:
            return jnp.where(i == 0, 1, 0)
        if b == SWA_QB - 1:
            return jnp.where(i == nstep - 1, 2, 0)
        return 0

    def logits(b, g):
        ks = k_all[b * B:(b + 3) * B, g * HEAD_DIM:(g + 1) * HEAD_DIM]
        qs = jnp.concatenate([q_ref[b * B:(b + 1) * B, (g * G + hh) * HEAD_DIM:(g * G + hh + 1) * HEAD_DIM]
                              for hh in range(G)], axis=0)
        return lax.dot_general(ks, qs, (((1,), (1,)), ((), ())), preferred_element_type=F32) + band_ref[band_index(b)]

    def softmax(s, g):
        sink = sink_ref[g]
        m = jnp.maximum(jnp.max(s, axis=0, keepdims=True), sink)
        p = jnp.exp2(s - m)
        l = jnp.sum(p, axis=0, keepdims=True) + jnp.exp2(sink - m)
        return p.astype(BF16), 1.0 / l

    def weighted(p_inv, b, g):
        p, inv_l = p_inv
        vt = jnp.concatenate([vt_all[b + j][g * HEAD_DIM:(g + 1) * HEAD_DIM, :] for j in range(3)], axis=-1)
        return _dot(vt, p) * inv_l

    tiles = [(b, g) for b in range(SWA_QB) for g in range(SWA_KV_HEADS)]
    s_q, p_q = {}, {}
    for n in range(len(tiles) + 2):
        if n < len(tiles):
            s_q[n] = logits(*tiles[n])
        if 1 <= n <= len(tiles):
            p_q[n - 1] = softmax(s_q.pop(n - 1), tiles[n - 1][1])
        if n >= 2:
            b, g = tiles[n - 2]
            o_t = weighted(p_q.pop(n - 2), b, g)
            for hh in range(0, G, 2):
                pair_t = jnp.concatenate([o_t[:, hh * B:(hh + 1) * B], o_t[:, (hh + 1) * B:(hh + 2) * B]], axis=0)
                h = g * G + hh
                o_ref[b * B:(b + 1) * B, h * HEAD_DIM:(h + 2) * HEAD_DIM] = pair_t.T.astype(BF16)


def _swa_band_tables():
    B = SWA_BLOCK
    G = SWA_Q_HEADS // SWA_KV_HEADS
    qi = np.arange(G * B)[None, :] % B
    kj = np.arange(3 * B)[:, None]
    band = np.abs(kj - B - qi) <= B
    tabs = [band, band & (kj >= B), band & (kj < 2 * B)]
    return jnp.asarray(np.where(np.stack(tabs), 0.0, NEG_BIG), F32)


def _swa_attention(qk, vs_t, l, band, sink_row):
    L = qk.shape[0]
    B = SWA_BLOCK
    T = SWA_QB * B
    nstep = L // T
    nblk = L // B
    prev = lambda i: jnp.maximum(i * SWA_QB - 1, 0)
    nxt = lambda i: jnp.minimum((i + 1) * SWA_QB, nblk - 1)
    return pl.pallas_call(
        functools.partial(_swa_kernel, nstep=nstep),
        grid=(nstep,),
        in_specs=[pl.BlockSpec((T, 512), lambda i: (i, 1)),
                  pl.BlockSpec((B, LANES), lambda i: (prev(i), 8)),
                  pl.BlockSpec((T, LANES), lambda i: (i, 8)),
                  pl.BlockSpec((B, LANES), lambda i: (nxt(i), 8)),
                  pl.BlockSpec((1, LANES, B), lambda i: (prev(i), 0, 0)),
                  pl.BlockSpec((SWA_QB, LANES, B), lambda i: (i, 0, 0)),
                  pl.BlockSpec((1, LANES, B), lambda i: (nxt(i), 0, 0)),
                  _const_spec(band), _layer_spec(sink_row, l)],
        out_specs=pl.BlockSpec((T, 512), lambda i: (i, 0)),
        out_shape=jax.ShapeDtypeStruct((L, 512), BF16),
        compiler_params=_cparams(("parallel",)),
        name="swa_attn",
    )(qk, qk, qk, qk, vs_t, vs_t, vs_t, band, sink_row)


def _gdn_prep_kernel(x_ref, hp_ref, hn_ref, cw_ref, ba_ref, alog_ref, dtb_ref, bd_ref,
                     qkv_ref, gb_ref, gbt_ref, *, nblk):
    i = pl.program_id(0)
    x = x_ref[...].astype(F32)
    rid = lax.broadcasted_iota(jnp.int32, (TM, 1), 0)
    hp = jnp.where(i > 0, hp_ref[HALO - 1:HALO, :].astype(F32), 0.0)
    hn = jnp.where(i < nblk - 1, hn_ref[0:1, :].astype(F32), 0.0)
    xprev = jnp.where(rid == 0, hp, pltpu.roll(x, 1, 0))
    xnext = jnp.where(rid == TM - 1, hn, pltpu.roll(x, TM - 1, 0))
    y = xprev * cw_ref[0:1, :] + x * cw_ref[1:2, :] + xnext * cw_ref[2:3, :]
    y = y * _sigmoid(y)
    for part in range(3):
        yp = y[:, part * 256:(part + 1) * 256]
        if part < 2:
            yp = yp * lax.rsqrt(_group_sum(yp * yp, bd_ref) + NORM_EPS)
        if part == 0:
            yp = yp * (HEAD_DIM ** -0.5)
        yp = yp.astype(BF16)
        for hh in range(GDN_HEADS):
            qkv_ref[part * GDN_HEADS + hh] = yp[:, hh * HEAD_DIM:(hh + 1) * HEAD_DIM]

    ba = ba_ref[...]
    lane = lax.broadcasted_iota(jnp.int32, (1, LANES), 1)
    beta = _sigmoid(ba)
    sp_in = ba + dtb_ref[...]
    softplus = jnp.maximum(sp_in, 0.0) + jnp.log(1.0 + jnp.exp(-jnp.abs(sp_in)))
    g = jnp.where((lane >= 8) & (lane < 16), -jnp.exp(alog_ref[...]) * softplus, 0.0)
    rc = rid % GDN_CHUNK
    pre = g
    suf = g
    s = 1
    while s < GDN_CHUNK:
        pre = pre + jnp.where(rc >= s, pltpu.roll(pre, s, 0), 0.0)
        suf = suf + jnp.where(rc < GDN_CHUNK - s, pltpu.roll(suf, TM - s, 0), 0.0)
        s *= 2
    tot = pre + suf - g
    gc = jnp.where(lane < 12, pre, suf)
    slab = jnp.where(lane < 8, beta, jnp.where(lane < 16, gc, jnp.where(lane < 24, pltpu.roll(tot, 8, 1), 0.0)))
    gb_ref[...] = slab
    gbt_ref[...] = slab.T[0:24, :]


def _gdn_prep(plain, ba, l, conv_w, alog_row, dtb_row, bd):
    L = plain.shape[0]
    nblk = L // TM
    hb = TM // HALO
    return pl.pallas_call(
        functools.partial(_gdn_prep_kernel, nblk=nblk),
        grid=(nblk,),
        in_specs=[pl.BlockSpec((TM, 768), lambda i: (i, 0)),
                  pl.BlockSpec((HALO, 768), lambda i: (jnp.maximum(i * hb - 1, 0), 0)),
                  pl.BlockSpec((HALO, 768), lambda i: (jnp.minimum((i + 1) * hb, L // HALO - 1), 0)),
                  _layer_spec(conv_w, l),
                  pl.BlockSpec((TM, LANES), lambda i: (i, 0)),
                  _layer_spec(alog_row, l), _layer_spec(dtb_row, l), _const_spec(bd)],
        out_specs=[pl.BlockSpec((3 * GDN_HEADS, TM, HEAD_DIM), lambda i: (0, i, 0)),
                   pl.BlockSpec((TM, LANES), lambda i: (i, 0)),
                   pl.BlockSpec((24, TM), lambda i: (0, i))],
        out_shape=[jax.ShapeDtypeStruct((3 * GDN_HEADS, L, HEAD_DIM), BF16),
                   jax.ShapeDtypeStruct((L, LANES), F32),
                   jax.ShapeDtypeStruct((24, L), F32)],
        compiler_params=_cparams(("parallel",)),
        name="gdn_prep",
    )(plain, plain, plain, conv_w, ba, alog_row, dtb_row, bd)


def _bmm(a, b):
    return jnp.einsum('bij,bjk->bik', a.astype(BF16), b.astype(BF16), preferred_element_type=F32)


def _bmm_nt(a, b):
    return jnp.einsum('bid,bjd->bij', a.astype(BF16), b.astype(BF16), preferred_element_type=F32)


def _gdn_chunk_terms(qkv, gb, gbt, rev, out):
    G, C, H = GDN_G, GDN_CHUNK, GDN_HEADS

    def heads(part):
        return qkv[part * H:(part + 1) * H].reshape(H * G, C, HEAD_DIM)

    q = heads(0)
    k = heads(1)
    v = heads(2).astype(F32)
    kf = k.astype(F32)
    qf = q.astype(F32)
    d0 = H if rev else 0

    def colv(base):
        return jnp.concatenate([gb[:, base + h:base + h + 1].reshape(G, C, 1) for h in range(H)], axis=0)

    beta = jnp.broadcast_to(colv(d0), (H * G, C, HEAD_DIM))
    gcc = jnp.broadcast_to(colv(8 + d0), (H * G, C, HEAD_DIM))
    gl = jnp.broadcast_to(colv(16 + d0), (H * G, C, HEAD_DIM))
    grow = jnp.concatenate([gbt[8 + d0 + h:9 + d0 + h, c * C:(c + 1) * C].reshape(1, 1, C)
                            for h in range(H) for c in range(G)], axis=0)
    ii = lax.broadcasted_iota(jnp.int32, (1, C, C), 1)
    jj = lax.broadcasted_iota(jnp.int32, (1, C, C), 2)
    incl = (jj >= ii) if rev else (jj <= ii)
    strict = (jj > ii) if rev else (jj < ii)
    decay = jnp.exp(jnp.where(incl, gcc - grow, NEG_BIG))
    kk = _bmm_nt(k, k)
    qk = _bmm_nt(q, k)
    yield
    nmat = jnp.where(strict, kk * decay, 0.0) * beta
    eg = jnp.exp(gcc)
    rhs = jnp.concatenate([v * beta, kf * (beta * eg)], axis=-1)
    m = _bmm(nmat, jnp.concatenate([rhs, nmat], axis=-1))
    yield
    x = rhs - m[:, :, 0:2 * HEAD_DIM]
    p = m[:, :, 2 * HEAD_DIM:]
    for _ in range(4):
        m = _bmm(p, jnp.concatenate([x, p], axis=-1))
        yield
        x = x + m[:, :, 0:2 * HEAD_DIM]
        p = m[:, :, 2 * HEAD_DIM:]
    x = x + _bmm(p, x)
    yield
    u = x[:, :, 0:HEAD_DIM]
    w = x[:, :, HEAD_DIM:2 * HEAD_DIM]
    qkm = jnp.where(incl, qk * decay, 0.0)
    k_tail_t = jnp.swapaxes(kf * jnp.exp(gl - gcc), 1, 2)
    lhs1 = jnp.concatenate([w, qf * eg], axis=1).astype(BF16)
    lhs2 = jnp.concatenate([qkm, k_tail_t], axis=1).astype(BF16)
    dch = jnp.exp(gl[:, 0:8, :])
    out.extend([u, lhs1, lhs2, dch])


def _gdn_scan_group(u_ref, l1_ref, l2_ref, d_ref, s_ref, o_ref, rev):
    G, C, H = GDN_G, GDN_CHUNK, GDN_HEADS
    S = s_ref[...]
    for c in (range(G - 1, -1, -1) if rev else range(G)):
        r1 = _bmm(l1_ref[:, c], S)
        yield
        vn = u_ref[:, c] - r1[:, 0:C]
        r2 = _bmm(l2_ref[:, c], vn)
        yield
        oc = r1[:, C:2 * C] + r2[:, 0:C]
        S = S * d_ref[:, c][:, 0:1, :] + r2[:, C:2 * C]
        o_ref[c * C:(c + 1) * C, :] = jnp.concatenate([oc[h] for h in range(H)], axis=-1)
    s_ref[...] = S


def _gdn_kernel(qkvf_ref, gbf_ref, gbtf_ref, qkvb_ref, gbb_ref, gbtb_ref, of_ref, ob_ref,
                sf_ref, sb_ref, uf_ref, l1f_ref, l2f_ref, df_ref, ub_ref, l1b_ref, l2b_ref, db_ref):
    G, C, H = GDN_G, GDN_CHUNK, GDN_HEADS
    step = pl.program_id(0)

    @pl.when(step == 0)
    def _():
        for r in (sf_ref, sb_ref, uf_ref, l1f_ref, l2f_ref, df_ref, ub_ref, l1b_ref, l2b_ref, db_ref):
            r[...] = jnp.zeros_like(r)

    terms_f, terms_b = [], []
    strands = [_gdn_scan_group(uf_ref, l1f_ref, l2f_ref, df_ref, sf_ref, of_ref, False),
               _gdn_scan_group(ub_ref, l1b_ref, l2b_ref, db_ref, sb_ref, ob_ref, True),
               _gdn_chunk_terms(qkvf_ref[...], gbf_ref[...], gbtf_ref[...], False, terms_f),
               _gdn_chunk_terms(qkvb_ref[...], gbb_ref[...], gbtb_ref[...], True, terms_b)]
    while strands:
        strands = [s for s in strands if next(s, True) is None]
    for (u, l1, l2, dch), (u_ref, l1_ref, l2_ref, d_ref) in ((terms_f, (uf_ref, l1f_ref, l2f_ref, df_ref)),
                                                          (terms_b, (ub_ref, l1b_ref, l2b_ref, db_ref))):
        u_ref[...] = u.reshape(H, G, C, HEAD_DIM)
        l1_ref[...] = l1.reshape(H, G, 2 * C, HEAD_DIM)
        l2_ref[...] = l2.reshape(H, G, 2 * C, C)
        d_ref[...] = dch.reshape(H, G, 8, HEAD_DIM)


def _gdn(qkv, gb, gbt):
    L = qkv.shape[1]
    G, C, H = GDN_G, GDN_CHUNK, GDN_HEADS
    T = G * C
    n = L // T
    fin = lambda s: jnp.minimum(s, n - 1)
    bin_ = lambda s: jnp.maximum(n - 1 - s, 0)
    fout = lambda s: jnp.maximum(s - 1, 0)
    bout = lambda s: jnp.minimum(n - s, n - 1)
    term_scratch = [pltpu.VMEM((H, G, C, HEAD_DIM), F32), pltpu.VMEM((H, G, 2 * C, HEAD_DIM), BF16),
                    pltpu.VMEM((H, G, 2 * C, C), BF16), pltpu.VMEM((H, G, 8, HEAD_DIM), F32)]
    return pl.pallas_call(
        _gdn_kernel,
        grid=(n + 1,),
        in_specs=[pl.BlockSpec((3 * H, T, HEAD_DIM), lambda s: (0, fin(s), 0)),
                  pl.BlockSpec((T, LANES), lambda s: (fin(s), 0)),
                  pl.BlockSpec((24, T), lambda s: (0, fin(s))),
                  pl.BlockSpec((3 * H, T, HEAD_DIM), lambda s: (0, bin_(s), 0)),
                  pl.BlockSpec((T, LANES), lambda s: (bin_(s), 0)),
                  pl.BlockSpec((24, T), lambda s: (0, bin_(s)))],
        out_specs=[pl.BlockSpec((T, 256), lambda s: (fout(s), 0)),
                   pl.BlockSpec((T, 256), lambda s: (bout(s), 0))],
        out_shape=[jax.ShapeDtypeStruct((L, 256), F32), jax.ShapeDtypeStruct((L, 256), F32)],
        scratch_shapes=[pltpu.VMEM((H, HEAD_DIM, HEAD_DIM), F32), pltpu.VMEM((H, HEAD_DIM, HEAD_DIM), F32)]
                       + term_scratch + term_scratch,
        compiler_params=_cparams(("arbitrary",)),
        name="gdn_scan",
    )(qkv, gb, gbt, qkv, gb, gbt)


def _merge_kernel(x_ref, yna_ref, yswa_ref, of_ref, ob_ref, z_ref, gate_ref, gn_ref, bd_ref,
                  wna_ref, wswa_ref, wgdn_ref, wout_ref, o_ref):
    o = of_ref[...] + ob_ref[...]
    z = z_ref[...].astype(F32)
    ms = _group_sum(o * o, bd_ref) * (1.0 / HEAD_DIM)
    ygdn = o * lax.rsqrt(ms + NORM_EPS) * gn_ref[...] * (z * _sigmoid(z))
    m = (gate_ref[:, 0:D_MODEL].astype(F32) * _dot(yna_ref[...], wna_ref[...])
         + gate_ref[:, D_MODEL:2 * D_MODEL].astype(F32) * _dot(yswa_ref[...], wswa_ref[...])
         + gate_ref[:, 2 * D_MODEL:3 * D_MODEL].astype(F32) * _dot(ygdn.astype(BF16), wgdn_ref[...]))
    o_ref[...] = x_ref[...] + _dot(m.astype(BF16), wout_ref[...])


def _merge(x, yna, yswa, of, ob, plain, gate, l, gn, bd, wna, wswa, wgdn, wout):
    L = x.shape[0]
    row = lambda w_, c=0: pl.BlockSpec((TM, w_), lambda i: (i, c))
    return pl.pallas_call(
        _merge_kernel,
        grid=(L // TM,),
        in_specs=[row(D_MODEL), row(256), row(512), row(256), row(256), row(256, 4), row(SEG_GATE),
                  _layer_spec(gn, l), _const_spec(bd), _layer_spec(wna, l), _layer_spec(wswa, l),
                  _layer_spec(wgdn, l), _layer_spec(wout, l)],
        out_specs=row(D_MODEL),
        out_shape=jax.ShapeDtypeStruct((L, D_MODEL), F32),
        compiler_params=_cparams(("parallel",)),
        name="merge",
    )(x, yna, yswa, of, ob, plain, gate, gn, bd, wna, wswa, wgdn, wout)


def _ffn_kernel(x_ref, xp_ref, xn_ref, g_ref, wu_ref, cw_ref, cb_ref, wd_ref, o_ref, hbuf, *, nblk):
    i = pl.program_id(0)

    def normed(xv):
        ms = jnp.mean(xv * xv, axis=-1, keepdims=True)
        return xv * lax.rsqrt(ms + NORM_EPS) * g_ref[...]

    hbuf[0:HALO, :] = jnp.where(i > 0, normed(xp_ref[...]), 0.0).astype(BF16)
    hbuf[HALO:HALO + TM, :] = normed(x_ref[...]).astype(BF16)
    hbuf[HALO + TM:, :] = jnp.where(i < nblk - 1, normed(xn_ref[...]), 0.0).astype(BF16)
    h = hbuf[...]
    n = TM + 2 * HALO
    chunks = [slice(j * FFN_NC, (j + 1) * FFN_NC) for j in range(D_FF // FFN_NC)]
    gate_cols = [slice(D_FF + j * FFN_NC, D_FF + (j + 1) * FFN_NC) for j in range(D_FF // FFN_NC)]

    def conv(u, cols):
        y = (pltpu.roll(u, 1, 0) * cw_ref[0:1, cols] + u * cw_ref[1:2, cols]
             + pltpu.roll(u, n - 1, 0) * cw_ref[2:3, cols])
        return y[HALO:HALO + TM] + cb_ref[:, cols]

    ups = [(_dot(h, wu_ref[:, ca]), _dot(h, wu_ref[:, cb])) for ca, cb in zip(chunks, gate_cols)]
    acc = x_ref[...]
    for (ua, ub), ca, cb in zip(ups, chunks, gate_cols):
        a = conv(ua, ca)
        gated = (a * _sigmoid(a) * conv(ub, cb)).astype(BF16)
        acc = acc + _dot(gated, wd_ref[ca, :])
    o_ref[...] = acc


def _ffn(x, l, g, w_up, conv_w, conv_b, w_down):
    L = x.shape[0]
    nblk = L // TM
    hb = TM // HALO
    once = pl.Buffered(1)
    return pl.pallas_call(
        functools.partial(_ffn_kernel, nblk=nblk),
        grid=(nblk,),
        in_specs=[pl.BlockSpec((TM, D_MODEL), lambda i: (i, 0)),
                  pl.BlockSpec((HALO, D_MODEL), lambda i: (jnp.maximum(i * hb - 1, 0), 0)),
                  pl.BlockSpec((HALO, D_MODEL), lambda i: (jnp.minimum((i + 1) * hb, L // HALO - 1), 0)),
                  _layer_spec(g, l), _layer_spec(w_up, l, pipeline_mode=once), _layer_spec(conv_w, l),
                  _layer_spec(conv_b, l), _layer_spec(w_down, l, pipeline_mode=once)],
        out_specs=pl.BlockSpec((TM, D_MODEL), lambda i: (i, 0)),
        out_shape=jax.ShapeDtypeStruct((L, D_MODEL), F32),
        scratch_shapes=[pltpu.VMEM((TM + 2 * HALO, D_MODEL), BF16)],
        compiler_params=_cparams(("parallel",)),
        name="ffn",
    )(x, x, x, g, w_up, conv_w, conv_b, w_down)


def _pack_w_in(w):
    qa, ka, va = w[..., 0:256], w[..., 256:512], w[..., 512:768]
    qs, ks, vs = w[..., 768:1280], w[..., 1280:1408], w[..., 1408:1536]
    qkv_c, z_c = w[..., 1536:2304], w[..., 2304:2560]
    ba, gate = w[..., 2560:2576], w[..., 2576:5648]
    pad = jnp.zeros(w.shape[:-1] + (SEG_BA - 16,), w.dtype)
    return jnp.concatenate([qa, ka, qs, ks, qkv_c, va, z_c, vs, gate, ba, pad], axis=-1).astype(BF16)


def _rope_tables(L):
    inv = 1.0 / (ROPE_THETA ** (jnp.arange(0, HEAD_DIM, 2, dtype=F32) / HEAD_DIM))
    ang = jnp.arange(L, dtype=F32)[:, None] * inv[None, :]
    cos, sin = jnp.cos(ang), jnp.sin(ang)
    cos128 = jnp.concatenate([cos, cos, cos, cos], axis=1)
    sin128 = jnp.concatenate([-sin, sin, -sin, sin], axis=1)
    return cos128, sin128


def kernel(x, attn_norm, w_in, qk_norm, na_rpb, swa_sink, gdn_conv_w, gdn_a_log, gdn_dt_bias, gdn_norm,
           w_branch_na, w_branch_swa, w_branch_gdn, w_out, ffn_norm, w_up, ffn_conv_w, ffn_conv_b, w_down):
    B, L, D = x.shape
    assert B == 1 and D == D_MODEL and L % (NA_ROWS * GRID_W) == 0 and L // GRID_W >= 2 * NA_ROWS
    depth = w_in.shape[0]
    cos128, sin128 = _rope_tables(L)
    blockdiag = jnp.asarray(np.kron(np.eye(4), np.ones((HEAD_DIM, HEAD_DIM))), BF16)
    bias_tabs = _na_bias_tables(na_rpb, L // GRID_W)
    swa_band = _swa_band_tables()
    scale = HEAD_DIM ** -0.5 * LOG2E
    G = SWA_Q_HEADS // SWA_KV_HEADS
    gain = jnp.concatenate([jnp.tile(qk_norm[:, 0] * scale, (1, NA_HEADS)), jnp.tile(qk_norm[:, 1], (1, NA_HEADS)),
                            jnp.tile(qk_norm[:, 2] * scale, (1, SWA_Q_HEADS)),
                            jnp.tile(qk_norm[:, 3], (1, SWA_KV_HEADS))], axis=1)[:, None, :]
    sink_row = jnp.repeat(swa_sink.reshape(depth, SWA_KV_HEADS, G) * LOG2E, SWA_BLOCK, axis=2)[:, :, None, :]
    alog_row = jnp.pad(gdn_a_log.reshape(depth, 1, 8), ((0, 0), (0, 0), (8, LANES - 16)))
    dtb_row = jnp.pad(gdn_dt_bias.reshape(depth, 1, 8), ((0, 0), (0, 0), (8, LANES - 16)))
    gdn_gain = jnp.tile(gdn_norm, (1, GDN_HEADS))[:, None, :]
    attn_g = attn_norm[:, None, :]
    ffn_g = ffn_norm[:, None, :]
    ffn_b = ffn_conv_b[:, None, :]
    w_in_p = _pack_w_in(w_in.astype(BF16))
    w_na, w_swa, w_gdn = w_branch_na.astype(BF16), w_branch_swa.astype(BF16), w_branch_gdn.astype(BF16)
    w_o, w_u, w_d = w_out.astype(BF16), w_up.astype(BF16), w_down.astype(BF16)
    xs = x[0]
    for l in range(depth):
        qk, plain, gate, ba, va_t, vs_t = _inproj(xs, l, attn_g, w_in_p, gain, cos128, sin128, blockdiag)
        y_na = _na_attention(qk, va_t, l, bias_tabs)
        y_swa = _swa_attention(qk, vs_t, l, swa_band, sink_row)
        qkv_n, gb, gbt = _gdn_prep(plain, ba, l, gdn_conv_w, alog_row, dtb_row, blockdiag)
        o_f, o_b = _gdn(qkv_n, gb, gbt)
        xs = _merge(xs, y_na, y_swa, o_f, o_b, plain, gate, l, gdn_gain, blockdiag, w_na, w_swa, w_gdn, w_o)
        xs = _ffn(xs, l, ffn_g, w_u, ffn_conv_w, ffn_b, w_d)
    return xs[None]
```

```python
import functools

import numpy as np
import jax
import jax.numpy as jnp
from jax import lax
from jax.experimental import pallas as pl
from jax.experimental.pallas import tpu as pltpu

F32 = jnp.float32
BF16 = jnp.bfloat16

D_MODEL = 1024
HEAD_DIM = 64
GRID_W = 64
NORM_EPS = 1e-6
NA_HEADS = 4
NA_KH = 8
NA_KW = 16
SWA_Q_HEADS = 8
SWA_KV_HEADS = 2
SWA_BLOCK = 128
ROPE_THETA = 10000.0
GDN_HEADS = 4
GDN_CHUNK = 64
D_FF = 2816

LANES = 128
NEG_BIG = -1e30
LOG2E = 1.4426950408889634
VMEM_LIMIT = 56 * 1024 * 1024

SEG_QK = 1152
SEG_PLAIN = 1408
PLAIN_VA = 768
PLAIN_VS = 1280
SEG_GATE = 3072
SEG_BA = 128
IN_PACKED = SEG_QK + SEG_PLAIN + SEG_GATE + SEG_BA

TM = 512
NA_ROWS = 16
NA_WIN = NA_KH + 2
SWA_QB = 8
GDN_G = 4
FFN_NC = 1408
HALO = 8


def _cparams(sem):
    return pltpu.CompilerParams(dimension_semantics=sem, vmem_limit_bytes=VMEM_LIMIT)


def _dot(a, b):
    return jnp.dot(a, b, preferred_element_type=F32)


def _sigmoid(x):
    return 1.0 / (1.0 + jnp.exp(-x))


def _group_sum(sq, bd_ref):
    w = sq.shape[-1]
    return _dot(sq.astype(BF16), bd_ref[0:w, 0:w])


def _inproj_kernel(x_ref, g_ref, w_ref, gain_ref, rope_blk_ref, rope_off_ref, bd_ref,
                   oqk_ref, oplain_ref, ogate_ref, oba_ref, ovat_ref, ovst_ref):
    x = x_ref[...]
    ms = jnp.mean(x * x, axis=-1, keepdims=True)
    h = (x * lax.rsqrt(ms + NORM_EPS) * g_ref[...]).astype(BF16)
    cw = 2 * LANES
    lane = lax.broadcasted_iota(jnp.int32, (1, cw), 1)
    first_half = (lane % HEAD_DIM) < (HEAD_DIM // 2)
    c_a, s_a = rope_blk_ref[0:1, :], rope_blk_ref[1:2, :]
    cos128 = c_a * rope_off_ref[0] - s_a * rope_off_ref[1]
    sin128 = s_a * rope_off_ref[2] + c_a * rope_off_ref[3]
    cos = jnp.concatenate([cos128, cos128], axis=1)
    sin = jnp.concatenate([sin128, sin128], axis=1)

    def head_norm(t, c):
        cols = slice(c * cw, min((c + 1) * cw, SEG_QK))
        w = cols.stop - cols.start
        tc = t[:, cols]
        ss = _group_sum(tc * tc, bd_ref)
        y = tc * lax.rsqrt(ss * (1.0 / HEAD_DIM) + NORM_EPS) * gain_ref[:, cols]
        if cols.start >= 512:
            rot = jnp.where(first_half[:, 0:w], pltpu.roll(y, w - HEAD_DIM // 2, 1), pltpu.roll(y, HEAD_DIM // 2, 1))
            y = y * cos[:, 0:w] + rot * sin[:, 0:w]
        oqk_ref[:, cols] = y.astype(BF16)

    o_plain, o_gate, o_ba = SEG_QK, SEG_QK + SEG_PLAIN, SEG_QK + SEG_PLAIN + SEG_GATE
    t = _dot(h, w_ref[:, 0:SEG_QK])
    tp = _dot(h, w_ref[:, o_plain:o_plain + SEG_PLAIN])
    for c in (0, 1):
        head_norm(t, c)
    tg = _dot(h, w_ref[:, o_gate:o_gate + D_MODEL])
    oplain_ref[...] = tp.astype(BF16)
    for grp in range(TM // LANES):
        rows = slice(grp * LANES, (grp + 1) * LANES)
        ovat_ref[grp] = tp[rows, PLAIN_VA:PLAIN_VA + 256].T.astype(BF16)
        ovst_ref[grp] = tp[rows, PLAIN_VS:PLAIN_VS + 128].T.astype(BF16)
    for c in (2, 3):
        head_norm(t, c)
    for c in range(3):
        nxt = (_dot(h, w_ref[:, o_gate + (c + 1) * D_MODEL:o_gate + (c + 2) * D_MODEL]) if c < 2
               else _dot(h, w_ref[:, o_ba:o_ba + SEG_BA]))
        ogate_ref[:, c * D_MODEL:(c + 1) * D_MODEL] = _sigmoid(tg).astype(BF16)
        if c == 0:
            head_norm(t, 4)
        tg = nxt
    oba_ref[...] = tg


def _layer_spec(a, l, **kw):
    nd = a.ndim - 1
    return pl.BlockSpec((None,) + a.shape[1:], lambda *_: (l,) + (0,) * nd, **kw)


def _const_spec(a):
    return pl.BlockSpec(a.shape, lambda *_: (0,) * a.ndim)


def _inproj(x, l, g, w, gain, rope_blk, rope_off, bd):
    L = x.shape[0]
    row = lambda w_: pl.BlockSpec((TM, w_), lambda i: (i, 0))
    return pl.pallas_call(
        _inproj_kernel,
        grid=(L // TM,),
        in_specs=[row(D_MODEL), _layer_spec(g, l), _layer_spec(w, l, pipeline_mode=pl.Buffered(1)),
                  _layer_spec(gain, l), pl.BlockSpec((None, 2, LANES), lambda i: (i, 0, 0)), _const_spec(rope_off),
                  _const_spec(bd)],
        out_specs=[row(SEG_QK), row(SEG_PLAIN), row(SEG_GATE), row(SEG_BA),
                   pl.BlockSpec((TM // LANES, 256, LANES), lambda i: (i, 0, 0)),
                   pl.BlockSpec((TM // LANES, 128, LANES), lambda i: (i, 0, 0))],
        out_shape=[jax.ShapeDtypeStruct((L, SEG_QK), BF16),
                   jax.ShapeDtypeStruct((L, SEG_PLAIN), BF16),
                   jax.ShapeDtypeStruct((L, SEG_GATE), BF16),
                   jax.ShapeDtypeStruct((L, SEG_BA), F32),
                   jax.ShapeDtypeStruct((L // LANES, 256, LANES), BF16),
                   jax.ShapeDtypeStruct((L // LANES, 128, LANES), BF16)],
        compiler_params=_cparams(("parallel",)),
        name="inproj",
    )(x, g, w, gain, rope_blk, rope_off, bd)


def _toeplitz_kernel(rpb_ref, onehot_ref, colmask_ref, o_ref):
    r = rpb_ref[...]
    hi = r.astype(BF16)
    r1 = r - hi.astype(F32)
    mid = r1.astype(BF16)
    lo = (r1 - mid.astype(F32)).astype(BF16)
    oh = onehot_ref[...]
    w = _dot(hi, oh) + _dot(mid, oh) + _dot(lo, oh)
    o_ref[...] = jnp.where(colmask_ref[...] > 0.0, w * LOG2E, NEG_BIG)


def _na_assemble_kernel(tiles_ref, o_ref, *, a_idx):
    neg = jnp.full((GRID_W, GRID_W), NEG_BIG, F32)
    for t in range(a_idx.shape[0]):
        for j in range(NA_WIN):
            for e in range(2):
                a = int(a_idx[t, j, e])
                o_ref[t, j * GRID_W:(j + 1) * GRID_W, e * GRID_W:(e + 1) * GRID_W] = neg if a < 0 else tiles_ref[a]


def _na_bias_tables(na_rpb, rows):
    depth, H, nr, nc = na_rpb.shape
    kc = np.arange(GRID_W)[:, None]
    qc = np.arange(GRID_W)[None, :]
    dc = np.clip(kc - qc + (NA_KW - 1), 0, 2 * NA_KW - 2).reshape(-1)
    onehot = np.zeros((32, GRID_W * GRID_W), np.float32)
    onehot[dc, np.arange(GRID_W * GRID_W)] = 1.0
    col_start = np.clip(qc - NA_KW // 2, 0, GRID_W - NA_KW)
    colmask = ((kc >= col_start) & (kc < col_start + NA_KW)).astype(np.float32).reshape(1, -1)
    rpb_rows = jnp.pad(na_rpb, ((0, 0), (0, 0), (0, 16 - nr), (0, 32 - nc))).reshape(depth * H * 16, 32)
    tiles = pl.pallas_call(
        _toeplitz_kernel,
        out_shape=jax.ShapeDtypeStruct((depth * H * 16, GRID_W * GRID_W), F32),
        name="na_bias_tiles",
    )(rpb_rows, jnp.asarray(onehot, BF16), jnp.asarray(colmask))
    tiles = tiles.reshape(depth * H, 16, GRID_W, GRID_W)
    a_idx = np.stack([_na_pair_structure(r0, rows)[1] for r0 in _na_pair_type_rows(rows)])
    nt = a_idx.shape[0]
    tabs = pl.pallas_call(
        functools.partial(_na_assemble_kernel, a_idx=a_idx),
        grid=(depth * H,),
        in_specs=[pl.BlockSpec((None, 16, GRID_W, GRID_W), lambda i: (i, 0, 0, 0))],
        out_specs=pl.BlockSpec((None, nt, NA_WIN * GRID_W, 2 * GRID_W), lambda i: (i, 0, 0, 0)),
        out_shape=jax.ShapeDtypeStruct((depth * H, nt, NA_WIN * GRID_W, 2 * GRID_W), F32),
        compiler_params=_cparams(("parallel",)),
        name="na_bias_tables",
    )(tiles)
    return tabs.reshape(depth, H, nt, NA_WIN * GRID_W, 2 * GRID_W)


def _na_pair_structure(r0, rows):
    wstart = int(np.clip(r0 - NA_KH // 2, 0, rows - NA_KH - 1)) // 2 * 2
    a_idx = np.full((NA_WIN, 2), -1, np.int64)
    for e in range(2):
        rr = r0 + e
        rs = int(np.clip(rr - NA_KH // 2, 0, rows - NA_KH))
        for j in range(NA_WIN):
            krow = wstart + j
            if rs <= krow < rs + NA_KH:
                a_idx[j, e] = krow - rr + NA_KH - 1
    return wstart, a_idx


def _na_pair_type_rows(rows):
    reps = [0, 2, 4, rows - 4, rows - 2]
    for r0 in range(0, rows, 2):
        t = 3 + (r0 - (rows - 4)) // 2 if r0 >= rows - 4 else min(r0 // 2, 2)
        ws, a = _na_pair_structure(r0, rows)
        ws_t, a_t = _na_pair_structure(reps[t], rows)
        assert (a == a_t).all() and r0 - ws == reps[t] - ws_t
    return reps


def _na_kernel(q_ref, kp_ref, kc_ref, kn_ref, vp_ref, vc_ref, vn_ref, bias_ref, o_ref, kbuf, vbuf, *, rows):
    i = pl.program_id(0)
    blk = NA_ROWS * GRID_W
    grp = blk // LANES
    kbuf[0:blk, :] = kp_ref[...]
    kbuf[blk:2 * blk, :] = kc_ref[...]
    kbuf[2 * blk:3 * blk, :] = kn_ref[...]
    vbuf[0:grp] = vp_ref[...]
    vbuf[grp:2 * grp] = vc_ref[...]
    vbuf[2 * grp:3 * grp] = vn_ref[...]
    win = NA_WIN * GRID_W
    wgrp = win // LANES
    pair = 2 * GRID_W

    npair = NA_ROWS // 2
    goffs, types = [], []
    for pp in range(npair):
        r0 = i * NA_ROWS + 2 * pp
        wstart = jnp.clip(r0 - NA_KH // 2, 0, rows - NA_KH - 1) // 2 * 2
        types.append(jnp.where(r0 >= rows - 4, 3 + (r0 - (rows - 4)) // 2, jnp.minimum(r0 // 2, 2)))
        goffs.append((wstart - (i - 1) * NA_ROWS) // 2)

    def logits(pp, h):
        sl = slice(h * HEAD_DIM, (h + 1) * HEAD_DIM)
        kw = kbuf[pl.ds(pl.multiple_of(goffs[pp] * LANES, LANES), win), sl]
        s = lax.dot_general(kw, q_ref[pp * pair:(pp + 1) * pair, sl],
                            (((1,), (1,)), ((), ())), preferred_element_type=F32)
        return s + bias_ref[h, types[pp]]

    def softmax(s):
        m = jnp.max(s, axis=0, keepdims=True)
        p = jnp.exp2(s - m)
        return p.astype(BF16), 1.0 / jnp.sum(p, axis=0, keepdims=True)

    def weighted(p_inv, pp, h):
        p, inv_l = p_inv
        vt = jnp.concatenate([vbuf[goffs[pp] + j, h * HEAD_DIM:(h + 1) * HEAD_DIM, :] for j in range(wgrp)], axis=-1)
        return _dot(vt, p) * inv_l

    tiles = [(pp, h) for pp in range(npair) for h in range(NA_HEADS)]
    s_q, p_q, outs = {}, {}, {}
    for n in range(len(tiles) + 2):
        if n < len(tiles):
            s_q[n] = logits(*tiles[n])
        if 1 <= n <= len(tiles):
            p_q[n - 1] = softmax(s_q.pop(n - 1))
        if n >= 2:
            pp, h = tiles[n - 2]
            outs[(pp, h)] = weighted(p_q.pop(n - 2), pp, h)
            if h == NA_HEADS - 1:
                o_t = jnp.concatenate([outs.pop((pp, hh)) for hh in range(NA_HEADS)], axis=0)
                o_ref[pp * pair:(pp + 1) * pair, :] = o_t.T.astype(BF16)


def _na_attention(qk, va_t, l, bias):
    L = qk.shape[0]
    rows = L // GRID_W
    nblk = rows // NA_ROWS
    blk = NA_ROWS * GRID_W
    grp = blk // LANES
    w = NA_HEADS * HEAD_DIM
    prev = lambda i: jnp.maximum(i - 1, 0)
    nxt = lambda i: jnp.minimum(i + 1, nblk - 1)
    spec = lambda f, c: pl.BlockSpec((blk, w), lambda i: (f(i), c))
    vspec = lambda f: pl.BlockSpec((grp, w, LANES), lambda i: (f(i), 0, 0))
    same = lambda i: i
    return pl.pallas_call(
        functools.partial(_na_kernel, rows=rows),
        grid=(nblk,),
        in_specs=[spec(same, 0),
                  spec(prev, 1), spec(same, 1), spec(nxt, 1),
                  vspec(prev), vspec(same), vspec(nxt),
                  _layer_spec(bias, l)],
        out_specs=pl.BlockSpec((blk, w), lambda i: (i, 0)),
        out_shape=jax.ShapeDtypeStruct((L, w), BF16),
        scratch_shapes=[pltpu.VMEM((3 * blk, w), BF16), pltpu.VMEM((3 * grp, w, LANES), BF16)],
        compiler_params=_cparams(("parallel",)),
        name="na_attn",
    )(qk, qk, qk, qk, va_t, va_t, va_t, bias)


def _swa_kernel(q_ref, kp_ref, kc_ref, kn_ref, vp_ref, vc_ref, vn_ref, band_ref, sink_ref, o_ref, *, nstep):
    i = pl.program_id(0)
    B = SWA_BLOCK
    G = SWA_Q_HEADS // SWA_KV_HEADS
    k_all = jnp.concatenate([kp_ref[...], kc_ref[...], kn_ref[...]], axis=0)
    vt_all = [vp_ref[0]] + [vc_ref[j] for j in range(SWA_QB)] + [vn_ref[0]]

    def band_index(b):
        if b == 0:
            return jnp.where(i == 0, 1, 0)
        if b == SWA_QB - 1:
            return jnp.where(i == nstep - 1, 2, 0)
        return 0

    def logits(b, g):
        ks = k_all[b * B:(b + 3) * B, g * HEAD_DIM:(g + 1) * HEAD_DIM]
        qs = jnp.concatenate([q_ref[b * B:(b + 1) * B, (g * G + hh) * HEAD_DIM:(g * G + hh + 1) * HEAD_DIM]
                              for hh in range(G)], axis=0)
        return lax.dot_general(ks, qs, (((1,), (1,)), ((), ())), preferred_element_type=F32) + band_ref[band_index(b)]

    def softmax(s, g):
        sink = sink_ref[g]
        m = jnp.maximum(jnp.max(s, axis=0, keepdims=True), sink)
        p = jnp.exp2(s - m)
        l = jnp.sum(p, axis=0, keepdims=True) + jnp.exp2(sink - m)
        return p.astype(BF16), 1.0 / l

    def weighted(p_inv, b, g):
        p, inv_l = p_inv
        vt = jnp.concatenate([vt_all[b + j][g * HEAD_DIM:(g + 1) * HEAD_DIM, :] for j in range(3)], axis=-1)
        return _dot(vt, p) * inv_l

    tiles = [(b, g) for b in range(SWA_QB) for g in range(SWA_KV_HEADS)]
    s_q, p_q = {}, {}
    for n in range(len(tiles) + 2):
        if n < len(tiles):
            s_q[n] = logits(*tiles[n])
        if 1 <= n <= len(tiles):
            p_q[n - 1] = softmax(s_q.pop(n - 1), tiles[n - 1][1])
        if n >= 2:
            b, g = tiles[n - 2]
            o_t = weighted(p_q.pop(n - 2), b, g)
            for hh in range(0, G, 2):
                pair_t = jnp.concatenate([o_t[:, hh * B:(hh + 1) * B], o_t[:, (hh + 1) * B:(hh + 2) * B]], axis=0)
                h = g * G + hh
                o_ref[b * B:(b + 1) * B, h * HEAD_DIM:(h + 2) * HEAD_DIM] = pair_t.T.astype(BF16)


def _swa_band_tables():
    B = SWA_BLOCK
    G = SWA_Q_HEADS // SWA_KV_HEADS
    qi = np.arange(G * B)[None, :] % B
    kj = np.arange(3 * B)[:, None]
    band = np.abs(kj - B - qi) <= B
    tabs = [band, band & (kj >= B), band & (kj < 2 * B)]
    return jnp.asarray(np.where(np.stack(tabs), 0.0, NEG_BIG), F32)


def _swa_attention(qk, vs_t, l, band, sink_row):
    L = qk.shape[0]
    B = SWA_BLOCK
    T = SWA_QB * B
    nstep = L // T
    nblk = L // B
    prev = lambda i: jnp.maximum(i * SWA_QB - 1, 0)
    nxt = lambda i: jnp.minimum((i + 1) * SWA_QB, nblk - 1)
    return pl.pallas_call(
        functools.partial(_swa_kernel, nstep=nstep),
        grid=(nstep,),
        in_specs=[pl.BlockSpec((T, 512), lambda i: (i, 1)),
                  pl.BlockSpec((B, LANES), lambda i: (prev(i), 8)),
                  pl.BlockSpec((T, LANES), lambda i: (i, 8)),
                  pl.BlockSpec((B, LANES), lambda i: (nxt(i), 8)),
                  pl.BlockSpec((1, LANES, B), lambda i: (prev(i), 0, 0)),
                  pl.BlockSpec((SWA_QB, LANES, B), lambda i: (i, 0, 0)),
                  pl.BlockSpec((1, LANES, B), lambda i: (nxt(i), 0, 0)),
                  _const_spec(band), _layer_spec(sink_row, l)],
        out_specs=pl.BlockSpec((T, 512), lambda i: (i, 0)),
        out_shape=jax.ShapeDtypeStruct((L, 512), BF16),
        compiler_params=_cparams(("parallel",)),
        name="swa_attn",
    )(qk, qk, qk, qk, vs_t, vs_t, vs_t, band, sink_row)


def _gdn_prep_kernel(x_ref, hp_ref, hn_ref, cw_ref, ba_ref, alog_ref, dtb_ref, bd_ref,
                     qkv_ref, gb_ref, gbt_ref, *, nblk):
    i = pl.program_id(0)
    x = x_ref[...].astype(F32)
    rid = lax.broadcasted_iota(jnp.int32, (TM, 1), 0)
    hp = jnp.where(i > 0, hp_ref[HALO - 1:HALO, :].astype(F32), 0.0)
    hn = jnp.where(i < nblk - 1, hn_ref[0:1, :].astype(F32), 0.0)
    xprev = jnp.where(rid == 0, hp, pltpu.roll(x, 1, 0))
    xnext = jnp.where(rid == TM - 1, hn, pltpu.roll(x, TM - 1, 0))
    y = xprev * cw_ref[0:1, :] + x * cw_ref[1:2, :] + xnext * cw_ref[2:3, :]
    y = y * _sigmoid(y)
    for part in range(3):
        yp = y[:, part * 256:(part + 1) * 256]
        if part < 2:
            yp = yp * lax.rsqrt(_group_sum(yp * yp, bd_ref) + NORM_EPS)
        if part == 0:
            yp = yp * (HEAD_DIM ** -0.5)
        yp = yp.astype(BF16)
        for hh in range(GDN_HEADS):
            qkv_ref[part * GDN_HEADS + hh] = yp[:, hh * HEAD_DIM:(hh + 1) * HEAD_DIM]

    ba = ba_ref[...]
    lane = lax.broadcasted_iota(jnp.int32, (1, LANES), 1)
    beta = _sigmoid(ba)
    sp_in = ba + dtb_ref[...]
    softplus = jnp.maximum(sp_in, 0.0) + jnp.log(1.0 + jnp.exp(-jnp.abs(sp_in)))
    g = jnp.where((lane >= 8) & (lane < 16), -jnp.exp(alog_ref[...]) * softplus, 0.0)
    rc = rid % GDN_CHUNK
    pre = g
    suf = g
    s = 1
    while s < GDN_CHUNK:
        pre = pre + jnp.where(rc >= s, pltpu.roll(pre, s, 0), 0.0)
        suf = suf + jnp.where(rc < GDN_CHUNK - s, pltpu.roll(suf, TM - s, 0), 0.0)
        s *= 2
    tot = pre + suf - g
    gc = jnp.where(lane < 12, pre, suf)
    slab = jnp.where(lane < 8, beta, jnp.where(lane < 16, gc, jnp.where(lane < 24, pltpu.roll(tot, 8, 1), 0.0)))
    gb_ref[...] = slab
    gbt_ref[...] = slab.T[0:24, :]


def _gdn_prep(plain, ba, l, conv_w, alog_row, dtb_row, bd):
    L = plain.shape[0]
    nblk = L // TM
    hb = TM // HALO
    return pl.pallas_call(
        functools.partial(_gdn_prep_kernel, nblk=nblk),
        grid=(nblk,),
        in_specs=[pl.BlockSpec((TM, 768), lambda i: (i, 0)),
                  pl.BlockSpec((HALO, 768), lambda i: (jnp.maximum(i * hb - 1, 0), 0)),
                  pl.BlockSpec((HALO, 768), lambda i: (jnp.minimum((i + 1) * hb, L // HALO - 1), 0)),
                  _layer_spec(conv_w, l),
                  pl.BlockSpec((TM, LANES), lambda i: (i, 0)),
                  _layer_spec(alog_row, l), _layer_spec(dtb_row, l), _const_spec(bd)],
        out_specs=[pl.BlockSpec((3 * GDN_HEADS, TM, HEAD_DIM), lambda i: (0, i, 0)),
                   pl.BlockSpec((TM, LANES), lambda i: (i, 0)),
                   pl.BlockSpec((24, TM), lambda i: (0, i))],
        out_shape=[jax.ShapeDtypeStruct((3 * GDN_HEADS, L, HEAD_DIM), BF16),
                   jax.ShapeDtypeStruct((L, LANES), F32),
                   jax.ShapeDtypeStruct((24, L), F32)],
        compiler_params=_cparams(("parallel",)),
        name="gdn_prep",
    )(plain, plain, plain, conv_w, ba, alog_row, dtb_row, bd)


def _bmm(a, b):
    return jnp.einsum('bij,bjk->bik', a.astype(BF16), b.astype(BF16), preferred_element_type=F32)


def _bmm_nt(a, b):
    return jnp.einsum('bid,bjd->bij', a.astype(BF16), b.astype(BF16), preferred_element_type=F32)


def _gdn_chunk_terms(qkv, gb, gbt, rev, out):
    G, C, H = GDN_G, GDN_CHUNK, GDN_HEADS

    def heads(part):
        return qkv[part * H:(part + 1) * H].reshape(H * G, C, HEAD_DIM)

    q = heads(0)
    k = heads(1)
    v = heads(2).astype(F32)
    kf = k.astype(F32)
    qf = q.astype(F32)
    d0 = H if rev else 0

    def colv(base):
        return jnp.concatenate([gb[:, base + h:base + h + 1].reshape(G, C, 1) for h in range(H)], axis=0)

    beta = jnp.broadcast_to(colv(d0), (H * G, C, HEAD_DIM))
    gcc = jnp.broadcast_to(colv(8 + d0), (H * G, C, HEAD_DIM))
    gl = jnp.broadcast_to(colv(16 + d0), (H * G, C, HEAD_DIM))
    grow = jnp.concatenate([gbt[8 + d0 + h:9 + d0 + h, c * C:(c + 1) * C].reshape(1, 1, C)
                            for h in range(H) for c in range(G)], axis=0)
    ii = lax.broadcasted_iota(jnp.int32, (1, C, C), 1)
    jj = lax.broadcasted_iota(jnp.int32, (1, C, C), 2)
    incl = (jj >= ii) if rev else (jj <= ii)
    strict = (jj > ii) if rev else (jj < ii)
    decay = jnp.exp(jnp.where(incl, gcc - grow, NEG_BIG))
    kk = _bmm_nt(k, k)
    qk = _bmm_nt(q, k)
    yield
    nmat = jnp.where(strict, kk * decay, 0.0) * beta
    eg = jnp.exp(gcc)
    rhs = jnp.concatenate([v * beta, kf * (beta * eg)], axis=-1)
    m = _bmm(nmat, jnp.concatenate([rhs, nmat], axis=-1))
    yield
    x = rhs - m[:, :, 0:2 * HEAD_DIM]
    p = m[:, :, 2 * HEAD_DIM:]
    for _ in range(4):
        m = _bmm(p, jnp.concatenate([x, p], axis=-1))
        yield
        x = x + m[:, :, 0:2 * HEAD_DIM]
        p = m[:, :, 2 * HEAD_DIM:]
    x = x + _bmm(p, x)
    yield
    u = x[:, :, 0:HEAD_DIM]
    w = x[:, :, HEAD_DIM:2 * HEAD_DIM]
    qkm = jnp.where(incl, qk * decay, 0.0)
    k_tail_t = jnp.swapaxes(kf * jnp.exp(gl - gcc), 1, 2)
    lhs1 = jnp.concatenate([w, qf * eg], axis=1).astype(BF16)
    lhs2 = jnp.concatenate([qkm, k_tail_t], axis=1).astype(BF16)
    dch = jnp.exp(gl[:, 0:8, :])
    out.extend([u, lhs1, lhs2, dch])


def _gdn_scan_group(u_ref, l1_ref, l2_ref, d_ref, s_ref, o_ref, rev):
    G, C, H = GDN_G, GDN_CHUNK, GDN_HEADS
    S = s_ref[...]
    for c in (range(G - 1, -1, -1) if rev else range(G)):
        r1 = _bmm(l1_ref[:, c], S)
        yield
        vn = u_ref[:, c] - r1[:, 0:C]
        r2 = _bmm(l2_ref[:, c], vn)
        yield
        oc = r1[:, C:2 * C] + r2[:, 0:C]
        S = S * d_ref[:, c][:, 0:1, :] + r2[:, C:2 * C]
        o_ref[c * C:(c + 1) * C, :] = jnp.concatenate([oc[h] for h in range(H)], axis=-1)
    s_ref[...] = S


def _gdn_kernel(qkvf_ref, gbf_ref, gbtf_ref, qkvb_ref, gbb_ref, gbtb_ref, of_ref, ob_ref,
                sf_ref, sb_ref, uf_ref, l1f_ref, l2f_ref, df_ref, ub_ref, l1b_ref, l2b_ref, db_ref):
    G, C, H = GDN_G, GDN_CHUNK, GDN_HEADS
    step = pl.program_id(0)

    @pl.when(step == 0)
    def _():
        for r in (sf_ref, sb_ref, uf_ref, l1f_ref, l2f_ref, df_ref, ub_ref, l1b_ref, l2b_ref, db_ref):
            r[...] = jnp.zeros_like(r)

    terms_f, terms_b = [], []
    strands = [_gdn_scan_group(uf_ref, l1f_ref, l2f_ref, df_ref, sf_ref, of_ref, False),
               _gdn_scan_group(ub_ref, l1b_ref, l2b_ref, db_ref, sb_ref, ob_ref, True),
               _gdn_chunk_terms(qkvf_ref[...], gbf_ref[...], gbtf_ref[...], False, terms_f),
               _gdn_chunk_terms(qkvb_ref[...], gbb_ref[...], gbtb_ref[...], True, terms_b)]
    while strands:
        strands = [s for s in strands if next(s, True) is None]
    for (u, l1, l2, dch), (u_ref, l1_ref, l2_ref, d_ref) in ((terms_f, (uf_ref, l1f_ref, l2f_ref, df_ref)),
                                                          (terms_b, (ub_ref, l1b_ref, l2b_ref, db_ref))):
        u_ref[...] = u.reshape(H, G, C, HEAD_DIM)
        l1_ref[...] = l1.reshape(H, G, 2 * C, HEAD_DIM)
        l2_ref[...] = l2.reshape(H, G, 2 * C, C)
        d_ref[...] = dch.reshape(H, G, 8, HEAD_DIM)


def _gdn(qkv, gb, gbt):
    L = qkv.shape[1]
    G, C, H = GDN_G, GDN_CHUNK, GDN_HEADS
    T = G * C
    n = L // T
    fin = lambda s: jnp.minimum(s, n - 1)
    bin_ = lambda s: jnp.maximum(n - 1 - s, 0)
    fout = lambda s: jnp.maximum(s - 1, 0)
    bout = lambda s: jnp.minimum(n - s, n - 1)
    term_scratch = [pltpu.VMEM((H, G, C, HEAD_DIM), F32), pltpu.VMEM((H, G, 2 * C, HEAD_DIM), BF16),
                    pltpu.VMEM((H, G, 2 * C, C), BF16), pltpu.VMEM((H, G, 8, HEAD_DIM), F32)]
    return pl.pallas_call(
        _gdn_kernel,
        grid=(n + 1,),
        in_specs=[pl.BlockSpec((3 * H, T, HEAD_DIM), lambda s: (0, fin(s), 0)),
                  pl.BlockSpec((T, LANES), lambda s: (fin(s), 0)),
                  pl.BlockSpec((24, T), lambda s: (0, fin(s))),
                  pl.BlockSpec((3 * H, T, HEAD_DIM), lambda s: (0, bin_(s), 0)),
                  pl.BlockSpec((T, LANES), lambda s: (bin_(s), 0)),
                  pl.BlockSpec((24, T), lambda s: (0, bin_(s)))],
        out_specs=[pl.BlockSpec((T, 256), lambda s: (fout(s), 0)),
                   pl.BlockSpec((T, 256), lambda s: (bout(s), 0))],
        out_shape=[jax.ShapeDtypeStruct((L, 256), F32), jax.ShapeDtypeStruct((L, 256), F32)],
        scratch_shapes=[pltpu.VMEM((H, HEAD_DIM, HEAD_DIM), F32), pltpu.VMEM((H, HEAD_DIM, HEAD_DIM), F32)]
                       + term_scratch + term_scratch,
        compiler_params=_cparams(("arbitrary",)),
        name="gdn_scan",
    )(qkv, gb, gbt, qkv, gb, gbt)


def _merge_kernel(x_ref, yna_ref, yswa_ref, of_ref, ob_ref, z_ref, gate_ref, gn_ref, bd_ref,
                  wna_ref, wswa_ref, wgdn_ref, wout_ref, o_ref):
    o = of_ref[...] + ob_ref[...]
    z = z_ref[...].astype(F32)
    ms = _group_sum(o * o, bd_ref) * (1.0 / HEAD_DIM)
    ygdn = o * lax.rsqrt(ms + NORM_EPS) * gn_ref[...] * (z * _sigmoid(z))
    m = (gate_ref[:, 0:D_MODEL].astype(F32) * _dot(yna_ref[...], wna_ref[...])
         + gate_ref[:, D_MODEL:2 * D_MODEL].astype(F32) * _dot(yswa_ref[...], wswa_ref[...])
         + gate_ref[:, 2 * D_MODEL:3 * D_MODEL].astype(F32) * _dot(ygdn.astype(BF16), wgdn_ref[...]))
    o_ref[...] = x_ref[...] + _dot(m.astype(BF16), wout_ref[...])


def _merge(x, yna, yswa, of, ob, plain, gate, l, gn, bd, wna, wswa, wgdn, wout):
    L = x.shape[0]
    row = lambda w_, c=0: pl.BlockSpec((TM, w_), lambda i: (i, c))
    return pl.pallas_call(
        _merge_kernel,
        grid=(L // TM,),
        in_specs=[row(D_MODEL), row(256), row(512), row(256), row(256), row(256, 4), row(SEG_GATE),
                  _layer_spec(gn, l), _const_spec(bd), _layer_spec(wna, l), _layer_spec(wswa, l),
                  _layer_spec(wgdn, l), _layer_spec(wout, l)],
        out_specs=row(D_MODEL),
        out_shape=jax.ShapeDtypeStruct((L, D_MODEL), F32),
        compiler_params=_cparams(("parallel",)),
        name="merge",
    )(x, yna, yswa, of, ob, plain, gate, gn, bd, wna, wswa, wgdn, wout)


def _ffn_kernel(x_ref, xp_ref, xn_ref, g_ref, wu_ref, cw_ref, cb_ref, wd_ref, o_ref, hbuf, *, nblk):
    i = pl.program_id(0)

    def normed(xv):
        ms = jnp.mean(xv * xv, axis=-1, keepdims=True)
        return xv * lax.rsqrt(ms + NORM_EPS) * g_ref[...]

    hbuf[0:HALO, :] = jnp.where(i > 0, normed(xp_ref[...]), 0.0).astype(BF16)
    hbuf[HALO:HALO + TM, :] = normed(x_ref[...]).astype(BF16)
    hbuf[HALO + TM:, :] = jnp.where(i < nblk - 1, normed(xn_ref[...]), 0.0).astype(BF16)
    h = hbuf[...]
    n = TM + 2 * HALO
    chunks = [slice(j * FFN_NC, (j + 1) * FFN_NC) for j in range(D_FF // FFN_NC)]
    gate_cols = [slice(D_FF + j * FFN_NC, D_FF + (j + 1) * FFN_NC) for j in range(D_FF // FFN_NC)]

    def conv(u, cols):
        y = (pltpu.roll(u, 1, 0) * cw_ref[0:1, cols] + u * cw_ref[1:2, cols]
             + pltpu.roll(u, n - 1, 0) * cw_ref[2:3, cols])
        return y[HALO:HALO + TM] + cb_ref[:, cols]

    ups = [(_dot(h, wu_ref[:, ca]), _dot(h, wu_ref[:, cb])) for ca, cb in zip(chunks, gate_cols)]
    acc = x_ref[...]
    for (ua, ub), ca, cb in zip(ups, chunks, gate_cols):
        a = conv(ua, ca)
        gated = (a * _sigmoid(a) * conv(ub, cb)).astype(BF16)
        acc = acc + _dot(gated, wd_ref[ca, :])
    o_ref[...] = acc


def _ffn(x, l, g, w_up, conv_w, conv_b, w_down):
    L = x.shape[0]
    nblk = L // TM
    hb = TM // HALO
    once = pl.Buffered(1)
    return pl.pallas_call(
        functools.partial(_ffn_kernel, nblk=nblk),
        grid=(nblk,),
        in_specs=[pl.BlockSpec((TM, D_MODEL), lambda i: (i, 0)),
                  pl.BlockSpec((HALO, D_MODEL), lambda i: (jnp.maximum(i * hb - 1, 0), 0)),
                  pl.BlockSpec((HALO, D_MODEL), lambda i: (jnp.minimum((i + 1) * hb, L // HALO - 1), 0)),
                  _layer_spec(g, l), _layer_spec(w_up, l, pipeline_mode=once), _layer_spec(conv_w, l),
                  _layer_spec(conv_b, l), _layer_spec(w_down, l, pipeline_mode=once)],
        out_specs=pl.BlockSpec((TM, D_MODEL), lambda i: (i, 0)),
        out_shape=jax.ShapeDtypeStruct((L, D_MODEL), F32),
        scratch_shapes=[pltpu.VMEM((TM + 2 * HALO, D_MODEL), BF16)],
        compiler_params=_cparams(("parallel",)),
        name="ffn",
    )(x, x, x, g, w_up, conv_w, conv_b, w_down)


PACK_SEGMENTS = ((0, 256, 0), (256, 512, 256), (768, 1280, 512), (1280, 1408, 1024),
                 (1536, 2304, 1152), (512, 768, 1920), (2304, 2560, 2176), (1408, 1536, 2432),
                 (2576, 5648, 2560), (2560, 2576, 5632))
PACK_ROWS = 256


def _pack_kernel(w_ref, o_ref):
    for s0, s1, d0 in PACK_SEGMENTS:
        o_ref[:, d0:d0 + s1 - s0] = w_ref[:, s0:s1].astype(BF16)
    used = PACK_SEGMENTS[-1][2] + PACK_SEGMENTS[-1][1] - PACK_SEGMENTS[-1][0]
    o_ref[:, used:] = jnp.zeros((o_ref.shape[0], IN_PACKED - used), BF16)


def _pack_w_in(w):
    depth, k, n = w.shape
    return pl.pallas_call(
        _pack_kernel,
        grid=(depth, k // PACK_ROWS),
        in_specs=[pl.BlockSpec((None, PACK_ROWS, n), lambda l, i: (l, i, 0))],
        out_specs=pl.BlockSpec((None, PACK_ROWS, IN_PACKED), lambda l, i: (l, i, 0)),
        out_shape=jax.ShapeDtypeStruct((depth, k, IN_PACKED), BF16),
        compiler_params=_cparams(("parallel", "parallel")),
        name="pack_w_in",
    )(w)


def _rope_tables(L):
    inv = 1.0 / (ROPE_THETA ** (np.arange(0, HEAD_DIM, 2, dtype=np.float64) / HEAD_DIM))
    freq = np.tile(inv, LANES // inv.size)
    sign = np.where(np.arange(LANES) % HEAD_DIM < HEAD_DIM // 2, -1.0, 1.0)
    start = np.arange(L // TM, dtype=np.float64)[:, None] * TM * freq[None, :]
    off = np.arange(TM, dtype=np.float64)[:, None] * freq[None, :]
    blk = np.stack([np.cos(start), np.sin(start)], axis=1)
    offs = np.stack([np.cos(off), np.sin(off), sign * np.cos(off), sign * np.sin(off)])
    return jnp.asarray(blk, F32), jnp.asarray(offs, F32)


def kernel(x, attn_norm, w_in, qk_norm, na_rpb, swa_sink, gdn_conv_w, gdn_a_log, gdn_dt_bias, gdn_norm,
           w_branch_na, w_branch_swa, w_branch_gdn, w_out, ffn_norm, w_up, ffn_conv_w, ffn_conv_b, w_down):
    B, L, D = x.shape
    assert B == 1 and D == D_MODEL and L % (NA_ROWS * GRID_W) == 0 and L // GRID_W >= 2 * NA_ROWS
    depth = w_in.shape[0]
    rope_blk, rope_off = _rope_tables(L)
    blockdiag = jnp.asarray(np.kron(np.eye(4), np.ones((HEAD_DIM, HEAD_DIM))), BF16)
    bias_tabs = _na_bias_tables(na_rpb, L // GRID_W)
    swa_band = _swa_band_tables()
    scale = HEAD_DIM ** -0.5 * LOG2E
    G = SWA_Q_HEADS // SWA_KV_HEADS
    gain = jnp.concatenate([jnp.tile(qk_norm[:, 0] * scale, (1, NA_HEADS)), jnp.tile(qk_norm[:, 1], (1, NA_HEADS)),
                            jnp.tile(qk_norm[:, 2] * scale, (1, SWA_Q_HEADS)),
                            jnp.tile(qk_norm[:, 3], (1, SWA_KV_HEADS))], axis=1)[:, None, :]
    sink_row = jnp.repeat(swa_sink.reshape(depth, SWA_KV_HEADS, G) * LOG2E, SWA_BLOCK, axis=2)[:, :, None, :]
    alog_row = jnp.pad(gdn_a_log.reshape(depth, 1, 8), ((0, 0), (0, 0), (8, LANES - 16)))
    dtb_row = jnp.pad(gdn_dt_bias.reshape(depth, 1, 8), ((0, 0), (0, 0), (8, LANES - 16)))
    gdn_gain = jnp.tile(gdn_norm, (1, GDN_HEADS))[:, None, :]
    attn_g = attn_norm[:, None, :]
    ffn_g = ffn_norm[:, None, :]
    ffn_b = ffn_conv_b[:, None, :]
    w_in_p = _pack_w_in(w_in)
    w_na, w_swa, w_gdn = w_branch_na.astype(BF16), w_branch_swa.astype(BF16), w_branch_gdn.astype(BF16)
    w_o, w_u, w_d = w_out.astype(BF16), w_up.astype(BF16), w_down.astype(BF16)
    xs = x[0]
    for l in range(depth):
        qk, plain, gate, ba, va_t, vs_t = _inproj(xs, l, attn_g, w_in_p, gain, rope_blk, rope_off, blockdiag)
        y_na = _na_attention(qk, va_t, l, bias_tabs)
        y_swa = _swa_attention(qk, vs_t, l, swa_band, sink_row)
        qkv_n, gb, gbt = _gdn_prep(plain, ba, l, gdn_conv_w, alog_row, dtb_row, blockdiag)
        o_f, o_b = _gdn(qkv_n, gb, gbt)
        xs = _merge(xs, y_na, y_swa, o_f, o_b, plain, gate, l, gdn_gain, blockdiag, w_na, w_swa, w_gdn, w_o)
        xs = _ffn(xs, l, ffn_g, w_u, ffn_conv_w, ffn_b, w_d)
    return xs[None]
```

```python
import functools

import numpy as np
import jax
import jax.numpy as jnp
from jax import lax
from jax.experimental import pallas as pl
from jax.experimental.pallas import tpu as pltpu

F32 = jnp.float32
BF16 = jnp.bfloat16

D_MODEL = 1024
HEAD_DIM = 64
GRID_W = 64
NORM_EPS = 1e-6
NA_HEADS = 4
NA_KH = 8
NA_KW = 16
SWA_Q_HEADS = 8
SWA_KV_HEADS = 2
SWA_BLOCK = 128
ROPE_THETA = 10000.0
GDN_HEADS = 4
GDN_CHUNK = 64
D_FF = 2816

LANES = 128
NEG_BIG = -1e30
LOG2E = 1.4426950408889634
VMEM_LIMIT = 56 * 1024 * 1024

SEG_QK = 1152
SEG_PLAIN = 1408
PLAIN_VA = 768
PLAIN_Z = 1024
PLAIN_VS = 1280
SEG_GATE = 3072
SEG_BA = 128
IN_PACKED = SEG_QK + SEG_PLAIN + SEG_GATE + SEG_BA

TM = 512
NA_ROWS = 16
NA_WIN = NA_KH + 2
SWA_QB = 8
GDN_G = 4
FFN_NC = 1408
HALO = 8


def _cparams(sem):
    return pltpu.CompilerParams(dimension_semantics=sem, vmem_limit_bytes=VMEM_LIMIT)


def _dot(a, b):
    return jnp.dot(a, b, preferred_element_type=F32)


def _sigmoid(x):
    return 1.0 / (1.0 + jnp.exp(-x))


def _group_sum(sq, bd_ref):
    w = sq.shape[-1]
    return _dot(sq.astype(BF16), bd_ref[0:w, 0:w])


def _gdn_token_qkv(x, row_prev, row_next, cw_ref, bd_ref, qkv_ref):
    rid = lax.broadcasted_iota(jnp.int32, (TM, 1), 0)
    xprev = jnp.where(rid == 0, row_prev, pltpu.roll(x, 1, 0))
    xnext = jnp.where(rid == TM - 1, row_next, pltpu.roll(x, TM - 1, 0))
    y = xprev * cw_ref[0:1, :] + x * cw_ref[1:2, :] + xnext * cw_ref[2:3, :]
    y = y * _sigmoid(y)
    for part in range(3):
        yp = y[:, part * 256:(part + 1) * 256]
        if part < 2:
            yp = yp * lax.rsqrt(_group_sum(yp * yp, bd_ref) + NORM_EPS)
        if part == 0:
            yp = yp * (HEAD_DIM ** -0.5)
        yp = yp.astype(BF16)
        for hh in range(GDN_HEADS):
            qkv_ref[part * GDN_HEADS + hh] = yp[:, hh * HEAD_DIM:(hh + 1) * HEAD_DIM]


def _gdn_token_gates(ba, alog_ref, dtb_ref, gb_ref, gbt_ref):
    rid = lax.broadcasted_iota(jnp.int32, (TM, 1), 0)
    lane = lax.broadcasted_iota(jnp.int32, (1, LANES), 1)
    beta = _sigmoid(ba)
    sp_in = ba + dtb_ref[...]
    softplus = jnp.maximum(sp_in, 0.0) + jnp.log(1.0 + jnp.exp(-jnp.abs(sp_in)))
    g = jnp.where((lane >= 8) & (lane < 16), -jnp.exp(alog_ref[...]) * softplus, 0.0)
    rc = rid % GDN_CHUNK
    pre = g
    suf = g
    s = 1
    while s < GDN_CHUNK:
        pre = pre + jnp.where(rc >= s, pltpu.roll(pre, s, 0), 0.0)
        suf = suf + jnp.where(rc < GDN_CHUNK - s, pltpu.roll(suf, TM - s, 0), 0.0)
        s *= 2
    tot = pre + suf - g
    gc = jnp.where(lane < 12, pre, suf)
    slab = jnp.where(lane < 8, beta, jnp.where(lane < 16, gc, jnp.where(lane < 24, pltpu.roll(tot, 8, 1), 0.0)))
    gb_ref[...] = slab
    gbt_ref[...] = slab.T[0:24, :]


def _inproj_kernel(x_ref, xp_ref, xn_ref, g_ref, w_ref, gain_ref, rope_blk_ref, rope_off_ref, bd_ref,
                   cw_ref, alog_ref, dtb_ref,
                   oqk_ref, oz_ref, ogate_ref, ovat_ref, ovst_ref, oqkvh_ref, ogb_ref, ogbt_ref, *, nblk):
    i = pl.program_id(0)

    def normed(xv):
        ms = jnp.mean(xv * xv, axis=-1, keepdims=True)
        return xv * lax.rsqrt(ms + NORM_EPS) * g_ref[...]

    h = normed(x_ref[...]).astype(BF16)
    h_halo = jnp.concatenate([jnp.where(i > 0, normed(xp_ref[...]), 0.0),
                              jnp.where(i < nblk - 1, normed(xn_ref[...]), 0.0)], axis=0).astype(BF16)
    cw = 2 * LANES
    lane = lax.broadcasted_iota(jnp.int32, (1, cw), 1)
    first_half = (lane % HEAD_DIM) < (HEAD_DIM // 2)
    c_a, s_a = rope_blk_ref[0:1, :], rope_blk_ref[1:2, :]
    cos128 = c_a * rope_off_ref[0] - s_a * rope_off_ref[1]
    sin128 = s_a * rope_off_ref[2] + c_a * rope_off_ref[3]
    cos = jnp.concatenate([cos128, cos128], axis=1)
    sin = jnp.concatenate([sin128, sin128], axis=1)

    def head_norm(t, c):
        cols = slice(c * cw, min((c + 1) * cw, SEG_QK))
        w = cols.stop - cols.start
        tc = t[:, cols]
        ss = _group_sum(tc * tc, bd_ref)
        y = tc * lax.rsqrt(ss * (1.0 / HEAD_DIM) + NORM_EPS) * gain_ref[:, cols]
        if cols.start >= 512:
            rot = jnp.where(first_half[:, 0:w], pltpu.roll(y, w - HEAD_DIM // 2, 1), pltpu.roll(y, HEAD_DIM // 2, 1))
            y = y * cos[:, 0:w] + rot * sin[:, 0:w]
        oqk_ref[:, cols] = y.astype(BF16)

    o_plain, o_gate, o_ba = SEG_QK, SEG_QK + SEG_PLAIN, SEG_QK + SEG_PLAIN + SEG_GATE
    gate_cols = [slice(o_gate + c * D_MODEL, o_gate + (c + 1) * D_MODEL) for c in range(3)]
    out_cols = [slice(c * D_MODEL, (c + 1) * D_MODEL) for c in range(3)]
    t = _dot(h, w_ref[:, 0:SEG_QK])
    t_gdn = _dot(h, w_ref[:, o_plain:o_plain + 768])
    tba = _dot(h, w_ref[:, o_ba:o_ba + SEG_BA])
    t_halo = _dot(h_halo, w_ref[:, o_plain:o_plain + 768])
    for c in (0, 1):
        head_norm(t, c)
    tg0 = _dot(h, w_ref[:, gate_cols[0]])
    for c in (2, 3, 4):
        head_norm(t, c)
    tg1 = _dot(h, w_ref[:, gate_cols[1]])
    ogate_ref[:, out_cols[0]] = _sigmoid(tg0).astype(BF16)
    _gdn_token_qkv(t_gdn, t_halo[HALO - 1:HALO, :], t_halo[HALO:HALO + 1, :], cw_ref, bd_ref, oqkvh_ref)
    tg2 = _dot(h, w_ref[:, gate_cols[2]])
    ogate_ref[:, out_cols[1]] = _sigmoid(tg1).astype(BF16)
    _gdn_token_gates(tba, alog_ref, dtb_ref, ogb_ref, ogbt_ref)
    t_rest = _dot(h, w_ref[:, o_plain + PLAIN_VA:o_plain + SEG_PLAIN])
    ogate_ref[:, out_cols[2]] = _sigmoid(tg2).astype(BF16)
    oz_ref[...] = t_rest[:, PLAIN_Z - PLAIN_VA:PLAIN_Z - PLAIN_VA + 256].astype(BF16)
    for grp in range(TM // LANES):
        rows = slice(grp * LANES, (grp + 1) * LANES)
        ovat_ref[grp] = t_rest[rows, 0:256].T.astype(BF16)
        ovst_ref[grp] = t_rest[rows, PLAIN_VS - PLAIN_VA:PLAIN_VS - PLAIN_VA + 128].T.astype(BF16)


def _layer_spec(a, l, **kw):
    nd = a.ndim - 1
    return pl.BlockSpec((None,) + a.shape[1:], lambda *_: (l,) + (0,) * nd, **kw)


def _const_spec(a):
    return pl.BlockSpec(a.shape, lambda *_: (0,) * a.ndim)


def _inproj(x, l, g, w, gain, rope_blk, rope_off, bd, conv_w, alog_row, dtb_row):
    L = x.shape[0]
    nblk = L // TM
    hb = TM // HALO
    row = lambda w_: pl.BlockSpec((TM, w_), lambda i: (i, 0))
    return pl.pallas_call(
        functools.partial(_inproj_kernel, nblk=nblk),
        grid=(nblk,),
        in_specs=[row(D_MODEL),
                  pl.BlockSpec((HALO, D_MODEL), lambda i: (jnp.maximum(i * hb - 1, 0), 0)),
                  pl.BlockSpec((HALO, D_MODEL), lambda i: (jnp.minimum((i + 1) * hb, L // HALO - 1), 0)),
                  _layer_spec(g, l), _layer_spec(w, l, pipeline_mode=pl.Buffered(1)),
                  _layer_spec(gain, l), pl.BlockSpec((None, 2, LANES), lambda i: (i, 0, 0)), _const_spec(rope_off),
                  _const_spec(bd), _layer_spec(conv_w, l), _layer_spec(alog_row, l), _layer_spec(dtb_row, l)],
        out_specs=[row(SEG_QK), row(256), row(SEG_GATE),
                   pl.BlockSpec((TM // LANES, 256, LANES), lambda i: (i, 0, 0)),
                   pl.BlockSpec((TM // LANES, 128, LANES), lambda i: (i, 0, 0)),
                   pl.BlockSpec((3 * GDN_HEADS, TM, HEAD_DIM), lambda i: (0, i, 0)),
                   row(LANES),
                   pl.BlockSpec((24, TM), lambda i: (0, i))],
        out_shape=[jax.ShapeDtypeStruct((L, SEG_QK), BF16),
                   jax.ShapeDtypeStruct((L, 256), BF16),
                   jax.ShapeDtypeStruct((L, SEG_GATE), BF16),
                   jax.ShapeDtypeStruct((L // LANES, 256, LANES), BF16),
                   jax.ShapeDtypeStruct((L // LANES, 128, LANES), BF16),
                   jax.ShapeDtypeStruct((3 * GDN_HEADS, L, HEAD_DIM), BF16),
                   jax.ShapeDtypeStruct((L, LANES), F32),
                   jax.ShapeDtypeStruct((24, L), F32)],
        compiler_params=_cparams(("parallel",)),
        name="inproj",
    )(x, x, x, g, w, gain, rope_blk, rope_off, bd, conv_w, alog_row, dtb_row)


def _toeplitz_kernel(rpb_ref, onehot_ref, colmask_ref, o_ref):
    r = rpb_ref[...]
    hi = r.astype(BF16)
    r1 = r - hi.astype(F32)
    mid = r1.astype(BF16)
    lo = (r1 - mid.astype(F32)).astype(BF16)
    oh = onehot_ref[...]
    w = _dot(hi, oh) + _dot(mid, oh) + _dot(lo, oh)
    o_ref[...] = jnp.where(colmask_ref[...] > 0.0, w * LOG2E, NEG_BIG)


def _na_assemble_kernel(tiles_ref, o_ref, *, a_idx):
    neg = jnp.full((GRID_W, GRID_W), NEG_BIG, F32)
    for t in range(a_idx.shape[0]):
        for j in range(NA_WIN):
            for e in range(2):
                a = int(a_idx[t, j, e])
                o_ref[t, j * GRID_W:(j + 1) * GRID_W, e * GRID_W:(e + 1) * GRID_W] = neg if a < 0 else tiles_ref[a]


def _na_bias_tables(na_rpb, rows):
    depth, H, nr, nc = na_rpb.shape
    kc = np.arange(GRID_W)[:, None]
    qc = np.arange(GRID_W)[None, :]
    dc = np.clip(kc - qc + (NA_KW - 1), 0, 2 * NA_KW - 2).reshape(-1)
    onehot = np.zeros((32, GRID_W * GRID_W), np.float32)
    onehot[dc, np.arange(GRID_W * GRID_W)] = 1.0
    col_start = np.clip(qc - NA_KW // 2, 0, GRID_W - NA_KW)
    colmask = ((kc >= col_start) & (kc < col_start + NA_KW)).astype(np.float32).reshape(1, -1)
    rpb_rows = jnp.pad(na_rpb, ((0, 0), (0, 0), (0, 16 - nr), (0, 32 - nc))).reshape(depth * H * 16, 32)
    tiles = pl.pallas_call(
        _toeplitz_kernel,
        out_shape=jax.ShapeDtypeStruct((depth * H * 16, GRID_W * GRID_W), F32),
        name="na_bias_tiles",
    )(rpb_rows, jnp.asarray(onehot, BF16), jnp.asarray(colmask))
    tiles = tiles.reshape(depth * H, 16, GRID_W, GRID_W)
    a_idx = np.stack([_na_pair_structure(r0, rows)[1] for r0 in _na_pair_type_rows(rows)])
    nt = a_idx.shape[0]
    tabs = pl.pallas_call(
        functools.partial(_na_assemble_kernel, a_idx=a_idx),
        grid=(depth * H,),
        in_specs=[pl.BlockSpec((None, 16, GRID_W, GRID_W), lambda i: (i, 0, 0, 0))],
        out_specs=pl.BlockSpec((None, nt, NA_WIN * GRID_W, 2 * GRID_W), lambda i: (i, 0, 0, 0)),
        out_shape=jax.ShapeDtypeStruct((depth * H, nt, NA_WIN * GRID_W, 2 * GRID_W), F32),
        compiler_params=_cparams(("parallel",)),
        name="na_bias_tables",
    )(tiles)
    return tabs.reshape(depth, H, nt, NA_WIN * GRID_W, 2 * GRID_W)


def _na_pair_structure(r0, rows):
    wstart = int(np.clip(r0 - NA_KH // 2, 0, rows - NA_KH - 1)) // 2 * 2
    a_idx = np.full((NA_WIN, 2), -1, np.int64)
    for e in range(2):
        rr = r0 + e
        rs = int(np.clip(rr - NA_KH // 2, 0, rows - NA_KH))
        for j in range(NA_WIN):
            krow = wstart + j
            if rs <= krow < rs + NA_KH:
                a_idx[j, e] = krow - rr + NA_KH - 1
    return wstart, a_idx


def _na_pair_type_rows(rows):
    reps = [0, 2, 4, rows - 4, rows - 2]
    for r0 in range(0, rows, 2):
        t = 3 + (r0 - (rows - 4)) // 2 if r0 >= rows - 4 else min(r0 // 2, 2)
        ws, a = _na_pair_structure(r0, rows)
        ws_t, a_t = _na_pair_structure(reps[t], rows)
        assert (a == a_t).all() and r0 - ws == reps[t] - ws_t
    return reps


def _na_kernel(q_ref, kp_ref, kc_ref, kn_ref, vp_ref, vc_ref, vn_ref, bias_ref, o_ref, kbuf, vbuf, *, rows):
    i = pl.program_id(0)
    blk = NA_ROWS * GRID_W
    grp = blk // LANES
    kbuf[0:blk, :] = kp_ref[...]
    kbuf[blk:2 * blk, :] = kc_ref[...]
    kbuf[2 * blk:3 * blk, :] = kn_ref[...]
    vbuf[0:grp] = vp_ref[...]
    vbuf[grp:2 * grp] = vc_ref[...]
    vbuf[2 * grp:3 * grp] = vn_ref[...]
    win = NA_WIN * GRID_W
    wgrp = win // LANES
    pair = 2 * GRID_W

    npair = NA_ROWS // 2
    goffs, types = [], []
    for pp in range(npair):
        r0 = i * NA_ROWS + 2 * pp
        wstart = jnp.clip(r0 - NA_KH // 2, 0, rows - NA_KH - 1) // 2 * 2
        types.append(jnp.where(r0 >= rows - 4, 3 + (r0 - (rows - 4)) // 2, jnp.minimum(r0 // 2, 2)))
        goffs.append((wstart - (i - 1) * NA_ROWS) // 2)

    def logits(pp, h):
        sl = slice(h * HEAD_DIM, (h + 1) * HEAD_DIM)
        kw = kbuf[pl.ds(pl.multiple_of(goffs[pp] * LANES, LANES), win), sl]
        s = lax.dot_general(kw, q_ref[pp * pair:(pp + 1) * pair, sl],
                            (((1,), (1,)), ((), ())), preferred_element_type=F32)
        return s + bias_ref[h, types[pp]]

    def softmax(s):
        m = jnp.max(s, axis=0, keepdims=True)
        p = jnp.exp2(s - m)
        return p.astype(BF16), 1.0 / jnp.sum(p, axis=0, keepdims=True)

    def weighted(p_inv, pp, h):
        p, inv_l = p_inv
        vt = jnp.concatenate([vbuf[goffs[pp] + j, h * HEAD_DIM:(h + 1) * HEAD_DIM, :] for j in range(wgrp)], axis=-1)
        return _dot(vt, p) * inv_l

    tiles = [(pp, h) for pp in range(npair) for h in range(NA_HEADS)]
    s_q, p_q, outs = {}, {}, {}
    for n in range(len(tiles) + 2):
        if n < len(tiles):
            s_q[n] = logits(*tiles[n])
        if 1 <= n <= len(tiles):
            p_q[n - 1] = softmax(s_q.pop(n - 1))
        if n >= 2:
            pp, h = tiles[n - 2]
            outs[(pp, h)] = weighted(p_q.pop(n - 2), pp, h)
            if h == NA_HEADS - 1:
                o_t = jnp.concatenate([outs.pop((pp, hh)) for hh in range(NA_HEADS)], axis=0)
                o_ref[pp * pair:(pp + 1) * pair, :] = o_t.T.astype(BF16)


def _na_attention(qk, va_t, l, bias):
    L = qk.shape[0]
    rows = L // GRID_W
    nblk = rows // NA_ROWS
    blk = NA_ROWS * GRID_W
    grp = blk // LANES
    w = NA_HEADS * HEAD_DIM
    prev = lambda i: jnp.maximum(i - 1, 0)
    nxt = lambda i: jnp.minimum(i + 1, nblk - 1)
    spec = lambda f, c: pl.BlockSpec((blk, w), lambda i: (f(i), c))
    vspec = lambda f: pl.BlockSpec((grp, w, LANES), lambda i: (f(i), 0, 0))
    same = lambda i: i
    return pl.pallas_call(
        functools.partial(_na_kernel, rows=rows),
        grid=(nblk,),
        in_specs=[spec(same, 0),
                  spec(prev, 1), spec(same, 1), spec(nxt, 1),
                  vspec(prev), vspec(same), vspec(nxt),
                  _layer_spec(bias, l)],
        out_specs=pl.BlockSpec((blk, w), lambda i: (i, 0)),
        out_shape=jax.ShapeDtypeStruct((L, w), BF16),
        scratch_shapes=[pltpu.VMEM((3 * blk, w), BF16), pltpu.VMEM((3 * grp, w, LANES), BF16)],
        compiler_params=_cparams(("parallel",)),
        name="na_attn",
    )(qk, qk, qk, qk, va_t, va_t, va_t, bias)


def _swa_kernel(q_ref, kp_ref, kc_ref, kn_ref, vp_ref, vc_ref, vn_ref, band_ref, sink_ref, o_ref, *, nstep):
    i = pl.program_id(0)
    B = SWA_BLOCK
    G = SWA_Q_HEADS // SWA_KV_HEADS
    k_all = jnp.concatenate([kp_ref[...], kc_ref[...], kn_ref[...]], axis=0)
    vt_all = [vp_ref[0]] + [vc_ref[j] for j in range(SWA_QB)] + [vn_ref[0]]

    def band_index(b):
        if b == 0:
            return jnp.where(i == 0, 1, 0)
        if b == SWA_QB - 1:
            return jnp.where(i == nstep - 1, 2, 0)
        return 0

    def logits(b, g):
        ks = k_all[b * B:(b + 3) * B, g * HEAD_DIM:(g + 1) * HEAD_DIM]
        qs = jnp.concatenate([q_ref[b * B:(b + 1) * B, (g * G + hh) * HEAD_DIM:(g * G + hh + 1) * HEAD_DIM]
                              for hh in range(G)], axis=0)
        return lax.dot_general(ks, qs, (((1,), (1,)), ((), ())), preferred_element_type=F32) + band_ref[band_index(b)]

    def softmax(s, g):
        sink = sink_ref[g]
        m = jnp.maximum(jnp.max(s, axis=0, keepdims=True), sink)
        p = jnp.exp2(s - m)
        l = jnp.sum(p, axis=0, keepdims=True) + jnp.exp2(sink - m)
        return p.astype(BF16), 1.0 / l

    def weighted(p_inv, b, g):
        p, inv_l = p_inv
        vt = jnp.concatenate([vt_all[b + j][g * HEAD_DIM:(g + 1) * HEAD_DIM, :] for j in range(3)], axis=-1)
        return _dot(vt, p) * inv_l

    tiles = [(b, g) for b in range(SWA_QB) for g in range(SWA_KV_HEADS)]
    s_q, p_q = {}, {}
    for n in range(len(tiles) + 2):
        if n < len(tiles):
            s_q[n] = logits(*tiles[n])
        if 1 <= n <= len(tiles):
            p_q[n - 1] = softmax(s_q.pop(n - 1), tiles[n - 1][1])
        if n >= 2:
            b, g = tiles[n - 2]
            o_t = weighted(p_q.pop(n - 2), b, g)
            for hh in range(0, G, 2):
                pair_t = jnp.concatenate([o_t[:, hh * B:(hh + 1) * B], o_t[:, (hh + 1) * B:(hh + 2) * B]], axis=0)
                h = g * G + hh
                o_ref[b * B:(b + 1) * B, h * HEAD_DIM:(h + 2) * HEAD_DIM] = pair_t.T.astype(BF16)


def _swa_band_tables():
    B = SWA_BLOCK
    G = SWA_Q_HEADS // SWA_KV_HEADS
    qi = np.arange(G * B)[None, :] % B
    kj = np.arange(3 * B)[:, None]
    band = np.abs(kj - B - qi) <= B
    tabs = [band, band & (kj >= B), band & (kj < 2 * B)]
    return jnp.asarray(np.where(np.stack(tabs), 0.0, NEG_BIG), F32)


def _swa_attention(qk, vs_t, l, band, sink_row):
    L = qk.shape[0]
    B = SWA_BLOCK
    T = SWA_QB * B
    nstep = L // T
    nblk = L // B
    prev = lambda i: jnp.maximum(i * SWA_QB - 1, 0)
    nxt = lambda i: jnp.minimum((i + 1) * SWA_QB, nblk - 1)
    return pl.pallas_call(
        functools.partial(_swa_kernel, nstep=nstep),
        grid=(nstep,),
        in_specs=[pl.BlockSpec((T, 512), lambda i: (i, 1)),
                  pl.BlockSpec((B, LANES), lambda i: (prev(i), 8)),
                  pl.BlockSpec((T, LANES), lambda i: (i, 8)),
                  pl.BlockSpec((B, LANES), lambda i: (nxt(i), 8)),
                  pl.BlockSpec((1, LANES, B), lambda i: (prev(i), 0, 0)),
                  pl.BlockSpec((SWA_QB, LANES, B), lambda i: (i, 0, 0)),
                  pl.BlockSpec((1, LANES, B), lambda i: (nxt(i), 0, 0)),
                  _const_spec(band), _layer_spec(sink_row, l)],
        out_specs=pl.BlockSpec((T, 512), lambda i: (i, 0)),
        out_shape=jax.ShapeDtypeStruct((L, 512), BF16),
        compiler_params=_cparams(("parallel",)),
        name="swa_attn",
    )(qk, qk, qk, qk, vs_t, vs_t, vs_t, band, sink_row)


def _bmm(a, b):
    return jnp.einsum('bij,bjk->bik', a.astype(BF16), b.astype(BF16), preferred_element_type=F32)


def _bmm_nt(a, b):
    return jnp.einsum('bid,bjd->bij', a.astype(BF16), b.astype(BF16), preferred_element_type=F32)


def _gdn_chunk_terms(qkv, gb, gbt, rev, out):
    G, C, H = GDN_G, GDN_CHUNK, GDN_HEADS

    def heads(part):
        return qkv[part * H:(part + 1) * H].reshape(H * G, C, HEAD_DIM)

    q = heads(0)
    k = heads(1)
    v = heads(2).astype(F32)
    kf = k.astype(F32)
    qf = q.astype(F32)
    d0 = H if rev else 0

    def colv(base):
        return jnp.concatenate([gb[:, base + h:base + h + 1].reshape(G, C, 1) for h in range(H)], axis=0)

    beta = jnp.broadcast_to(colv(d0), (H * G, C, HEAD_DIM))
    gcc = jnp.broadcast_to(colv(8 + d0), (H * G, C, HEAD_DIM))
    gl = jnp.broadcast_to(colv(16 + d0), (H * G, C, HEAD_DIM))
    grow = jnp.concatenate([gbt[8 + d0 + h:9 + d0 + h, c * C:(c + 1) * C].reshape(1, 1, C)
                            for h in range(H) for c in range(G)], axis=0)
    ii = lax.broadcasted_iota(jnp.int32, (1, C, C), 1)
    jj = lax.broadcasted_iota(jnp.int32, (1, C, C), 2)
    incl = (jj >= ii) if rev else (jj <= ii)
    strict = (jj > ii) if rev else (jj < ii)
    decay = jnp.exp(jnp.where(incl, gcc - grow, NEG_BIG))
    kk = _bmm_nt(k, k)
    qk = _bmm_nt(q, k)
    yield
    nmat = jnp.where(strict, kk * decay, 0.0) * beta
    eg = jnp.exp(gcc)
    rhs = jnp.concatenate([v * beta, kf * (beta * eg)], axis=-1)
    m = _bmm(nmat, jnp.concatenate([rhs, nmat], axis=-1))
    yield
    x = rhs - m[:, :, 0:2 * HEAD_DIM]
    p = m[:, :, 2 * HEAD_DIM:]
    for _ in range(4):
        m = _bmm(p, jnp.concatenate([x, p], axis=-1))
        yield
        x = x + m[:, :, 0:2 * HEAD_DIM]
        p = m[:, :, 2 * HEAD_DIM:]
    x = x + _bmm(p, x)
    yield
    u = x[:, :, 0:HEAD_DIM]
    w = x[:, :, HEAD_DIM:2 * HEAD_DIM]
    qkm = jnp.where(incl, qk * decay, 0.0)
    k_tail_t = jnp.swapaxes(kf * jnp.exp(gl - gcc), 1, 2)
    lhs1 = jnp.concatenate([w, qf * eg], axis=1).astype(BF16)
    lhs2 = jnp.concatenate([qkm, k_tail_t], axis=1).astype(BF16)
    dch = jnp.exp(gl[:, 0:8, :])
    out.extend([u, lhs1, lhs2, dch])


def _gdn_scan_group(u_ref, l1_ref, l2_ref, d_ref, s_ref, o_ref, rev):
    G, C, H = GDN_G, GDN_CHUNK, GDN_HEADS
    S = s_ref[...]
    for c in (range(G - 1, -1, -1) if rev else range(G)):
        r1 = _bmm(l1_ref[:, c], S)
        yield
        vn = u_ref[:, c] - r1[:, 0:C]
        r2 = _bmm(l2_ref[:, c], vn)
        yield
        oc = r1[:, C:2 * C] + r2[:, 0:C]
        S = S * d_ref[:, c][:, 0:1, :] + r2[:, C:2 * C]
        o_ref[c * C:(c + 1) * C, :] = jnp.concatenate([oc[h] for h in range(H)], axis=-1)
    s_ref[...] = S


def _gdn_kernel(qkvf_ref, gbf_ref, gbtf_ref, qkvb_ref, gbb_ref, gbtb_ref, of_ref, ob_ref,
                sf_ref, sb_ref, uf_ref, l1f_ref, l2f_ref, df_ref, ub_ref, l1b_ref, l2b_ref, db_ref):
    G, C, H = GDN_G, GDN_CHUNK, GDN_HEADS
    step = pl.program_id(0)

    @pl.when(step == 0)
    def _():
        for r in (sf_ref, sb_ref, uf_ref, l1f_ref, l2f_ref, df_ref, ub_ref, l1b_ref, l2b_ref, db_ref):
            r[...] = jnp.zeros_like(r)

    terms_f, terms_b = [], []
    strands = [_gdn_scan_group(uf_ref, l1f_ref, l2f_ref, df_ref, sf_ref, of_ref, False),
               _gdn_scan_group(ub_ref, l1b_ref, l2b_ref, db_ref, sb_ref, ob_ref, True),
               _gdn_chunk_terms(qkvf_ref[...], gbf_ref[...], gbtf_ref[...], False, terms_f),
               _gdn_chunk_terms(qkvb_ref[...], gbb_ref[...], gbtb_ref[...], True, terms_b)]
    while strands:
        strands = [s for s in strands if next(s, True) is None]
    for (u, l1, l2, dch), (u_ref, l1_ref, l2_ref, d_ref) in ((terms_f, (uf_ref, l1f_ref, l2f_ref, df_ref)),
                                                          (terms_b, (ub_ref, l1b_ref, l2b_ref, db_ref))):
        u_ref[...] = u.reshape(H, G, C, HEAD_DIM)
        l1_ref[...] = l1.reshape(H, G, 2 * C, HEAD_DIM)
        l2_ref[...] = l2.reshape(H, G, 2 * C, C)
        d_ref[...] = dch.reshape(H, G, 8, HEAD_DIM)


def _gdn(qkv, gb, gbt):
    L = qkv.shape[1]
    G, C, H = GDN_G, GDN_CHUNK, GDN_HEADS
    T = G * C
    n = L // T
    fin = lambda s: jnp.minimum(s, n - 1)
    bin_ = lambda s: jnp.maximum(n - 1 - s, 0)
    fout = lambda s: jnp.maximum(s - 1, 0)
    bout = lambda s: jnp.minimum(n - s, n - 1)
    term_scratch = [pltpu.VMEM((H, G, C, HEAD_DIM), F32), pltpu.VMEM((H, G, 2 * C, HEAD_DIM), BF16),
                    pltpu.VMEM((H, G, 2 * C, C), BF16), pltpu.VMEM((H, G, 8, HEAD_DIM), F32)]
    return pl.pallas_call(
        _gdn_kernel,
        grid=(n + 1,),
        in_specs=[pl.BlockSpec((3 * H, T, HEAD_DIM), lambda s: (0, fin(s), 0)),
                  pl.BlockSpec((T, LANES), lambda s: (fin(s), 0)),
                  pl.BlockSpec((24, T), lambda s: (0, fin(s))),
                  pl.BlockSpec((3 * H, T, HEAD_DIM), lambda s: (0, bin_(s), 0)),
                  pl.BlockSpec((T, LANES), lambda s: (bin_(s), 0)),
                  pl.BlockSpec((24, T), lambda s: (0, bin_(s)))],
        out_specs=[pl.BlockSpec((T, 256), lambda s: (fout(s), 0)),
                   pl.BlockSpec((T, 256), lambda s: (bout(s), 0))],
        out_shape=[jax.ShapeDtypeStruct((L, 256), F32), jax.ShapeDtypeStruct((L, 256), F32)],
        scratch_shapes=[pltpu.VMEM((H, HEAD_DIM, HEAD_DIM), F32), pltpu.VMEM((H, HEAD_DIM, HEAD_DIM), F32)]
                       + term_scratch + term_scratch,
        compiler_params=_cparams(("arbitrary",)),
        name="gdn_scan",
    )(qkv, gb, gbt, qkv, gb, gbt)


def _merge_kernel(x_ref, yna_ref, yswa_ref, of_ref, ob_ref, z_ref, gate_ref, gn_ref, bd_ref,
                  wna_ref, wswa_ref, wgdn_ref, wout_ref, o_ref):
    o = of_ref[...] + ob_ref[...]
    z = z_ref[...].astype(F32)
    ms = _group_sum(o * o, bd_ref) * (1.0 / HEAD_DIM)
    ygdn = o * lax.rsqrt(ms + NORM_EPS) * gn_ref[...] * (z * _sigmoid(z))
    m = (gate_ref[:, 0:D_MODEL].astype(F32) * _dot(yna_ref[...], wna_ref[...])
         + gate_ref[:, D_MODEL:2 * D_MODEL].astype(F32) * _dot(yswa_ref[...], wswa_ref[...])
         + gate_ref[:, 2 * D_MODEL:3 * D_MODEL].astype(F32) * _dot(ygdn.astype(BF16), wgdn_ref[...]))
    o_ref[...] = x_ref[...] + _dot(m.astype(BF16), wout_ref[...])


def _merge(x, yna, yswa, of, ob, z, gate, l, gn, bd, wna, wswa, wgdn, wout):
    L = x.shape[0]
    row = lambda w_: pl.BlockSpec((TM, w_), lambda i: (i, 0))
    return pl.pallas_call(
        _merge_kernel,
        grid=(L // TM,),
        in_specs=[row(D_MODEL), row(256), row(512), row(256), row(256), row(256), row(SEG_GATE),
                  _layer_spec(gn, l), _const_spec(bd), _layer_spec(wna, l), _layer_spec(wswa, l),
                  _layer_spec(wgdn, l), _layer_spec(wout, l)],
        out_specs=row(D_MODEL),
        out_shape=jax.ShapeDtypeStruct((L, D_MODEL), F32),
        compiler_params=_cparams(("parallel",)),
        name="merge",
    )(x, yna, yswa, of, ob, z, gate, gn, bd, wna, wswa, wgdn, wout)


def _ffn_kernel(x_ref, xp_ref, xn_ref, g_ref, wu_ref, cw_ref, cb_ref, wd_ref, o_ref, hbuf, *, nblk):
    i = pl.program_id(0)

    def normed(xv):
        ms = jnp.mean(xv * xv, axis=-1, keepdims=True)
        return xv * lax.rsqrt(ms + NORM_EPS) * g_ref[...]

    hbuf[0:HALO, :] = jnp.where(i > 0, normed(xp_ref[...]), 0.0).astype(BF16)
    hbuf[HALO:HALO + TM, :] = normed(x_ref[...]).astype(BF16)
    hbuf[HALO + TM:, :] = jnp.where(i < nblk - 1, normed(xn_ref[...]), 0.0).astype(BF16)
    h = hbuf[...]
    n = TM + 2 * HALO
    chunks = [slice(j * FFN_NC, (j + 1) * FFN_NC) for j in range(D_FF // FFN_NC)]
    gate_cols = [slice(D_FF + j * FFN_NC, D_FF + (j + 1) * FFN_NC) for j in range(D_FF // FFN_NC)]

    def conv(u, cols):
        y = (pltpu.roll(u, 1, 0) * cw_ref[0:1, cols] + u * cw_ref[1:2, cols]
             + pltpu.roll(u, n - 1, 0) * cw_ref[2:3, cols])
        return y[HALO:HALO + TM] + cb_ref[:, cols]

    ups = [(_dot(h, wu_ref[:, ca]), _dot(h, wu_ref[:, cb])) for ca, cb in zip(chunks, gate_cols)]
    acc = x_ref[...]
    for (ua, ub), ca, cb in zip(ups, chunks, gate_cols):
        a = conv(ua, ca)
        gated = (a * _sigmoid(a) * conv(ub, cb)).astype(BF16)
        acc = acc + _dot(gated, wd_ref[ca, :])
    o_ref[...] = acc


def _ffn(x, l, g, w_up, conv_w, conv_b, w_down):
    L = x.shape[0]
    nblk = L // TM
    hb = TM // HALO
    once = pl.Buffered(1)
    return pl.pallas_call(
        functools.partial(_ffn_kernel, nblk=nblk),
        grid=(nblk,),
        in_specs=[pl.BlockSpec((TM, D_MODEL), lambda i: (i, 0)),
                  pl.BlockSpec((HALO, D_MODEL), lambda i: (jnp.maximum(i * hb - 1, 0), 0)),
                  pl.BlockSpec((HALO, D_MODEL), lambda i: (jnp.minimum((i + 1) * hb, L // HALO - 1), 0)),
                  _layer_spec(g, l), _layer_spec(w_up, l, pipeline_mode=once), _layer_spec(conv_w, l),
                  _layer_spec(conv_b, l), _layer_spec(w_down, l, pipeline_mode=once)],
        out_specs=pl.BlockSpec((TM, D_MODEL), lambda i: (i, 0)),
        out_shape=jax.ShapeDtypeStruct((L, D_MODEL), F32),
        scratch_shapes=[pltpu.VMEM((TM + 2 * HALO, D_MODEL), BF16)],
        compiler_params=_cparams(("parallel",)),
        name="ffn",
    )(x, x, x, g, w_up, conv_w, conv_b, w_down)


PACK_SEGMENTS = ((0, 256, 0), (256, 512, 256), (768, 1280, 512), (1280, 1408, 1024),
                 (1536, 2304, 1152), (512, 768, 1920), (2304, 2560, 2176), (1408, 1536, 2432),
                 (2576, 5648, 2560), (2560, 2576, 5632))
PACK_K = 256
PACK_PIECE = 512


def _pack_kernel(w_ref, o_ref):
    for s0, s1, d0 in PACK_SEGMENTS[:-1]:
        for p0 in range(s0, s1, PACK_PIECE):
            p1 = min(p0 + PACK_PIECE, s1)
            o_ref[:, d0 + p0 - s0:d0 + p1 - s0] = w_ref[p0:p1, :].T.astype(BF16)
    s0, s1, d0 = PACK_SEGMENTS[-1]
    lane = lax.broadcasted_iota(jnp.int32, (1, LANES), 1)
    o_ref[:, d0:d0 + LANES] = jnp.where(lane < s1 - s0, w_ref[s0:s0 + LANES, :].T, 0.0).astype(BF16)


def _pack_w_in(w):
    depth, k, n = w.shape
    return pl.pallas_call(
        _pack_kernel,
        grid=(depth, k // PACK_K),
        in_specs=[pl.BlockSpec((None, n, PACK_K), lambda l, i: (l, 0, i))],
        out_specs=pl.BlockSpec((None, PACK_K, IN_PACKED), lambda l, i: (l, i, 0)),
        out_shape=jax.ShapeDtypeStruct((depth, k, IN_PACKED), BF16),
        compiler_params=_cparams(("parallel", "parallel")),
        name="pack_w_in",
    )(jnp.swapaxes(w, 1, 2))


def _rope_tables(L):
    inv = 1.0 / (ROPE_THETA ** (np.arange(0, HEAD_DIM, 2, dtype=np.float64) / HEAD_DIM))
    freq = np.tile(inv, LANES // inv.size)
    sign = np.where(np.arange(LANES) % HEAD_DIM < HEAD_DIM // 2, -1.0, 1.0)
    start = np.arange(L // TM, dtype=np.float64)[:, None] * TM * freq[None, :]
    off = np.arange(TM, dtype=np.float64)[:, None] * freq[None, :]
    blk = np.stack([np.cos(start), np.sin(start)], axis=1)
    offs = np.stack([np.cos(off), np.sin(off), sign * np.cos(off), sign * np.sin(off)])
    return jnp.asarray(blk, F32), jnp.asarray(offs, F32)


def kernel(x, attn_norm, w_in, qk_norm, na_rpb, swa_sink, gdn_conv_w, gdn_a_log, gdn_dt_bias, gdn_norm,
           w_branch_na, w_branch_swa, w_branch_gdn, w_out, ffn_norm, w_up, ffn_conv_w, ffn_conv_b, w_down):
    B, L, D = x.shape
    assert B == 1 and D == D_MODEL and L % (NA_ROWS * GRID_W) == 0 and L // GRID_W >= 2 * NA_ROWS
    depth = w_in.shape[0]
    rope_blk, rope_off = _rope_tables(L)
    blockdiag = jnp.asarray(np.kron(np.eye(4), np.ones((HEAD_DIM, HEAD_DIM))), BF16)
    bias_tabs = _na_bias_tables(na_rpb, L // GRID_W)
    swa_band = _swa_band_tables()
    scale = HEAD_DIM ** -0.5 * LOG2E
    G = SWA_Q_HEADS // SWA_KV_HEADS
    gain = jnp.concatenate([jnp.tile(qk_norm[:, 0] * scale, (1, NA_HEADS)), jnp.tile(qk_norm[:, 1], (1, NA_HEADS)),
                            jnp.tile(qk_norm[:, 2] * scale, (1, SWA_Q_HEADS)),
                            jnp.tile(qk_norm[:, 3], (1, SWA_KV_HEADS))], axis=1)[:, None, :]
    sink_row = jnp.repeat(swa_sink.reshape(depth, SWA_KV_HEADS, G) * LOG2E, SWA_BLOCK, axis=2)[:, :, None, :]
    alog_row = jnp.pad(gdn_a_log.reshape(depth, 1, 8), ((0, 0), (0, 0), (8, LANES - 16)))
    dtb_row = jnp.pad(gdn_dt_bias.reshape(depth, 1, 8), ((0, 0), (0, 0), (8, LANES - 16)))
    gdn_gain = jnp.tile(gdn_norm, (1, GDN_HEADS))[:, None, :]
    attn_g = attn_norm[:, None, :]
    ffn_g = ffn_norm[:, None, :]
    ffn_b = ffn_conv_b[:, None, :]
    w_in_p = _pack_w_in(w_in)
    w_na, w_swa, w_gdn = w_branch_na.astype(BF16), w_branch_swa.astype(BF16), w_branch_gdn.astype(BF16)
    w_o, w_u, w_d = w_out.astype(BF16), w_up.astype(BF16), w_down.astype(BF16)
    xs = x[0]
    for l in range(depth):
        qk, z, gate, va_t, vs_t, qkv_n, gb, gbt = _inproj(xs, l, attn_g, w_in_p, gain, rope_blk, rope_off, blockdiag,
                                                          gdn_conv_w, alog_row, dtb_row)
        y_na = _na_attention(qk, va_t, l, bias_tabs)
        y_swa = _swa_attention(qk, vs_t, l, swa_band, sink_row)
        o_f, o_b = _gdn(qkv_n, gb, gbt)
        xs = _merge(xs, y_na, y_swa, o_f, o_b, z, gate, l, gdn_gain, blockdiag, w_na, w_swa, w_gdn, w_o)
        xs = _ffn(xs, l, ffn_g, w_u, ffn_conv_w, ffn_b, w_d)
    return xs[None]
```

```python
import functools

import numpy as np
import jax
import jax.numpy as jnp
from jax import lax
from jax.experimental import pallas as pl
from jax.experimental.pallas import tpu as pltpu

F32 = jnp.float32
BF16 = jnp.bfloat16

D_MODEL = 1024
HEAD_DIM = 64
GRID_W = 64
NORM_EPS = 1e-6
NA_HEADS = 4
NA_KH = 8
NA_KW = 16
SWA_Q_HEADS = 8
SWA_KV_HEADS = 2
SWA_BLOCK = 128
ROPE_THETA = 10000.0
GDN_HEADS = 4
GDN_CHUNK = 64
D_FF = 2816

LANES = 128
NEG_BIG = -1e30
LOG2E = 1.4426950408889634
VMEM_LIMIT = 56 * 1024 * 1024

SEG_QK = 1152
SEG_PLAIN = 1408
PLAIN_VA = 768
PLAIN_Z = 1024
PLAIN_VS = 1280
SEG_GATE = 3072
SEG_BA = 128
IN_PACKED = SEG_QK + SEG_PLAIN + SEG_GATE + SEG_BA

TM = 512
NA_ROWS = 16
NA_WIN = NA_KH + 2
SWA_QB = 8
GDN_G = 4
MXU_TILE = 256
FFN_SPLIT = (0, 6 * MXU_TILE, D_FF)
HALO = 8


def _cparams(sem):
    return pltpu.CompilerParams(dimension_semantics=sem, vmem_limit_bytes=VMEM_LIMIT)


def _dot(a, b):
    return jnp.dot(a, b, preferred_element_type=F32)


def _sigmoid(x):
    return 1.0 / (1.0 + jnp.exp(-x))


def _group_sum(sq, bd_ref):
    w = sq.shape[-1]
    return _dot(sq.astype(BF16), bd_ref[0:w, 0:w])


def _gdn_token_qkv(x, row_prev, row_next, cw_ref, bd_ref, qkv_ref):
    rid = lax.broadcasted_iota(jnp.int32, (TM, 1), 0)
    xprev = jnp.where(rid == 0, row_prev, pltpu.roll(x, 1, 0))
    xnext = jnp.where(rid == TM - 1, row_next, pltpu.roll(x, TM - 1, 0))
    y = xprev * cw_ref[0:1, :] + x * cw_ref[1:2, :] + xnext * cw_ref[2:3, :]
    y = y * _sigmoid(y)
    for part in range(3):
        yp = y[:, part * 256:(part + 1) * 256]
        if part < 2:
            yp = yp * lax.rsqrt(_group_sum(yp * yp, bd_ref) + NORM_EPS)
        if part == 0:
            yp = yp * (HEAD_DIM ** -0.5)
        yp = yp.astype(BF16)
        for hh in range(GDN_HEADS):
            qkv_ref[part * GDN_HEADS + hh] = yp[:, hh * HEAD_DIM:(hh + 1) * HEAD_DIM]


def _gdn_token_gates(ba, alog_ref, dtb_ref, gb_ref, gbt_ref):
    rid = lax.broadcasted_iota(jnp.int32, (TM, 1), 0)
    lane = lax.broadcasted_iota(jnp.int32, (1, LANES), 1)
    beta = _sigmoid(ba)
    sp_in = ba + dtb_ref[...]
    softplus = jnp.maximum(sp_in, 0.0) + jnp.log(1.0 + jnp.exp(-jnp.abs(sp_in)))
    g = jnp.where((lane >= 8) & (lane < 16), -jnp.exp(alog_ref[...]) * softplus, 0.0)
    rc = rid % GDN_CHUNK
    pre = g
    suf = g
    s = 1
    while s < GDN_CHUNK:
        pre = pre + jnp.where(rc >= s, pltpu.roll(pre, s, 0), 0.0)
        suf = suf + jnp.where(rc < GDN_CHUNK - s, pltpu.roll(suf, TM - s, 0), 0.0)
        s *= 2
    tot = pre + suf - g
    gc = jnp.where(lane < 12, pre, suf)
    slab = jnp.where(lane < 8, beta, jnp.where(lane < 16, gc, jnp.where(lane < 24, pltpu.roll(tot, 8, 1), 0.0)))
    gb_ref[...] = slab
    gbt_ref[...] = slab.T[0:24, :]


def _inproj_kernel(x_ref, xp_ref, xn_ref, g_ref, w_ref, gain_ref, rope_blk_ref, rope_off_ref, bd_ref,
                   cw_ref, alog_ref, dtb_ref,
                   oqk_ref, oz_ref, ogate_ref, ovat_ref, ovst_ref, oqkvh_ref, ogb_ref, ogbt_ref, *, nblk):
    i = pl.program_id(0)

    def normed(xv):
        ms = jnp.mean(xv * xv, axis=-1, keepdims=True)
        return xv * lax.rsqrt(ms + NORM_EPS) * g_ref[...]

    h = normed(x_ref[...]).astype(BF16)
    h_halo = jnp.concatenate([jnp.where(i > 0, normed(xp_ref[...]), 0.0),
                              jnp.where(i < nblk - 1, normed(xn_ref[...]), 0.0)], axis=0).astype(BF16)
    cw = 2 * LANES
    lane = lax.broadcasted_iota(jnp.int32, (1, cw), 1)
    first_half = (lane % HEAD_DIM) < (HEAD_DIM // 2)
    c_a, s_a = rope_blk_ref[0:1, :], rope_blk_ref[1:2, :]
    cos128 = c_a * rope_off_ref[0] - s_a * rope_off_ref[1]
    sin128 = s_a * rope_off_ref[2] + c_a * rope_off_ref[3]
    cos = jnp.concatenate([cos128, cos128], axis=1)
    sin = jnp.concatenate([sin128, sin128], axis=1)

    def head_norm(t, c):
        cols = slice(c * cw, min((c + 1) * cw, SEG_QK))
        w = cols.stop - cols.start
        tc = t[:, cols]
        ss = _group_sum(tc * tc, bd_ref)
        y = tc * lax.rsqrt(ss * (1.0 / HEAD_DIM) + NORM_EPS) * gain_ref[:, cols]
        if cols.start >= 512:
            rot = jnp.where(first_half[:, 0:w], pltpu.roll(y, w - HEAD_DIM // 2, 1), pltpu.roll(y, HEAD_DIM // 2, 1))
            y = y * cos[:, 0:w] + rot * sin[:, 0:w]
        oqk_ref[:, cols] = y.astype(BF16)

    o_plain, o_gate, o_ba = SEG_QK, SEG_QK + SEG_PLAIN, SEG_QK + SEG_PLAIN + SEG_GATE
    gate_cols = [slice(o_gate + c * D_MODEL, o_gate + (c + 1) * D_MODEL) for c in range(3)]
    out_cols = [slice(c * D_MODEL, (c + 1) * D_MODEL) for c in range(3)]
    t = _dot(h, w_ref[:, 0:SEG_QK])
    t_gdn = _dot(h, w_ref[:, o_plain:o_plain + 768])
    tba = _dot(h, w_ref[:, o_ba:o_ba + SEG_BA])
    t_halo = _dot(h_halo, w_ref[:, o_plain:o_plain + 768])
    for c in (0, 1):
        head_norm(t, c)
    tg0 = _dot(h, w_ref[:, gate_cols[0]])
    for c in (2, 3, 4):
        head_norm(t, c)
    tg1 = _dot(h, w_ref[:, gate_cols[1]])
    ogate_ref[:, out_cols[0]] = _sigmoid(tg0).astype(BF16)
    _gdn_token_qkv(t_gdn, t_halo[HALO - 1:HALO, :], t_halo[HALO:HALO + 1, :], cw_ref, bd_ref, oqkvh_ref)
    tg2 = _dot(h, w_ref[:, gate_cols[2]])
    ogate_ref[:, out_cols[1]] = _sigmoid(tg1).astype(BF16)
    _gdn_token_gates(tba, alog_ref, dtb_ref, ogb_ref, ogbt_ref)
    t_rest = _dot(h, w_ref[:, o_plain + PLAIN_VA:o_plain + SEG_PLAIN])
    ogate_ref[:, out_cols[2]] = _sigmoid(tg2).astype(BF16)
    oz_ref[...] = t_rest[:, PLAIN_Z - PLAIN_VA:PLAIN_Z - PLAIN_VA + 256].astype(BF16)
    for grp in range(TM // LANES):
        rows = slice(grp * LANES, (grp + 1) * LANES)
        ovat_ref[grp] = t_rest[rows, 0:256].T.astype(BF16)
        ovst_ref[grp] = t_rest[rows, PLAIN_VS - PLAIN_VA:PLAIN_VS - PLAIN_VA + 128].T.astype(BF16)


def _layer_spec(a, l, **kw):
    nd = a.ndim - 1
    return pl.BlockSpec((None,) + a.shape[1:], lambda *_: (l,) + (0,) * nd, **kw)


def _const_spec(a):
    return pl.BlockSpec(a.shape, lambda *_: (0,) * a.ndim)


def _inproj(x, l, g, w, gain, rope_blk, rope_off, bd, conv_w, alog_row, dtb_row):
    L = x.shape[0]
    nblk = L // TM
    hb = TM // HALO
    row = lambda w_: pl.BlockSpec((TM, w_), lambda i: (i, 0))
    return pl.pallas_call(
        functools.partial(_inproj_kernel, nblk=nblk),
        grid=(nblk,),
        in_specs=[row(D_MODEL),
                  pl.BlockSpec((HALO, D_MODEL), lambda i: (jnp.maximum(i * hb - 1, 0), 0)),
                  pl.BlockSpec((HALO, D_MODEL), lambda i: (jnp.minimum((i + 1) * hb, L // HALO - 1), 0)),
                  _layer_spec(g, l), _layer_spec(w, l, pipeline_mode=pl.Buffered(1)),
                  _layer_spec(gain, l), pl.BlockSpec((None, 2, LANES), lambda i: (i, 0, 0)), _const_spec(rope_off),
                  _const_spec(bd), _layer_spec(conv_w, l), _layer_spec(alog_row, l), _layer_spec(dtb_row, l)],
        out_specs=[row(SEG_QK), row(256), row(SEG_GATE),
                   pl.BlockSpec((TM // LANES, 256, LANES), lambda i: (i, 0, 0)),
                   pl.BlockSpec((TM // LANES, 128, LANES), lambda i: (i, 0, 0)),
                   pl.BlockSpec((3 * GDN_HEADS, TM, HEAD_DIM), lambda i: (0, i, 0)),
                   row(LANES),
                   pl.BlockSpec((24, TM), lambda i: (0, i))],
        out_shape=[jax.ShapeDtypeStruct((L, SEG_QK), BF16),
                   jax.ShapeDtypeStruct((L, 256), BF16),
                   jax.ShapeDtypeStruct((L, SEG_GATE), BF16),
                   jax.ShapeDtypeStruct((L // LANES, 256, LANES), BF16),
                   jax.ShapeDtypeStruct((L // LANES, 128, LANES), BF16),
                   jax.ShapeDtypeStruct((3 * GDN_HEADS, L, HEAD_DIM), BF16),
                   jax.ShapeDtypeStruct((L, LANES), F32),
                   jax.ShapeDtypeStruct((24, L), F32)],
        compiler_params=_cparams(("parallel",)),
        name="inproj",
    )(x, x, x, g, w, gain, rope_blk, rope_off, bd, conv_w, alog_row, dtb_row)


def _toeplitz_kernel(rpb_ref, onehot_ref, colmask_ref, o_ref):
    r = rpb_ref[...]
    hi = r.astype(BF16)
    r1 = r - hi.astype(F32)
    mid = r1.astype(BF16)
    lo = (r1 - mid.astype(F32)).astype(BF16)
    oh = onehot_ref[...]
    w = _dot(hi, oh) + _dot(mid, oh) + _dot(lo, oh)
    o_ref[...] = jnp.where(colmask_ref[...] > 0.0, w * LOG2E, NEG_BIG)


def _na_assemble_kernel(tiles_ref, o_ref, *, a_idx):
    neg = jnp.full((GRID_W, GRID_W), NEG_BIG, F32)
    for t in range(a_idx.shape[0]):
        for j in range(NA_WIN):
            for e in range(2):
                a = int(a_idx[t, j, e])
                o_ref[t, j * GRID_W:(j + 1) * GRID_W, e * GRID_W:(e + 1) * GRID_W] = neg if a < 0 else tiles_ref[a]


def _na_bias_tables(na_rpb, rows):
    depth, H, nr, nc = na_rpb.shape
    kc = np.arange(GRID_W)[:, None]
    qc = np.arange(GRID_W)[None, :]
    dc = np.clip(kc - qc + (NA_KW - 1), 0, 2 * NA_KW - 2).reshape(-1)
    onehot = np.zeros((32, GRID_W * GRID_W), np.float32)
    onehot[dc, np.arange(GRID_W * GRID_W)] = 1.0
    col_start = np.clip(qc - NA_KW // 2, 0, GRID_W - NA_KW)
    colmask = ((kc >= col_start) & (kc < col_start + NA_KW)).astype(np.float32).reshape(1, -1)
    rpb_rows = jnp.pad(na_rpb, ((0, 0), (0, 0), (0, 16 - nr), (0, 32 - nc))).reshape(depth * H * 16, 32)
    tiles = pl.pallas_call(
        _toeplitz_kernel,
        out_shape=jax.ShapeDtypeStruct((depth * H * 16, GRID_W * GRID_W), F32),
        name="na_bias_tiles",
    )(rpb_rows, jnp.asarray(onehot, BF16), jnp.asarray(colmask))
    tiles = tiles.reshape(depth * H, 16, GRID_W, GRID_W)
    a_idx = np.stack([_na_pair_structure(r0, rows)[1] for r0 in _na_pair_type_rows(rows)])
    nt = a_idx.shape[0]
    tabs = pl.pallas_call(
        functools.partial(_na_assemble_kernel, a_idx=a_idx),
        grid=(depth * H,),
        in_specs=[pl.BlockSpec((None, 16, GRID_W, GRID_W), lambda i: (i, 0, 0, 0))],
        out_specs=pl.BlockSpec((None, nt, NA_WIN * GRID_W, 2 * GRID_W), lambda i: (i, 0, 0, 0)),
        out_shape=jax.ShapeDtypeStruct((depth * H, nt, NA_WIN * GRID_W, 2 * GRID_W), F32),
        compiler_params=_cparams(("parallel",)),
        name="na_bias_tables",
    )(tiles)
    return tabs.reshape(depth, H, nt, NA_WIN * GRID_W, 2 * GRID_W)


def _na_pair_structure(r0, rows):
    wstart = int(np.clip(r0 - NA_KH // 2, 0, rows - NA_KH - 1)) // 2 * 2
    a_idx = np.full((NA_WIN, 2), -1, np.int64)
    for e in range(2):
        rr = r0 + e
        rs = int(np.clip(rr - NA_KH // 2, 0, rows - NA_KH))
        for j in range(NA_WIN):
            krow = wstart + j
            if rs <= krow < rs + NA_KH:
                a_idx[j, e] = krow - rr + NA_KH - 1
    return wstart, a_idx


def _na_pair_type_rows(rows):
    reps = [0, 2, 4, rows - 4, rows - 2]
    for r0 in range(0, rows, 2):
        t = 3 + (r0 - (rows - 4)) // 2 if r0 >= rows - 4 else min(r0 // 2, 2)
        ws, a = _na_pair_structure(r0, rows)
        ws_t, a_t = _na_pair_structure(reps[t], rows)
        assert (a == a_t).all() and r0 - ws == reps[t] - ws_t
    return reps


def _na_kernel(q_ref, kp_ref, kc_ref, kn_ref, vp_ref, vc_ref, vn_ref, bias_ref, o_ref, kbuf, vbuf, *, rows):
    i = pl.program_id(0)
    blk = NA_ROWS * GRID_W
    grp = blk // LANES
    kbuf[0:blk, :] = kp_ref[...]
    kbuf[blk:2 * blk, :] = kc_ref[...]
    kbuf[2 * blk:3 * blk, :] = kn_ref[...]
    vbuf[0:grp] = vp_ref[...]
    vbuf[grp:2 * grp] = vc_ref[...]
    vbuf[2 * grp:3 * grp] = vn_ref[...]
    win = NA_WIN * GRID_W
    wgrp = win // LANES
    pair = 2 * GRID_W

    npair = NA_ROWS // 2
    goffs, types = [], []
    for pp in range(npair):
        r0 = i * NA_ROWS + 2 * pp
        wstart = jnp.clip(r0 - NA_KH // 2, 0, rows - NA_KH - 1) // 2 * 2
        types.append(jnp.where(r0 >= rows - 4, 3 + (r0 - (rows - 4)) // 2, jnp.minimum(r0 // 2, 2)))
        goffs.append((wstart - (i - 1) * NA_ROWS) // 2)

    def logits(pp, h):
        sl = slice(h * HEAD_DIM, (h + 1) * HEAD_DIM)
        kw = kbuf[pl.ds(pl.multiple_of(goffs[pp] * LANES, LANES), win), sl]
        s = lax.dot_general(kw, q_ref[pp * pair:(pp + 1) * pair, sl],
                            (((1,), (1,)), ((), ())), preferred_element_type=F32)
        return s + bias_ref[h, types[pp]]

    def softmax(s):
        m = jnp.max(s, axis=0, keepdims=True)
        p = jnp.exp2(s - m)
        return p.astype(BF16), 1.0 / jnp.sum(p, axis=0, keepdims=True)

    def weighted(p_inv, pp, h):
        p, inv_l = p_inv
        vt = jnp.concatenate([vbuf[goffs[pp] + j, h * HEAD_DIM:(h + 1) * HEAD_DIM, :] for j in range(wgrp)], axis=-1)
        return _dot(vt, p) * inv_l

    tiles = [(pp, h) for pp in range(npair) for h in range(NA_HEADS)]
    s_q, p_q, outs = {}, {}, {}
    lead, lag = 2, 1
    for n in range(len(tiles) + lead + lag):
        if n < len(tiles):
            s_q[n] = logits(*tiles[n])
        if lead <= n < len(tiles) + lead:
            p_q[n - lead] = softmax(s_q.pop(n - lead))
        if n >= lead + lag:
            pp, h = tiles[n - lead - lag]
            outs[(pp, h)] = weighted(p_q.pop(n - lead - lag), pp, h)
            if h == NA_HEADS - 1:
                o_t = jnp.concatenate([outs.pop((pp, hh)) for hh in range(NA_HEADS)], axis=0)
                o_ref[pp * pair:(pp + 1) * pair, :] = o_t.T.astype(BF16)


def _na_attention(qk, va_t, l, bias):
    L = qk.shape[0]
    rows = L // GRID_W
    nblk = rows // NA_ROWS
    blk = NA_ROWS * GRID_W
    grp = blk // LANES
    w = NA_HEADS * HEAD_DIM
    prev = lambda i: jnp.maximum(i - 1, 0)
    nxt = lambda i: jnp.minimum(i + 1, nblk - 1)
    spec = lambda f, c: pl.BlockSpec((blk, w), lambda i: (f(i), c))
    vspec = lambda f: pl.BlockSpec((grp, w, LANES), lambda i: (f(i), 0, 0))
    same = lambda i: i
    return pl.pallas_call(
        functools.partial(_na_kernel, rows=rows),
        grid=(nblk,),
        in_specs=[spec(same, 0),
                  spec(prev, 1), spec(same, 1), spec(nxt, 1),
                  vspec(prev), vspec(same), vspec(nxt),
                  _layer_spec(bias, l)],
        out_specs=pl.BlockSpec((blk, w), lambda i: (i, 0)),
        out_shape=jax.ShapeDtypeStruct((L, w), BF16),
        scratch_shapes=[pltpu.VMEM((3 * blk, w), BF16), pltpu.VMEM((3 * grp, w, LANES), BF16)],
        compiler_params=_cparams(("parallel",)),
        name="na_attn",
    )(qk, qk, qk, qk, va_t, va_t, va_t, bias)


def _swa_kernel(q_ref, kp_ref, kc_ref, kn_ref, vp_ref, vc_ref, vn_ref, band_ref, sink_ref, o_ref, *, nstep):
    i = pl.program_id(0)
    B = SWA_BLOCK
    G = SWA_Q_HEADS // SWA_KV_HEADS
    k_all = jnp.concatenate([kp_ref[...], kc_ref[...], kn_ref[...]], axis=0)
    vt_all = [vp_ref[0]] + [vc_ref[j] for j in range(SWA_QB)] + [vn_ref[0]]

    def band_index(b):
        if b == 0:
            return jnp.where(i == 0, 1, 0)
        if b == SWA_QB - 1:
            return jnp.where(i == nstep - 1, 2, 0)
        return 0

    def logits(b, g):
        ks = k_all[b * B:(b + 3) * B, g * HEAD_DIM:(g + 1) * HEAD_DIM]
        qs = jnp.concatenate([q_ref[b * B:(b + 1) * B, (g * G + hh) * HEAD_DIM:(g * G + hh + 1) * HEAD_DIM]
                              for hh in range(G)], axis=0)
        return lax.dot_general(ks, qs, (((1,), (1,)), ((), ())), preferred_element_type=F32) + band_ref[band_index(b)]

    def softmax(s, g):
        sink = sink_ref[g]
        m = jnp.maximum(jnp.max(s, axis=0, keepdims=True), sink)
        p = jnp.exp2(s - m)
        l = jnp.sum(p, axis=0, keepdims=True) + jnp.exp2(sink - m)
        return p.astype(BF16), 1.0 / l

    def weighted(p_inv, b, g):
        p, inv_l = p_inv
        vt = jnp.concatenate([vt_all[b + j][g * HEAD_DIM:(g + 1) * HEAD_DIM, :] for j in range(3)], axis=-1)
        return _dot(vt, p) * inv_l

    tiles = [(b, g) for b in range(SWA_QB) for g in range(SWA_KV_HEADS)]
    s_q, p_q = {}, {}
    lead, lag = 1, 1
    for n in range(len(tiles) + lead + lag):
        if n < len(tiles):
            s_q[n] = logits(*tiles[n])
        if lead <= n < len(tiles) + lead:
            p_q[n - lead] = softmax(s_q.pop(n - lead), tiles[n - lead][1])
        if n >= lead + lag:
            b, g = tiles[n - lead - lag]
            o_t = weighted(p_q.pop(n - lead - lag), b, g)
            for hh in range(0, G, 2):
                pair_t = jnp.concatenate([o_t[:, hh * B:(hh + 1) * B], o_t[:, (hh + 1) * B:(hh + 2) * B]], axis=0)
                h = g * G + hh
                o_ref[b * B:(b + 1) * B, h * HEAD_DIM:(h + 2) * HEAD_DIM] = pair_t.T.astype(BF16)


def _swa_band_tables():
    B = SWA_BLOCK
    G = SWA_Q_HEADS // SWA_KV_HEADS
    qi = np.arange(G * B)[None, :] % B
    kj = np.arange(3 * B)[:, None]
    band = np.abs(kj - B - qi) <= B
    tabs = [band, band & (kj >= B), band & (kj < 2 * B)]
    return jnp.asarray(np.where(np.stack(tabs), 0.0, NEG_BIG), F32)


def _swa_attention(qk, vs_t, l, band, sink_row):
    L = qk.shape[0]
    B = SWA_BLOCK
    T = SWA_QB * B
    nstep = L // T
    nblk = L // B
    prev = lambda i: jnp.maximum(i * SWA_QB - 1, 0)
    nxt = lambda i: jnp.minimum((i + 1) * SWA_QB, nblk - 1)
    return pl.pallas_call(
        functools.partial(_swa_kernel, nstep=nstep),
        grid=(nstep,),
        in_specs=[pl.BlockSpec((T, 512), lambda i: (i, 1)),
                  pl.BlockSpec((B, LANES), lambda i: (prev(i), 8)),
                  pl.BlockSpec((T, LANES), lambda i: (i, 8)),
                  pl.BlockSpec((B, LANES), lambda i: (nxt(i), 8)),
                  pl.BlockSpec((1, LANES, B), lambda i: (prev(i), 0, 0)),
                  pl.BlockSpec((SWA_QB, LANES, B), lambda i: (i, 0, 0)),
                  pl.BlockSpec((1, LANES, B), lambda i: (nxt(i), 0, 0)),
                  _const_spec(band), _layer_spec(sink_row, l)],
        out_specs=pl.BlockSpec((T, 512), lambda i: (i, 0)),
        out_shape=jax.ShapeDtypeStruct((L, 512), BF16),
        compiler_params=_cparams(("parallel",)),
        name="swa_attn",
    )(qk, qk, qk, qk, vs_t, vs_t, vs_t, band, sink_row)


def _bmm(a, b):
    return jnp.einsum('bij,bjk->bik', a.astype(BF16), b.astype(BF16), preferred_element_type=F32)


def _bmm_nt(a, b):
    return jnp.einsum('bid,bjd->bij', a.astype(BF16), b.astype(BF16), preferred_element_type=F32)


def _gdn_chunk_terms(qkv, gb, gbt, rev, out):
    G, C, H = GDN_G, GDN_CHUNK, GDN_HEADS

    def heads(part):
        return qkv[part * H:(part + 1) * H].reshape(H * G, C, HEAD_DIM)

    q = heads(0)
    k = heads(1)
    v = heads(2).astype(F32)
    kf = k.astype(F32)
    qf = q.astype(F32)
    d0 = H if rev else 0

    def colv(base):
        return jnp.concatenate([gb[:, base + h:base + h + 1].reshape(G, C, 1) for h in range(H)], axis=0)

    beta = jnp.broadcast_to(colv(d0), (H * G, C, HEAD_DIM))
    gcc = jnp.broadcast_to(colv(8 + d0), (H * G, C, HEAD_DIM))
    gl = jnp.broadcast_to(colv(16 + d0), (H * G, C, HEAD_DIM))
    grow = jnp.concatenate([gbt[8 + d0 + h:9 + d0 + h, c * C:(c + 1) * C].reshape(1, 1, C)
                            for h in range(H) for c in range(G)], axis=0)
    ii = lax.broadcasted_iota(jnp.int32, (1, C, C), 1)
    jj = lax.broadcasted_iota(jnp.int32, (1, C, C), 2)
    incl = (jj >= ii) if rev else (jj <= ii)
    strict = (jj > ii) if rev else (jj < ii)
    decay = jnp.exp(jnp.where(incl, gcc - grow, NEG_BIG))
    kk = _bmm_nt(k, k)
    qk = _bmm_nt(q, k)
    yield
    nmat = jnp.where(strict, kk * decay, 0.0) * beta
    eg = jnp.exp(gcc)
    rhs = jnp.concatenate([v * beta, kf * (beta * eg)], axis=-1)
    m = _bmm(nmat, jnp.concatenate([rhs, nmat], axis=-1))
    yield
    x = rhs - m[:, :, 0:2 * HEAD_DIM]
    p = m[:, :, 2 * HEAD_DIM:]
    for _ in range(4):
        m = _bmm(p, jnp.concatenate([x, p], axis=-1))
        yield
        x = x + m[:, :, 0:2 * HEAD_DIM]
        p = m[:, :, 2 * HEAD_DIM:]
    x = x + _bmm(p, x)
    yield
    u = x[:, :, 0:HEAD_DIM]
    w = x[:, :, HEAD_DIM:2 * HEAD_DIM]
    qkm = jnp.where(incl, qk * decay, 0.0)
    k_tail_t = jnp.swapaxes(kf * jnp.exp(gl - gcc), 1, 2)
    lhs1 = jnp.concatenate([w, qf * eg], axis=1).astype(BF16)
    lhs2 = jnp.concatenate([qkm, k_tail_t], axis=1).astype(BF16)
    dch = jnp.exp(gl[:, 0:8, :])
    out.extend([u, lhs1, lhs2, dch])


def _gdn_scan_group(u_ref, l1_ref, l2_ref, d_ref, s_ref, o_ref, rev):
    G, C, H = GDN_G, GDN_CHUNK, GDN_HEADS
    S = s_ref[...]
    for c in (range(G - 1, -1, -1) if rev else range(G)):
        r1 = _bmm(l1_ref[:, c], S)
        yield
        vn = u_ref[:, c] - r1[:, 0:C]
        r2 = _bmm(l2_ref[:, c], vn)
        yield
        oc = r1[:, C:2 * C] + r2[:, 0:C]
        S = S * d_ref[:, c][:, 0:1, :] + r2[:, C:2 * C]
        o_ref[c * C:(c + 1) * C, :] = jnp.concatenate([oc[h] for h in range(H)], axis=-1)
    s_ref[...] = S


def _gdn_kernel(qkvf_ref, gbf_ref, gbtf_ref, qkvb_ref, gbb_ref, gbtb_ref, of_ref, ob_ref,
                sf_ref, sb_ref, uf_ref, l1f_ref, l2f_ref, df_ref, ub_ref, l1b_ref, l2b_ref, db_ref):
    G, C, H = GDN_G, GDN_CHUNK, GDN_HEADS
    step = pl.program_id(0)

    @pl.when(step == 0)
    def _():
        for r in (sf_ref, sb_ref, uf_ref, l1f_ref, l2f_ref, df_ref, ub_ref, l1b_ref, l2b_ref, db_ref):
            r[...] = jnp.zeros_like(r)

    terms_f, terms_b = [], []
    strands = [_gdn_scan_group(uf_ref, l1f_ref, l2f_ref, df_ref, sf_ref, of_ref, False),
               _gdn_scan_group(ub_ref, l1b_ref, l2b_ref, db_ref, sb_ref, ob_ref, True),
               _gdn_chunk_terms(qkvf_ref[...], gbf_ref[...], gbtf_ref[...], False, terms_f),
               _gdn_chunk_terms(qkvb_ref[...], gbb_ref[...], gbtb_ref[...], True, terms_b)]
    while strands:
        strands = [s for s in strands if next(s, True) is None]
    for (u, l1, l2, dch), (u_ref, l1_ref, l2_ref, d_ref) in ((terms_f, (uf_ref, l1f_ref, l2f_ref, df_ref)),
                                                          (terms_b, (ub_ref, l1b_ref, l2b_ref, db_ref))):
        u_ref[...] = u.reshape(H, G, C, HEAD_DIM)
        l1_ref[...] = l1.reshape(H, G, 2 * C, HEAD_DIM)
        l2_ref[...] = l2.reshape(H, G, 2 * C, C)
        d_ref[...] = dch.reshape(H, G, 8, HEAD_DIM)


def _gdn(qkv, gb, gbt):
    L = qkv.shape[1]
    G, C, H = GDN_G, GDN_CHUNK, GDN_HEADS
    T = G * C
    n = L // T
    fin = lambda s: jnp.minimum(s, n - 1)
    bin_ = lambda s: jnp.maximum(n - 1 - s, 0)
    fout = lambda s: jnp.maximum(s - 1, 0)
    bout = lambda s: jnp.minimum(n - s, n - 1)
    term_scratch = [pltpu.VMEM((H, G, C, HEAD_DIM), F32), pltpu.VMEM((H, G, 2 * C, HEAD_DIM), BF16),
                    pltpu.VMEM((H, G, 2 * C, C), BF16), pltpu.VMEM((H, G, 8, HEAD_DIM), F32)]
    return pl.pallas_call(
        _gdn_kernel,
        grid=(n + 1,),
        in_specs=[pl.BlockSpec((3 * H, T, HEAD_DIM), lambda s: (0, fin(s), 0)),
                  pl.BlockSpec((T, LANES), lambda s: (fin(s), 0)),
                  pl.BlockSpec((24, T), lambda s: (0, fin(s))),
                  pl.BlockSpec((3 * H, T, HEAD_DIM), lambda s: (0, bin_(s), 0)),
                  pl.BlockSpec((T, LANES), lambda s: (bin_(s), 0)),
                  pl.BlockSpec((24, T), lambda s: (0, bin_(s)))],
        out_specs=[pl.BlockSpec((T, 256), lambda s: (fout(s), 0)),
                   pl.BlockSpec((T, 256), lambda s: (bout(s), 0))],
        out_shape=[jax.ShapeDtypeStruct((L, 256), F32), jax.ShapeDtypeStruct((L, 256), F32)],
        scratch_shapes=[pltpu.VMEM((H, HEAD_DIM, HEAD_DIM), F32), pltpu.VMEM((H, HEAD_DIM, HEAD_DIM), F32)]
                       + term_scratch + term_scratch,
        compiler_params=_cparams(("arbitrary",)),
        name="gdn_scan",
    )(qkv, gb, gbt, qkv, gb, gbt)


def _merge_kernel(x_ref, yna_ref, yswa_ref, of_ref, ob_ref, z_ref, gate_ref, gn_ref, bd_ref,
                  wna_ref, wswa_ref, wgdn_ref, wout_ref, o_ref):
    o = of_ref[...] + ob_ref[...]
    z = z_ref[...].astype(F32)
    ms = _group_sum(o * o, bd_ref) * (1.0 / HEAD_DIM)
    ygdn = o * lax.rsqrt(ms + NORM_EPS) * gn_ref[...] * (z * _sigmoid(z))
    m = (gate_ref[:, 0:D_MODEL].astype(F32) * _dot(yna_ref[...], wna_ref[...])
         + gate_ref[:, D_MODEL:2 * D_MODEL].astype(F32) * _dot(yswa_ref[...], wswa_ref[...])
         + gate_ref[:, 2 * D_MODEL:3 * D_MODEL].astype(F32) * _dot(ygdn.astype(BF16), wgdn_ref[...]))
    o_ref[...] = x_ref[...] + _dot(m.astype(BF16), wout_ref[...])


def _merge(x, yna, yswa, of, ob, z, gate, l, gn, bd, wna, wswa, wgdn, wout):
    L = x.shape[0]
    row = lambda w_: pl.BlockSpec((TM, w_), lambda i: (i, 0))
    return pl.pallas_call(
        _merge_kernel,
        grid=(L // TM,),
        in_specs=[row(D_MODEL), row(256), row(512), row(256), row(256), row(256), row(SEG_GATE),
                  _layer_spec(gn, l), _const_spec(bd), _layer_spec(wna, l), _layer_spec(wswa, l),
                  _layer_spec(wgdn, l), _layer_spec(wout, l)],
        out_specs=row(D_MODEL),
        out_shape=jax.ShapeDtypeStruct((L, D_MODEL), F32),
        compiler_params=_cparams(("parallel",)),
        name="merge",
    )(x, yna, yswa, of, ob, z, gate, gn, bd, wna, wswa, wgdn, wout)


def _ffn_kernel(x_ref, xp_ref, xn_ref, g_ref, wu_ref, cw_ref, cb_ref, wd_ref, o_ref, hbuf, *, nblk):
    i = pl.program_id(0)

    def normed(xv):
        ms = jnp.mean(xv * xv, axis=-1, keepdims=True)
        return xv * lax.rsqrt(ms + NORM_EPS) * g_ref[...]

    hbuf[0:HALO, :] = jnp.where(i > 0, normed(xp_ref[...]), 0.0).astype(BF16)
    hbuf[HALO:HALO + TM, :] = normed(x_ref[...]).astype(BF16)
    hbuf[HALO + TM:, :] = jnp.where(i < nblk - 1, normed(xn_ref[...]), 0.0).astype(BF16)
    h = hbuf[...]
    n = TM + 2 * HALO
    chunks = [slice(a, b) for a, b in zip(FFN_SPLIT[:-1], FFN_SPLIT[1:])]
    gate_cols = [slice(D_FF + a, D_FF + b) for a, b in zip(FFN_SPLIT[:-1], FFN_SPLIT[1:])]

    def conv(u, cols):
        y = (pltpu.roll(u, 1, 0) * cw_ref[0:1, cols] + u * cw_ref[1:2, cols]
             + pltpu.roll(u, n - 1, 0) * cw_ref[2:3, cols])
        return y[HALO:HALO + TM] + cb_ref[:, cols]

    ups = [(_dot(h, wu_ref[:, ca]), _dot(h, wu_ref[:, cb])) for ca, cb in zip(chunks, gate_cols)]
    acc = x_ref[...]
    for (ua, ub), ca, cb in zip(ups, chunks, gate_cols):
        a = conv(ua, ca)
        gated = (a * _sigmoid(a) * conv(ub, cb)).astype(BF16)
        acc = acc + _dot(gated, wd_ref[ca, :])
    o_ref[...] = acc


def _ffn(x, l, g, w_up, conv_w, conv_b, w_down):
    L = x.shape[0]
    nblk = L // TM
    hb = TM // HALO
    once = pl.Buffered(1)
    return pl.pallas_call(
        functools.partial(_ffn_kernel, nblk=nblk),
        grid=(nblk,),
        in_specs=[pl.BlockSpec((TM, D_MODEL), lambda i: (i, 0)),
                  pl.BlockSpec((HALO, D_MODEL), lambda i: (jnp.maximum(i * hb - 1, 0), 0)),
                  pl.BlockSpec((HALO, D_MODEL), lambda i: (jnp.minimum((i + 1) * hb, L // HALO - 1), 0)),
                  _layer_spec(g, l), _layer_spec(w_up, l, pipeline_mode=once), _layer_spec(conv_w, l),
                  _layer_spec(conv_b, l), _layer_spec(w_down, l, pipeline_mode=once)],
        out_specs=pl.BlockSpec((TM, D_MODEL), lambda i: (i, 0)),
        out_shape=jax.ShapeDtypeStruct((L, D_MODEL), F32),
        scratch_shapes=[pltpu.VMEM((TM + 2 * HALO, D_MODEL), BF16)],
        compiler_params=_cparams(("parallel",)),
        name="ffn",
    )(x, x, x, g, w_up, conv_w, conv_b, w_down)


PACK_SEGMENTS = ((0, 256, 0), (256, 512, 256), (768, 1280, 512), (1280, 1408, 1024),
                 (1536, 2304, 1152), (512, 768, 1920), (2304, 2560, 2176), (1408, 1536, 2432),
                 (2576, 5648, 2560), (2560, 2576, 5632))
PACK_K = 256
PACK_PIECE = 512


def _pack_kernel(w_ref, o_ref):
    for s0, s1, d0 in PACK_SEGMENTS[:-1]:
        for p0 in range(s0, s1, PACK_PIECE):
            p1 = min(p0 + PACK_PIECE, s1)
            o_ref[:, d0 + p0 - s0:d0 + p1 - s0] = w_ref[p0:p1, :].T.astype(BF16)
    s0, s1, d0 = PACK_SEGMENTS[-1]
    lane = lax.broadcasted_iota(jnp.int32, (1, LANES), 1)
    o_ref[:, d0:d0 + LANES] = jnp.where(lane < s1 - s0, w_ref[s0:s0 + LANES, :].T, 0.0).astype(BF16)


def _pack_w_in(w):
    depth, k, n = w.shape
    return pl.pallas_call(
        _pack_kernel,
        grid=(depth, k // PACK_K),
        in_specs=[pl.BlockSpec((None, n, PACK_K), lambda l, i: (l, 0, i))],
        out_specs=pl.BlockSpec((None, PACK_K, IN_PACKED), lambda l, i: (l, i, 0)),
        out_shape=jax.ShapeDtypeStruct((depth, k, IN_PACKED), BF16),
        compiler_params=_cparams(("parallel", "parallel")),
        name="pack_w_in",
    )(jnp.swapaxes(w, 1, 2))


def _rope_tables(L):
    inv = 1.0 / (ROPE_THETA ** (np.arange(0, HEAD_DIM, 2, dtype=np.float64) / HEAD_DIM))
    freq = np.tile(inv, LANES // inv.size)
    sign = np.where(np.arange(LANES) % HEAD_DIM < HEAD_DIM // 2, -1.0, 1.0)
    start = np.arange(L // TM, dtype=np.float64)[:, None] * TM * freq[None, :]
    off = np.arange(TM, dtype=np.float64)[:, None] * freq[None, :]
    blk = np.stack([np.cos(start), np.sin(start)], axis=1)
    offs = np.stack([np.cos(off), np.sin(off), sign * np.cos(off), sign * np.sin(off)])
    return jnp.asarray(blk, F32), jnp.asarray(offs, F32)


def kernel(x, attn_norm, w_in, qk_norm, na_rpb, swa_sink, gdn_conv_w, gdn_a_log, gdn_dt_bias, gdn_norm,
           w_branch_na, w_branch_swa, w_branch_gdn, w_out, ffn_norm, w_up, ffn_conv_w, ffn_conv_b, w_down):
    B, L, D = x.shape
    assert B == 1 and D == D_MODEL and L % (NA_ROWS * GRID_W) == 0 and L // GRID_W >= 2 * NA_ROWS
    depth = w_in.shape[0]
    rope_blk, rope_off = _rope_tables(L)
    blockdiag = jnp.asarray(np.kron(np.eye(4), np.ones((HEAD_DIM, HEAD_DIM))), BF16)
    bias_tabs = _na_bias_tables(na_rpb, L // GRID_W)
    swa_band = _swa_band_tables()
    scale = HEAD_DIM ** -0.5 * LOG2E
    G = SWA_Q_HEADS // SWA_KV_HEADS
    gain = jnp.concatenate([jnp.tile(qk_norm[:, 0] * scale, (1, NA_HEADS)), jnp.tile(qk_norm[:, 1], (1, NA_HEADS)),
                            jnp.tile(qk_norm[:, 2] * scale, (1, SWA_Q_HEADS)),
                            jnp.tile(qk_norm[:, 3], (1, SWA_KV_HEADS))], axis=1)[:, None, :]
    sink_row = jnp.repeat(swa_sink.reshape(depth, SWA_KV_HEADS, G) * LOG2E, SWA_BLOCK, axis=2)[:, :, None, :]
    alog_row = jnp.pad(gdn_a_log.reshape(depth, 1, 8), ((0, 0), (0, 0), (8, LANES - 16)))
    dtb_row = jnp.pad(gdn_dt_bias.reshape(depth, 1, 8), ((0, 0), (0, 0), (8, LANES - 16)))
    gdn_gain = jnp.tile(gdn_norm, (1, GDN_HEADS))[:, None, :]
    attn_g = attn_norm[:, None, :]
    ffn_g = ffn_norm[:, None, :]
    ffn_b = ffn_conv_b[:, None, :]
    w_in_p = _pack_w_in(w_in)
    w_na, w_swa, w_gdn = w_branch_na.astype(BF16), w_branch_swa.astype(BF16), w_branch_gdn.astype(BF16)
    w_o, w_u, w_d = w_out.astype(BF16), w_up.astype(BF16), w_down.astype(BF16)
    xs = x[0]
    for l in range(depth):
        qk, z, gate, va_t, vs_t, qkv_n, gb, gbt = _inproj(xs, l, attn_g, w_in_p, gain, rope_blk, rope_off, blockdiag,
                                                          gdn_conv_w, alog_row, dtb_row)
        y_na = _na_attention(qk, va_t, l, bias_tabs)
        y_swa = _swa_attention(qk, vs_t, l, swa_band, sink_row)
        o_f, o_b = _gdn(qkv_n, gb, gbt)
        xs = _merge(xs, y_na, y_swa, o_f, o_b, z, gate, l, gdn_gain, blockdiag, w_na, w_swa, w_gdn, w_o)
        xs = _ffn(xs, l, ffn_g, w_u, ffn_conv_w, ffn_b, w_d)
    return xs[None]
```

```python
import functools

import numpy as np
import jax
import jax.numpy as jnp
from jax import lax
from jax.experimental import pallas as pl
from jax.experimental.pallas import tpu as pltpu

F32 = jnp.float32
BF16 = jnp.bfloat16

D_MODEL = 1024
HEAD_DIM = 64
GRID_W = 64
NORM_EPS = 1e-6
NA_HEADS = 4
NA_KH = 8
NA_KW = 16
SWA_Q_HEADS = 8
SWA_KV_HEADS = 2
SWA_BLOCK = 128
ROPE_THETA = 10000.0
GDN_HEADS = 4
GDN_CHUNK = 64
D_FF = 2816

LANES = 128
NEG_BIG = -1e30
LOG2E = 1.4426950408889634
VMEM_LIMIT = 56 * 1024 * 1024

SEG_QK = 1152
SEG_PLAIN = 1408
PLAIN_VA = 768
PLAIN_Z = 1024
PLAIN_VS = 1280
SEG_GATE = 3072
SEG_BA = 128
IN_PACKED = SEG_QK + SEG_PLAIN + SEG_GATE + SEG_BA

TM = 512
NA_ROWS = 16
NA_WIN = NA_KH + 2
SWA_QB = 8
GDN_G = 4
MXU_TILE = 256
FFN_SPLIT = (0, 6 * MXU_TILE, D_FF)
HALO = 8


def _cparams(sem):
    return pltpu.CompilerParams(dimension_semantics=sem, vmem_limit_bytes=VMEM_LIMIT)


def _dot(a, b):
    return jnp.dot(a, b, preferred_element_type=F32)


def _sigmoid(x):
    return 1.0 / (1.0 + jnp.exp(-x))


def _group_sum(sq, bd_ref):
    w = sq.shape[-1]
    return _dot(sq.astype(BF16), bd_ref[0:w, 0:w])


def _gdn_token_qkv(x, row_prev, row_next, cw_ref, bd_ref, qkv_ref):
    rid = lax.broadcasted_iota(jnp.int32, (TM, 1), 0)
    xprev = jnp.where(rid == 0, row_prev, pltpu.roll(x, 1, 0))
    xnext = jnp.where(rid == TM - 1, row_next, pltpu.roll(x, TM - 1, 0))
    y = xprev * cw_ref[0:1, :] + x * cw_ref[1:2, :] + xnext * cw_ref[2:3, :]
    y = y * _sigmoid(y)
    for part in range(3):
        yp = y[:, part * 256:(part + 1) * 256]
        if part < 2:
            yp = yp * lax.rsqrt(_group_sum(yp * yp, bd_ref) + NORM_EPS)
        if part == 0:
            yp = yp * (HEAD_DIM ** -0.5)
        yp = yp.astype(BF16)
        for hh in range(GDN_HEADS):
            qkv_ref[part * GDN_HEADS + hh] = yp[:, hh * HEAD_DIM:(hh + 1) * HEAD_DIM]


def _gdn_token_gates(ba, alog_ref, dtb_ref, gb_ref, gbt_ref):
    rid = lax.broadcasted_iota(jnp.int32, (TM, 1), 0)
    lane = lax.broadcasted_iota(jnp.int32, (1, LANES), 1)
    beta = _sigmoid(ba)
    sp_in = ba + dtb_ref[...]
    softplus = jnp.maximum(sp_in, 0.0) + jnp.log(1.0 + jnp.exp(-jnp.abs(sp_in)))
    g = jnp.where((lane >= 8) & (lane < 16), -jnp.exp(alog_ref[...]) * softplus, 0.0)
    rc = rid % GDN_CHUNK
    pre = g
    suf = g
    s = 1
    while s < GDN_CHUNK:
        pre = pre + jnp.where(rc >= s, pltpu.roll(pre, s, 0), 0.0)
        suf = suf + jnp.where(rc < GDN_CHUNK - s, pltpu.roll(suf, TM - s, 0), 0.0)
        s *= 2
    tot = pre + suf - g
    gc = jnp.where(lane < 12, pre, suf)
    slab = jnp.where(lane < 8, beta, jnp.where(lane < 16, gc, jnp.where(lane < 24, pltpu.roll(tot, 8, 1), 0.0)))
    gb_ref[...] = slab
    gbt_ref[...] = slab.T[0:24, :]


def _inproj_kernel(x_ref, xp_ref, xn_ref, g_ref, w_ref, gain_ref, rope_blk_ref, rope_off_ref, bd_ref,
                   cw_ref, alog_ref, dtb_ref,
                   oqk_ref, oz_ref, ogate_ref, ovat_ref, ovst_ref, oqkvh_ref, ogb_ref, ogbt_ref, *, nblk):
    i = pl.program_id(0)

    def normed(xv):
        ms = jnp.mean(xv * xv, axis=-1, keepdims=True)
        return xv * lax.rsqrt(ms + NORM_EPS) * g_ref[...]

    h = normed(x_ref[...]).astype(BF16)
    h_halo = jnp.concatenate([jnp.where(i > 0, normed(xp_ref[...]), 0.0),
                              jnp.where(i < nblk - 1, normed(xn_ref[...]), 0.0)], axis=0).astype(BF16)
    cw = 2 * LANES
    lane = lax.broadcasted_iota(jnp.int32, (1, cw), 1)
    first_half = (lane % HEAD_DIM) < (HEAD_DIM // 2)
    c_a, s_a = rope_blk_ref[0:1, :], rope_blk_ref[1:2, :]
    cos128 = c_a * rope_off_ref[0] - s_a * rope_off_ref[1]
    sin128 = s_a * rope_off_ref[2] + c_a * rope_off_ref[3]
    cos = jnp.concatenate([cos128, cos128], axis=1)
    sin = jnp.concatenate([sin128, sin128], axis=1)

    def head_norm(t, c):
        cols = slice(c * cw, min((c + 1) * cw, SEG_QK))
        w = cols.stop - cols.start
        tc = t[:, cols]
        ss = _group_sum(tc * tc, bd_ref)
        y = tc * lax.rsqrt(ss * (1.0 / HEAD_DIM) + NORM_EPS) * gain_ref[:, cols]
        if cols.start >= 512:
            rot = jnp.where(first_half[:, 0:w], pltpu.roll(y, w - HEAD_DIM // 2, 1), pltpu.roll(y, HEAD_DIM // 2, 1))
            y = y * cos[:, 0:w] + rot * sin[:, 0:w]
        oqk_ref[:, cols] = y.astype(BF16)

    o_plain, o_gate, o_ba = SEG_QK, SEG_QK + SEG_PLAIN, SEG_QK + SEG_PLAIN + SEG_GATE
    gate_cols = [slice(o_gate + c * D_MODEL, o_gate + (c + 1) * D_MODEL) for c in range(3)]
    out_cols = [slice(c * D_MODEL, (c + 1) * D_MODEL) for c in range(3)]
    t = _dot(h, w_ref[:, 0:SEG_QK])
    t_gdn = _dot(h, w_ref[:, o_plain:o_plain + 768])
    tba = _dot(h, w_ref[:, o_ba:o_ba + SEG_BA])
    t_halo = _dot(h_halo, w_ref[:, o_plain:o_plain + 768])
    for c in (0, 1):
        head_norm(t, c)
    tg0 = _dot(h, w_ref[:, gate_cols[0]])
    for c in (2, 3, 4):
        head_norm(t, c)
    tg1 = _dot(h, w_ref[:, gate_cols[1]])
    ogate_ref[:, out_cols[0]] = _sigmoid(tg0).astype(BF16)
    _gdn_token_qkv(t_gdn, t_halo[HALO - 1:HALO, :], t_halo[HALO:HALO + 1, :], cw_ref, bd_ref, oqkvh_ref)
    tg2 = _dot(h, w_ref[:, gate_cols[2]])
    ogate_ref[:, out_cols[1]] = _sigmoid(tg1).astype(BF16)
    _gdn_token_gates(tba, alog_ref, dtb_ref, ogb_ref, ogbt_ref)
    t_rest = _dot(h, w_ref[:, o_plain + PLAIN_VA:o_plain + SEG_PLAIN])
    ogate_ref[:, out_cols[2]] = _sigmoid(tg2).astype(BF16)
    oz_ref[...] = t_rest[:, PLAIN_Z - PLAIN_VA:PLAIN_Z - PLAIN_VA + 256].astype(BF16)
    for grp in range(TM // LANES):
        rows = slice(grp * LANES, (grp + 1) * LANES)
        ovat_ref[grp] = t_rest[rows, 0:256].T.astype(BF16)
        ovst_ref[grp] = t_rest[rows, PLAIN_VS - PLAIN_VA:PLAIN_VS - PLAIN_VA + 128].T.astype(BF16)


def _layer_spec(a, l, **kw):
    nd = a.ndim - 1
    return pl.BlockSpec((None,) + a.shape[1:], lambda *_: (l,) + (0,) * nd, **kw)


def _const_spec(a):
    return pl.BlockSpec(a.shape, lambda *_: (0,) * a.ndim)


def _inproj(x, l, g, w, gain, rope_blk, rope_off, bd, conv_w, alog_row, dtb_row):
    L = x.shape[0]
    nblk = L // TM
    hb = TM // HALO
    row = lambda w_: pl.BlockSpec((TM, w_), lambda i: (i, 0))
    return pl.pallas_call(
        functools.partial(_inproj_kernel, nblk=nblk),
        grid=(nblk,),
        in_specs=[row(D_MODEL),
                  pl.BlockSpec((HALO, D_MODEL), lambda i: (jnp.maximum(i * hb - 1, 0), 0)),
                  pl.BlockSpec((HALO, D_MODEL), lambda i: (jnp.minimum((i + 1) * hb, L // HALO - 1), 0)),
                  _layer_spec(g, l), _layer_spec(w, l, pipeline_mode=pl.Buffered(1)),
                  _layer_spec(gain, l), pl.BlockSpec((None, 2, LANES), lambda i: (i, 0, 0)), _const_spec(rope_off),
                  _const_spec(bd), _layer_spec(conv_w, l), _layer_spec(alog_row, l), _layer_spec(dtb_row, l)],
        out_specs=[row(SEG_QK), row(256), row(SEG_GATE),
                   pl.BlockSpec((TM // LANES, 256, LANES), lambda i: (i, 0, 0)),
                   pl.BlockSpec((TM // LANES, 128, LANES), lambda i: (i, 0, 0)),
                   pl.BlockSpec((3 * GDN_HEADS, TM, HEAD_DIM), lambda i: (0, i, 0)),
                   row(LANES),
                   pl.BlockSpec((24, TM), lambda i: (0, i))],
        out_shape=[jax.ShapeDtypeStruct((L, SEG_QK), BF16),
                   jax.ShapeDtypeStruct((L, 256), BF16),
                   jax.ShapeDtypeStruct((L, SEG_GATE), BF16),
                   jax.ShapeDtypeStruct((L // LANES, 256, LANES), BF16),
                   jax.ShapeDtypeStruct((L // LANES, 128, LANES), BF16),
                   jax.ShapeDtypeStruct((3 * GDN_HEADS, L, HEAD_DIM), BF16),
                   jax.ShapeDtypeStruct((L, LANES), F32),
                   jax.ShapeDtypeStruct((24, L), F32)],
        compiler_params=_cparams(("parallel",)),
        name="inproj",
    )(x, x, x, g, w, gain, rope_blk, rope_off, bd, conv_w, alog_row, dtb_row)


def _toeplitz_kernel(rpb_ref, onehot_ref, colmask_ref, o_ref):
    r = rpb_ref[...]
    hi = r.astype(BF16)
    r1 = r - hi.astype(F32)
    mid = r1.astype(BF16)
    lo = (r1 - mid.astype(F32)).astype(BF16)
    oh = onehot_ref[...]
    w = _dot(hi, oh) + _dot(mid, oh) + _dot(lo, oh)
    o_ref[...] = jnp.where(colmask_ref[...] > 0.0, w * LOG2E, NEG_BIG)


def _na_assemble_kernel(tiles_ref, o_ref, *, a_idx):
    neg = jnp.full((GRID_W, GRID_W), NEG_BIG, F32)
    for t in range(a_idx.shape[0]):
        for j in range(NA_WIN):
            for e in range(2):
                a = int(a_idx[t, j, e])
                o_ref[t, j * GRID_W:(j + 1) * GRID_W, e * GRID_W:(e + 1) * GRID_W] = neg if a < 0 else tiles_ref[a]


def _na_bias_tables(na_rpb, rows):
    depth, H, nr, nc = na_rpb.shape
    kc = np.arange(GRID_W)[:, None]
    qc = np.arange(GRID_W)[None, :]
    dc = np.clip(kc - qc + (NA_KW - 1), 0, 2 * NA_KW - 2).reshape(-1)
    onehot = np.zeros((32, GRID_W * GRID_W), np.float32)
    onehot[dc, np.arange(GRID_W * GRID_W)] = 1.0
    col_start = np.clip(qc - NA_KW // 2, 0, GRID_W - NA_KW)
    colmask = ((kc >= col_start) & (kc < col_start + NA_KW)).astype(np.float32).reshape(1, -1)
    rpb_rows = jnp.pad(na_rpb, ((0, 0), (0, 0), (0, 16 - nr), (0, 32 - nc))).reshape(depth * H * 16, 32)
    tiles = pl.pallas_call(
        _toeplitz_kernel,
        out_shape=jax.ShapeDtypeStruct((depth * H * 16, GRID_W * GRID_W), F32),
        name="na_bias_tiles",
    )(rpb_rows, jnp.asarray(onehot, BF16), jnp.asarray(colmask))
    tiles = tiles.reshape(depth * H, 16, GRID_W, GRID_W)
    a_idx = np.stack([_na_pair_structure(r0, rows)[1] for r0 in _na_pair_type_rows(rows)])
    nt = a_idx.shape[0]
    tabs = pl.pallas_call(
        functools.partial(_na_assemble_kernel, a_idx=a_idx),
        grid=(depth * H,),
        in_specs=[pl.BlockSpec((None, 16, GRID_W, GRID_W), lambda i: (i, 0, 0, 0))],
        out_specs=pl.BlockSpec((None, nt, NA_WIN * GRID_W, 2 * GRID_W), lambda i: (i, 0, 0, 0)),
        out_shape=jax.ShapeDtypeStruct((depth * H, nt, NA_WIN * GRID_W, 2 * GRID_W), F32),
        compiler_params=_cparams(("parallel",)),
        name="na_bias_tables",
    )(tiles)
    return tabs.reshape(depth, H, nt, NA_WIN * GRID_W, 2 * GRID_W)


def _na_pair_structure(r0, rows):
    wstart = int(np.clip(r0 - NA_KH // 2, 0, rows - NA_KH - 1)) // 2 * 2
    a_idx = np.full((NA_WIN, 2), -1, np.int64)
    for e in range(2):
        rr = r0 + e
        rs = int(np.clip(rr - NA_KH // 2, 0, rows - NA_KH))
        for j in range(NA_WIN):
            krow = wstart + j
            if rs <= krow < rs + NA_KH:
                a_idx[j, e] = krow - rr + NA_KH - 1
    return wstart, a_idx


def _na_pair_type_rows(rows):
    reps = [0, 2, 4, rows - 4, rows - 2]
    for r0 in range(0, rows, 2):
        t = 3 + (r0 - (rows - 4)) // 2 if r0 >= rows - 4 else min(r0 // 2, 2)
        ws, a = _na_pair_structure(r0, rows)
        ws_t, a_t = _na_pair_structure(reps[t], rows)
        assert (a == a_t).all() and r0 - ws == reps[t] - ws_t
    return reps


def _na_kernel(q_ref, kp_ref, kc_ref, kn_ref, vp_ref, vc_ref, vn_ref, bias_ref, o_ref, kbuf, vbuf, *, rows):
    i = pl.program_id(0)
    blk = NA_ROWS * GRID_W
    grp = blk // LANES
    kbuf[0:blk, :] = kp_ref[...]
    kbuf[blk:2 * blk, :] = kc_ref[...]
    kbuf[2 * blk:3 * blk, :] = kn_ref[...]
    vbuf[0:grp] = vp_ref[...]
    vbuf[grp:2 * grp] = vc_ref[...]
    vbuf[2 * grp:3 * grp] = vn_ref[...]
    win = NA_WIN * GRID_W
    wgrp = win // LANES
    pair = 2 * GRID_W

    npair = NA_ROWS // 2
    goffs, types = [], []
    for pp in range(npair):
        r0 = i * NA_ROWS + 2 * pp
        wstart = jnp.clip(r0 - NA_KH // 2, 0, rows - NA_KH - 1) // 2 * 2
        types.append(jnp.where(r0 >= rows - 4, 3 + (r0 - (rows - 4)) // 2, jnp.minimum(r0 // 2, 2)))
        goffs.append((wstart - (i - 1) * NA_ROWS) // 2)

    def logits(pp, h):
        sl = slice(h * HEAD_DIM, (h + 1) * HEAD_DIM)
        kw = kbuf[pl.ds(pl.multiple_of(goffs[pp] * LANES, LANES), win), sl]
        s = lax.dot_general(kw, q_ref[pp * pair:(pp + 1) * pair, sl],
                            (((1,), (1,)), ((), ())), preferred_element_type=F32)
        return s + bias_ref[h, types[pp]]

    def softmax(s):
        m = jnp.max(s, axis=0, keepdims=True)
        p = jnp.exp2(s - m)
        return p.astype(BF16), 1.0 / jnp.sum(p, axis=0, keepdims=True)

    def weighted(p_inv, pp, h):
        p, inv_l = p_inv
        vt = jnp.concatenate([vbuf[goffs[pp] + j, h * HEAD_DIM:(h + 1) * HEAD_DIM, :] for j in range(wgrp)], axis=-1)
        return _dot(vt, p) * inv_l

    tiles = [(pp, h) for pp in range(npair) for h in range(NA_HEADS)]
    s_q, p_q, outs = {}, {}, {}
    lead, lag = 2, 1
    for n in range(len(tiles) + lead + lag):
        if n < len(tiles):
            s_q[n] = logits(*tiles[n])
        if lead <= n < len(tiles) + lead:
            p_q[n - lead] = softmax(s_q.pop(n - lead))
        if n >= lead + lag:
            pp, h = tiles[n - lead - lag]
            outs[(pp, h)] = weighted(p_q.pop(n - lead - lag), pp, h)
            if h == NA_HEADS - 1:
                o_t = jnp.concatenate([outs.pop((pp, hh)) for hh in range(NA_HEADS)], axis=0)
                o_ref[pp * pair:(pp + 1) * pair, :] = o_t.T.astype(BF16)


def _na_attention(qk, va_t, l, bias):
    L = qk.shape[0]
    rows = L // GRID_W
    nblk = rows // NA_ROWS
    blk = NA_ROWS * GRID_W
    grp = blk // LANES
    w = NA_HEADS * HEAD_DIM
    prev = lambda i: jnp.maximum(i - 1, 0)
    nxt = lambda i: jnp.minimum(i + 1, nblk - 1)
    spec = lambda f, c: pl.BlockSpec((blk, w), lambda i: (f(i), c))
    vspec = lambda f: pl.BlockSpec((grp, w, LANES), lambda i: (f(i), 0, 0))
    same = lambda i: i
    return pl.pallas_call(
        functools.partial(_na_kernel, rows=rows),
        grid=(nblk,),
        in_specs=[spec(same, 0),
                  spec(prev, 1), spec(same, 1), spec(nxt, 1),
                  vspec(prev), vspec(same), vspec(nxt),
                  _layer_spec(bias, l)],
        out_specs=pl.BlockSpec((blk, w), lambda i: (i, 0)),
        out_shape=jax.ShapeDtypeStruct((L, w), BF16),
        scratch_shapes=[pltpu.VMEM((3 * blk, w), BF16), pltpu.VMEM((3 * grp, w, LANES), BF16)],
        compiler_params=_cparams(("parallel",)),
        name="na_attn",
    )(qk, qk, qk, qk, va_t, va_t, va_t, bias)


def _swa_kernel(q_ref, kp_ref, kc_ref, kn_ref, vp_ref, vc_ref, vn_ref, band_ref, sink_ref, o_ref, *, nstep):
    i = pl.program_id(0)
    B = SWA_BLOCK
    G = SWA_Q_HEADS // SWA_KV_HEADS
    k_all = jnp.concatenate([kp_ref[...], kc_ref[...], kn_ref[...]], axis=0)
    vt_all = [vp_ref[0]] + [vc_ref[j] for j in range(SWA_QB)] + [vn_ref[0]]

    def band_index(b):
        if b == 0:
            return jnp.where(i == 0, 1, 0)
        if b == SWA_QB - 1:
            return jnp.where(i == nstep - 1, 2, 0)
        return 0

    def logits(b, hp):
        g = hp // (G // 2)
        ks = k_all[b * B:(b + 3) * B, g * HEAD_DIM:(g + 1) * HEAD_DIM]
        qs = jnp.concatenate([q_ref[b * B:(b + 1) * B, (2 * hp + e) * HEAD_DIM:(2 * hp + e + 1) * HEAD_DIM]
                              for e in range(2)], axis=0)
        s = lax.dot_general(ks, qs, (((1,), (1,)), ((), ())), preferred_element_type=F32)
        t = band_index(b)
        return jnp.concatenate([s[0:B] + band_ref[t, 0:B], s[B:2 * B], s[2 * B:3 * B] + band_ref[t, 2 * B:3 * B]],
                               axis=0)

    def softmax(s, hp):
        sink = sink_ref[hp]
        m = jnp.maximum(jnp.max(s, axis=0, keepdims=True), sink)
        p = jnp.exp2(s - m)
        l = jnp.sum(p, axis=0, keepdims=True) + jnp.exp2(sink - m)
        return p.astype(BF16), 1.0 / l

    def weighted(p_inv, b, hp):
        p, inv_l = p_inv
        g = hp // (G // 2)
        vt = jnp.concatenate([vt_all[b + j][g * HEAD_DIM:(g + 1) * HEAD_DIM, :] for j in range(3)], axis=-1)
        return _dot(vt, p) * inv_l

    tiles = [(b, hp) for b in range(SWA_QB) for hp in range(SWA_Q_HEADS // 2)]
    s_q, p_q = {}, {}
    lead, lag = 2, 1
    for n in range(len(tiles) + lead + lag):
        if n < len(tiles):
            s_q[n] = logits(*tiles[n])
        if lead <= n < len(tiles) + lead:
            p_q[n - lead] = softmax(s_q.pop(n - lead), tiles[n - lead][1])
        if n >= lead + lag:
            b, hp = tiles[n - lead - lag]
            o_t = weighted(p_q.pop(n - lead - lag), b, hp)
            pair_t = jnp.concatenate([o_t[:, 0:B], o_t[:, B:2 * B]], axis=0)
            o_ref[b * B:(b + 1) * B, 2 * hp * HEAD_DIM:(2 * hp + 2) * HEAD_DIM] = pair_t.T.astype(BF16)


def _swa_band_tables():
    B = SWA_BLOCK
    qi = np.arange(2 * B)[None, :] % B
    kj = np.arange(3 * B)[:, None]
    band = np.abs(kj - B - qi) <= B
    tabs = [band, band & (kj >= B), band & (kj < 2 * B)]
    return jnp.asarray(np.where(np.stack(tabs), 0.0, NEG_BIG), F32)


def _swa_attention(qk, vs_t, l, band, sink_row):
    L = qk.shape[0]
    B = SWA_BLOCK
    T = SWA_QB * B
    nstep = L // T
    nblk = L // B
    prev = lambda i: jnp.maximum(i * SWA_QB - 1, 0)
    nxt = lambda i: jnp.minimum((i + 1) * SWA_QB, nblk - 1)
    return pl.pallas_call(
        functools.partial(_swa_kernel, nstep=nstep),
        grid=(nstep,),
        in_specs=[pl.BlockSpec((T, 512), lambda i: (i, 1)),
                  pl.BlockSpec((B, LANES), lambda i: (prev(i), 8)),
                  pl.BlockSpec((T, LANES), lambda i: (i, 8)),
                  pl.BlockSpec((B, LANES), lambda i: (nxt(i), 8)),
                  pl.BlockSpec((1, LANES, B), lambda i: (prev(i), 0, 0)),
                  pl.BlockSpec((SWA_QB, LANES, B), lambda i: (i, 0, 0)),
                  pl.BlockSpec((1, LANES, B), lambda i: (nxt(i), 0, 0)),
                  _const_spec(band), _layer_spec(sink_row, l)],
        out_specs=pl.BlockSpec((T, 512), lambda i: (i, 0)),
        out_shape=jax.ShapeDtypeStruct((L, 512), BF16),
        compiler_params=_cparams(("parallel",)),
        name="swa_attn",
    )(qk, qk, qk, qk, vs_t, vs_t, vs_t, band, sink_row)


def _bmm(a, b):
    return jnp.einsum('bij,bjk->bik', a.astype(BF16), b.astype(BF16), preferred_element_type=F32)


def _bmm_nt(a, b):
    return jnp.einsum('bid,bjd->bij', a.astype(BF16), b.astype(BF16), preferred_element_type=F32)


def _gdn_chunk_terms(qkv, gb, gbt, rev, out):
    G, C, H = GDN_G, GDN_CHUNK, GDN_HEADS

    def heads(part):
        return qkv[part * H:(part + 1) * H].reshape(H * G, C, HEAD_DIM)

    q = heads(0)
    k = heads(1)
    v = heads(2).astype(F32)
    kf = k.astype(F32)
    qf = q.astype(F32)
    d0 = H if rev else 0

    def colv(base):
        return jnp.concatenate([gb[:, base + h:base + h + 1].reshape(G, C, 1) for h in range(H)], axis=0)

    beta = jnp.broadcast_to(colv(d0), (H * G, C, HEAD_DIM))
    gcc = jnp.broadcast_to(colv(8 + d0), (H * G, C, HEAD_DIM))
    gl = jnp.broadcast_to(colv(16 + d0), (H * G, C, HEAD_DIM))
    grow = jnp.concatenate([gbt[8 + d0 + h:9 + d0 + h, c * C:(c + 1) * C].reshape(1, 1, C)
                            for h in range(H) for c in range(G)], axis=0)
    ii = lax.broadcasted_iota(jnp.int32, (1, C, C), 1)
    jj = lax.broadcasted_iota(jnp.int32, (1, C, C), 2)
    incl = (jj >= ii) if rev else (jj <= ii)
    strict = (jj > ii) if rev else (jj < ii)
    decay = jnp.exp(jnp.where(incl, gcc - grow, NEG_BIG))
    kk = _bmm_nt(k, k)
    qk = _bmm_nt(q, k)
    yield
    nmat = jnp.where(strict, kk * decay, 0.0) * beta
    eg = jnp.exp(gcc)
    rhs = jnp.concatenate([v * beta, kf * (beta * eg)], axis=-1)
    m = _bmm(nmat, jnp.concatenate([rhs, nmat], axis=-1))
    yield
    x = rhs - m[:, :, 0:2 * HEAD_DIM]
    p = m[:, :, 2 * HEAD_DIM:]
    for _ in range(4):
        m = _bmm(p, jnp.concatenate([x, p], axis=-1))
        yield
        x = x + m[:, :, 0:2 * HEAD_DIM]
        p = m[:, :, 2 * HEAD_DIM:]
    x = x + _bmm(p, x)
    yield
    u = x[:, :, 0:HEAD_DIM]
    w = x[:, :, HEAD_DIM:2 * HEAD_DIM]
    qkm = jnp.where(incl, qk * decay, 0.0)
    k_tail_t = jnp.swapaxes(kf * jnp.exp(gl - gcc), 1, 2)
    lhs1 = jnp.concatenate([w, qf * eg], axis=1).astype(BF16)
    lhs2 = jnp.concatenate([qkm, k_tail_t], axis=1).astype(BF16)
    dch = jnp.exp(gl[:, 0:8, :])
    out.extend([u, lhs1, lhs2, dch])


def _gdn_scan_group(u_ref, l1_ref, l2_ref, d_ref, s_ref, o_ref, rev):
    G, C, H = GDN_G, GDN_CHUNK, GDN_HEADS
    S = s_ref[...]
    for c in (range(G - 1, -1, -1) if rev else range(G)):
        r1 = _bmm(l1_ref[:, c], S)
        yield
        vn = u_ref[:, c] - r1[:, 0:C]
        r2 = _bmm(l2_ref[:, c], vn)
        yield
        oc = r1[:, C:2 * C] + r2[:, 0:C]
        S = S * d_ref[:, c][:, 0:1, :] + r2[:, C:2 * C]
        o_ref[c * C:(c + 1) * C, :] = jnp.concatenate([oc[h] for h in range(H)], axis=-1).astype(BF16)
    s_ref[...] = S


def _gdn_kernel(qkvf_ref, gbf_ref, gbtf_ref, qkvb_ref, gbb_ref, gbtb_ref, of_ref, ob_ref,
                sf_ref, sb_ref, uf_ref, l1f_ref, l2f_ref, df_ref, ub_ref, l1b_ref, l2b_ref, db_ref):
    G, C, H = GDN_G, GDN_CHUNK, GDN_HEADS
    step = pl.program_id(0)

    @pl.when(step == 0)
    def _():
        for r in (sf_ref, sb_ref, uf_ref, l1f_ref, l2f_ref, df_ref, ub_ref, l1b_ref, l2b_ref, db_ref):
            r[...] = jnp.zeros_like(r)

    terms_f, terms_b = [], []
    strands = [_gdn_scan_group(uf_ref, l1f_ref, l2f_ref, df_ref, sf_ref, of_ref, False),
               _gdn_scan_group(ub_ref, l1b_ref, l2b_ref, db_ref, sb_ref, ob_ref, True),
               _gdn_chunk_terms(qkvf_ref[...], gbf_ref[...], gbtf_ref[...], False, terms_f),
               _gdn_chunk_terms(qkvb_ref[...], gbb_ref[...], gbtb_ref[...], True, terms_b)]
    while strands:
        strands = [s for s in strands if next(s, True) is None]
    for (u, l1, l2, dch), (u_ref, l1_ref, l2_ref, d_ref) in ((terms_f, (uf_ref, l1f_ref, l2f_ref, df_ref)),
                                                          (terms_b, (ub_ref, l1b_ref, l2b_ref, db_ref))):
        u_ref[...] = u.reshape(H, G, C, HEAD_DIM)
        l1_ref[...] = l1.reshape(H, G, 2 * C, HEAD_DIM)
        l2_ref[...] = l2.reshape(H, G, 2 * C, C)
        d_ref[...] = dch.reshape(H, G, 8, HEAD_DIM)


def _gdn(qkv, gb, gbt):
    L = qkv.shape[1]
    G, C, H = GDN_G, GDN_CHUNK, GDN_HEADS
    T = G * C
    n = L // T
    fin = lambda s: jnp.minimum(s, n - 1)
    bin_ = lambda s: jnp.maximum(n - 1 - s, 0)
    fout = lambda s: jnp.maximum(s - 1, 0)
    bout = lambda s: jnp.minimum(n - s, n - 1)
    term_scratch = [pltpu.VMEM((H, G, C, HEAD_DIM), F32), pltpu.VMEM((H, G, 2 * C, HEAD_DIM), BF16),
                    pltpu.VMEM((H, G, 2 * C, C), BF16), pltpu.VMEM((H, G, 8, HEAD_DIM), F32)]
    return pl.pallas_call(
        _gdn_kernel,
        grid=(n + 1,),
        in_specs=[pl.BlockSpec((3 * H, T, HEAD_DIM), lambda s: (0, fin(s), 0)),
                  pl.BlockSpec((T, LANES), lambda s: (fin(s), 0)),
                  pl.BlockSpec((24, T), lambda s: (0, fin(s))),
                  pl.BlockSpec((3 * H, T, HEAD_DIM), lambda s: (0, bin_(s), 0)),
                  pl.BlockSpec((T, LANES), lambda s: (bin_(s), 0)),
                  pl.BlockSpec((24, T), lambda s: (0, bin_(s)))],
        out_specs=[pl.BlockSpec((T, 256), lambda s: (fout(s), 0)),
                   pl.BlockSpec((T, 256), lambda s: (bout(s), 0))],
        out_shape=[jax.ShapeDtypeStruct((L, 256), BF16), jax.ShapeDtypeStruct((L, 256), BF16)],
        scratch_shapes=[pltpu.VMEM((H, HEAD_DIM, HEAD_DIM), F32), pltpu.VMEM((H, HEAD_DIM, HEAD_DIM), F32)]
                       + term_scratch + term_scratch,
        compiler_params=_cparams(("arbitrary",)),
        name="gdn_scan",
    )(qkv, gb, gbt, qkv, gb, gbt)


def _merge_kernel(x_ref, yna_ref, yswa_ref, of_ref, ob_ref, z_ref, gate_ref, gn_ref, bd_ref,
                  wna_ref, wswa_ref, wgdn_ref, wout_ref, o_ref):
    o = of_ref[...].astype(F32) + ob_ref[...].astype(F32)
    z = z_ref[...].astype(F32)
    ms = _group_sum(o * o, bd_ref) * (1.0 / HEAD_DIM)
    ygdn = o * lax.rsqrt(ms + NORM_EPS) * gn_ref[...] * (z * _sigmoid(z))
    m = (gate_ref[:, 0:D_MODEL].astype(F32) * _dot(yna_ref[...], wna_ref[...])
         + gate_ref[:, D_MODEL:2 * D_MODEL].astype(F32) * _dot(yswa_ref[...], wswa_ref[...])
         + gate_ref[:, 2 * D_MODEL:3 * D_MODEL].astype(F32) * _dot(ygdn.astype(BF16), wgdn_ref[...]))
    o_ref[...] = x_ref[...] + _dot(m.astype(BF16), wout_ref[...])


def _merge(x, yna, yswa, of, ob, z, gate, l, gn, bd, wna, wswa, wgdn, wout):
    L = x.shape[0]
    row = lambda w_: pl.BlockSpec((TM, w_), lambda i: (i, 0))
    return pl.pallas_call(
        _merge_kernel,
        grid=(L // TM,),
        in_specs=[row(D_MODEL), row(256), row(512), row(256), row(256), row(256), row(SEG_GATE),
                  _layer_spec(gn, l), _const_spec(bd), _layer_spec(wna, l), _layer_spec(wswa, l),
                  _layer_spec(wgdn, l), _layer_spec(wout, l)],
        out_specs=row(D_MODEL),
        out_shape=jax.ShapeDtypeStruct((L, D_MODEL), F32),
        compiler_params=_cparams(("parallel",)),
        name="merge",
    )(x, yna, yswa, of, ob, z, gate, gn, bd, wna, wswa, wgdn, wout)


def _ffn_kernel(x_ref, xp_ref, xn_ref, g_ref, wu_ref, cw_ref, cb_ref, wd_ref, o_ref, hbuf, *, nblk):
    i = pl.program_id(0)

    def normed(xv):
        ms = jnp.mean(xv * xv, axis=-1, keepdims=True)
        return xv * lax.rsqrt(ms + NORM_EPS) * g_ref[...]

    hbuf[0:HALO, :] = jnp.where(i > 0, normed(xp_ref[...]), 0.0).astype(BF16)
    hbuf[HALO:HALO + TM, :] = normed(x_ref[...]).astype(BF16)
    hbuf[HALO + TM:, :] = jnp.where(i < nblk - 1, normed(xn_ref[...]), 0.0).astype(BF16)
    h = hbuf[...]
    n = TM + 2 * HALO
    chunks = [slice(a, b) for a, b in zip(FFN_SPLIT[:-1], FFN_SPLIT[1:])]
    gate_cols = [slice(D_FF + a, D_FF + b) for a, b in zip(FFN_SPLIT[:-1], FFN_SPLIT[1:])]

    def conv(u, cols):
        y = (pltpu.roll(u, 1, 0) * cw_ref[0:1, cols] + u * cw_ref[1:2, cols]
             + pltpu.roll(u, n - 1, 0) * cw_ref[2:3, cols])
        return y[HALO:HALO + TM] + cb_ref[:, cols]

    ups = [(_dot(h, wu_ref[:, ca]), _dot(h, wu_ref[:, cb])) for ca, cb in zip(chunks, gate_cols)]
    acc = x_ref[...]
    for (ua, ub), ca, cb in zip(ups, chunks, gate_cols):
        a = conv(ua, ca)
        gated = (a * _sigmoid(a) * conv(ub, cb)).astype(BF16)
        acc = acc + _dot(gated, wd_ref[ca, :])
    o_ref[...] = acc


def _ffn(x, l, g, w_up, conv_w, conv_b, w_down):
    L = x.shape[0]
    nblk = L // TM
    hb = TM // HALO
    once = pl.Buffered(1)
    return pl.pallas_call(
        functools.partial(_ffn_kernel, nblk=nblk),
        grid=(nblk,),
        in_specs=[pl.BlockSpec((TM, D_MODEL), lambda i: (i, 0)),
                  pl.BlockSpec((HALO, D_MODEL), lambda i: (jnp.maximum(i * hb - 1, 0), 0)),
                  pl.BlockSpec((HALO, D_MODEL), lambda i: (jnp.minimum((i + 1) * hb, L // HALO - 1), 0)),
                  _layer_spec(g, l), _layer_spec(w_up, l, pipeline_mode=once), _layer_spec(conv_w, l),
                  _layer_spec(conv_b, l), _layer_spec(w_down, l, pipeline_mode=once)],
        out_specs=pl.BlockSpec((TM, D_MODEL), lambda i: (i, 0)),
        out_shape=jax.ShapeDtypeStruct((L, D_MODEL), F32),
        scratch_shapes=[pltpu.VMEM((TM + 2 * HALO, D_MODEL), BF16)],
        compiler_params=_cparams(("parallel",)),
        name="ffn",
    )(x, x, x, g, w_up, conv_w, conv_b, w_down)


PACK_SEGMENTS = ((0, 256, 0), (256, 512, 256), (768, 1280, 512), (1280, 1408, 1024),
                 (1536, 2304, 1152), (512, 768, 1920), (2304, 2560, 2176), (1408, 1536, 2432),
                 (2576, 5648, 2560), (2560, 2576, 5632))
PACK_K = 256
PACK_PIECE = 512


def _pack_kernel(w_ref, o_ref):
    for s0, s1, d0 in PACK_SEGMENTS[:-1]:
        for p0 in range(s0, s1, PACK_PIECE):
            p1 = min(p0 + PACK_PIECE, s1)
            o_ref[:, d0 + p0 - s0:d0 + p1 - s0] = w_ref[p0:p1, :].T.astype(BF16)
    s0, s1, d0 = PACK_SEGMENTS[-1]
    lane = lax.broadcasted_iota(jnp.int32, (1, LANES), 1)
    o_ref[:, d0:d0 + LANES] = jnp.where(lane < s1 - s0, w_ref[s0:s0 + LANES, :].T, 0.0).astype(BF16)


def _pack_w_in(w):
    depth, k, n = w.shape
    return pl.pallas_call(
        _pack_kernel,
        grid=(depth, k // PACK_K),
        in_specs=[pl.BlockSpec((None, n, PACK_K), lambda l, i: (l, 0, i))],
        out_specs=pl.BlockSpec((None, PACK_K, IN_PACKED), lambda l, i: (l, i, 0)),
        out_shape=jax.ShapeDtypeStruct((depth, k, IN_PACKED), BF16),
        compiler_params=_cparams(("parallel", "parallel")),
        name="pack_w_in",
    )(jnp.swapaxes(w, 1, 2))


def _rope_tables(L):
    inv = 1.0 / (ROPE_THETA ** (np.arange(0, HEAD_DIM, 2, dtype=np.float64) / HEAD_DIM))
    freq = np.tile(inv, LANES // inv.size)
    sign = np.where(np.arange(LANES) % HEAD_DIM < HEAD_DIM // 2, -1.0, 1.0)
    start = np.arange(L // TM, dtype=np.float64)[:, None] * TM * freq[None, :]
    off = np.arange(TM, dtype=np.float64)[:, None] * freq[None, :]
    blk = np.stack([np.cos(start), np.sin(start)], axis=1)
    offs = np.stack([np.cos(off), np.sin(off), sign * np.cos(off), sign * np.sin(off)])
    return jnp.asarray(blk, F32), jnp.asarray(offs, F32)


def kernel(x, attn_norm, w_in, qk_norm, na_rpb, swa_sink, gdn_conv_w, gdn_a_log, gdn_dt_bias, gdn_norm,
           w_branch_na, w_branch_swa, w_branch_gdn, w_out, ffn_norm, w_up, ffn_conv_w, ffn_conv_b, w_down):
    B, L, D = x.shape
    assert B == 1 and D == D_MODEL and L % (NA_ROWS * GRID_W) == 0 and L // GRID_W >= 2 * NA_ROWS
    depth = w_in.shape[0]
    rope_blk, rope_off = _rope_tables(L)
    blockdiag = jnp.asarray(np.kron(np.eye(4), np.ones((HEAD_DIM, HEAD_DIM))), BF16)
    bias_tabs = _na_bias_tables(na_rpb, L // GRID_W)
    swa_band = _swa_band_tables()
    scale = HEAD_DIM ** -0.5 * LOG2E
    G = SWA_Q_HEADS // SWA_KV_HEADS
    gain = jnp.concatenate([jnp.tile(qk_norm[:, 0] * scale, (1, NA_HEADS)), jnp.tile(qk_norm[:, 1], (1, NA_HEADS)),
                            jnp.tile(qk_norm[:, 2] * scale, (1, SWA_Q_HEADS)),
                            jnp.tile(qk_norm[:, 3], (1, SWA_KV_HEADS))], axis=1)[:, None, :]
    sink_row = jnp.repeat(swa_sink.reshape(depth, SWA_Q_HEADS // 2, 2) * LOG2E, SWA_BLOCK, axis=2)[:, :, None, :]
    alog_row = jnp.pad(gdn_a_log.reshape(depth, 1, 8), ((0, 0), (0, 0), (8, LANES - 16)))
    dtb_row = jnp.pad(gdn_dt_bias.reshape(depth, 1, 8), ((0, 0), (0, 0), (8, LANES - 16)))
    gdn_gain = jnp.tile(gdn_norm, (1, GDN_HEADS))[:, None, :]
    attn_g = attn_norm[:, None, :]
    ffn_g = ffn_norm[:, None, :]
    ffn_b = ffn_conv_b[:, None, :]
    w_in_p = _pack_w_in(w_in)
    w_na, w_swa, w_gdn = w_branch_na.astype(BF16), w_branch_swa.astype(BF16), w_branch_gdn.astype(BF16)
    w_o, w_u, w_d = w_out.astype(BF16), w_up.astype(BF16), w_down.astype(BF16)
    xs = x[0]
    for l in range(depth):
        qk, z, gate, va_t, vs_t, qkv_n, gb, gbt = _inproj(xs, l, attn_g, w_in_p, gain, rope_blk, rope_off, blockdiag,
                                                          gdn_conv_w, alog_row, dtb_row)
        y_na = _na_attention(qk, va_t, l, bias_tabs)
        y_swa = _swa_attention(qk, vs_t, l, swa_band, sink_row)
        o_f, o_b = _gdn(qkv_n, gb, gbt)
        xs = _merge(xs, y_na, y_swa, o_f, o_b, z, gate, l, gdn_gain, blockdiag, w_na, w_swa, w_gdn, w_o)
        xs = _ffn(xs, l, ffn_g, w_u, ffn_conv_w, ffn_b, w_d)
    return xs[None]
```

```python
import functools

import numpy as np
import jax
import jax.numpy as jnp
from jax import lax
from jax.experimental import pallas as pl
from jax.experimental.pallas import tpu as pltpu

F32 = jnp.float32
BF16 = jnp.bfloat16

D_MODEL = 1024
HEAD_DIM = 64
GRID_W = 64
NORM_EPS = 1e-6
NA_HEADS = 4
NA_KH = 8
NA_KW = 16
SWA_Q_HEADS = 8
SWA_KV_HEADS = 2
SWA_BLOCK = 128
ROPE_THETA = 10000.0
GDN_HEADS = 4
GDN_CHUNK = 64
D_FF = 2816

LANES = 128
NEG_BIG = -1e30
LOG2E = 1.4426950408889634
VMEM_LIMIT = 56 * 1024 * 1024

SEG_QK = 1152
SEG_PLAIN = 1408
PLAIN_VA = 768
PLAIN_Z = 1024
PLAIN_VS = 1280
SEG_GATE = 3072
SEG_BA = 128
IN_PACKED = SEG_QK + SEG_PLAIN + SEG_GATE + SEG_BA

TM = 512
NA_ROWS = 16
NA_WIN = NA_KH + 2
SWA_QB = 8
GDN_G = 4
GDN_SUB = 8
MXU_TILE = 256
FFN_SPLIT = (0, 6 * MXU_TILE, D_FF)
HALO = 8


def _cparams(sem):
    return pltpu.CompilerParams(dimension_semantics=sem, vmem_limit_bytes=VMEM_LIMIT)


def _dot(a, b):
    return jnp.dot(a, b, preferred_element_type=F32)


def _sigmoid(x):
    return 1.0 / (1.0 + jnp.exp(-x))


def _group_sum(sq, bd_ref):
    w = sq.shape[-1]
    return _dot(sq.astype(BF16), bd_ref[0:w, 0:w])


def _gdn_token_qkv(x, row_prev, row_next, cw_ref, bd_ref, qkv_ref):
    rid = lax.broadcasted_iota(jnp.int32, (TM, 1), 0)
    xprev = jnp.where(rid == 0, row_prev, pltpu.roll(x, 1, 0))
    xnext = jnp.where(rid == TM - 1, row_next, pltpu.roll(x, TM - 1, 0))
    y = xprev * cw_ref[0:1, :] + x * cw_ref[1:2, :] + xnext * cw_ref[2:3, :]
    y = y * _sigmoid(y)
    for part in range(3):
        yp = y[:, part * 256:(part + 1) * 256]
        if part < 2:
            yp = yp * lax.rsqrt(_group_sum(yp * yp, bd_ref) + NORM_EPS)
        if part == 0:
            yp = yp * (HEAD_DIM ** -0.5)
        yp = yp.astype(BF16)
        for hh in range(GDN_HEADS):
            qkv_ref[part * GDN_HEADS + hh] = yp[:, hh * HEAD_DIM:(hh + 1) * HEAD_DIM]


def _gdn_token_gates(ba, alog_ref, dtb_ref, gb_ref, gbt_ref):
    rid = lax.broadcasted_iota(jnp.int32, (TM, 1), 0)
    lane = lax.broadcasted_iota(jnp.int32, (1, LANES), 1)
    beta = _sigmoid(ba)
    sp_in = ba + dtb_ref[...]
    softplus = jnp.maximum(sp_in, 0.0) + jnp.log(1.0 + jnp.exp(-jnp.abs(sp_in)))
    g = jnp.where((lane >= 8) & (lane < 16), -jnp.exp(alog_ref[...]) * softplus, 0.0)
    rc = rid % GDN_CHUNK
    pre = g
    suf = g
    s = 1
    while s < GDN_CHUNK:
        pre = pre + jnp.where(rc >= s, pltpu.roll(pre, s, 0), 0.0)
        suf = suf + jnp.where(rc < GDN_CHUNK - s, pltpu.roll(suf, TM - s, 0), 0.0)
        s *= 2
    tot = pre + suf - g
    gc = jnp.where(lane < 12, pre, suf)
    slab = jnp.where(lane < 8, beta, jnp.where(lane < 16, gc, jnp.where(lane < 24, pltpu.roll(tot, 8, 1), 0.0)))
    gb_ref[...] = slab
    gbt_ref[...] = slab.T[0:24, :]


def _inproj_kernel(x_ref, xp_ref, xn_ref, g_ref, w_ref, gain_ref, rope_blk_ref, rope_off_ref, bd_ref,
                   cw_ref, alog_ref, dtb_ref,
                   oqk_ref, oz_ref, ogate_ref, ovat_ref, ovst_ref, oqkvh_ref, ogb_ref, ogbt_ref, *, nblk):
    i = pl.program_id(0)

    def normed(xv):
        ms = jnp.mean(xv * xv, axis=-1, keepdims=True)
        return xv * lax.rsqrt(ms + NORM_EPS) * g_ref[...]

    h = normed(x_ref[...]).astype(BF16)
    h_halo = jnp.concatenate([jnp.where(i > 0, normed(xp_ref[...]), 0.0),
                              jnp.where(i < nblk - 1, normed(xn_ref[...]), 0.0)], axis=0).astype(BF16)
    cw = 2 * LANES
    lane = lax.broadcasted_iota(jnp.int32, (1, cw), 1)
    first_half = (lane % HEAD_DIM) < (HEAD_DIM // 2)
    c_a, s_a = rope_blk_ref[0:1, :], rope_blk_ref[1:2, :]
    cos128 = c_a * rope_off_ref[0] - s_a * rope_off_ref[1]
    sin128 = s_a * rope_off_ref[2] + c_a * rope_off_ref[3]
    cos = jnp.concatenate([cos128, cos128], axis=1)
    sin = jnp.concatenate([sin128, sin128], axis=1)

    def head_norm(t, c):
        cols = slice(c * cw, min((c + 1) * cw, SEG_QK))
        w = cols.stop - cols.start
        tc = t[:, cols]
        ss = _group_sum(tc * tc, bd_ref)
        y = tc * lax.rsqrt(ss * (1.0 / HEAD_DIM) + NORM_EPS) * gain_ref[:, cols]
        if cols.start >= 512:
            rot = jnp.where(first_half[:, 0:w], pltpu.roll(y, w - HEAD_DIM // 2, 1), pltpu.roll(y, HEAD_DIM // 2, 1))
            y = y * cos[:, 0:w] + rot * sin[:, 0:w]
        oqk_ref[:, cols] = y.astype(BF16)

    o_plain, o_gate, o_ba = SEG_QK, SEG_QK + SEG_PLAIN, SEG_QK + SEG_PLAIN + SEG_GATE
    gate_cols = [slice(o_gate + c * D_MODEL, o_gate + (c + 1) * D_MODEL) for c in range(3)]
    out_cols = [slice(c * D_MODEL, (c + 1) * D_MODEL) for c in range(3)]
    t = _dot(h, w_ref[:, 0:SEG_QK])
    t_gdn = _dot(h, w_ref[:, o_plain:o_plain + 768])
    tba = _dot(h, w_ref[:, o_ba:o_ba + SEG_BA])
    t_halo = _dot(h_halo, w_ref[:, o_plain:o_plain + 768])
    for c in (0, 1):
        head_norm(t, c)
    tg0 = _dot(h, w_ref[:, gate_cols[0]])
    for c in (2, 3, 4):
        head_norm(t, c)
    tg1 = _dot(h, w_ref[:, gate_cols[1]])
    ogate_ref[:, out_cols[0]] = _sigmoid(tg0).astype(BF16)
    _gdn_token_qkv(t_gdn, t_halo[HALO - 1:HALO, :], t_halo[HALO:HALO + 1, :], cw_ref, bd_ref, oqkvh_ref)
    tg2 = _dot(h, w_ref[:, gate_cols[2]])
    ogate_ref[:, out_cols[1]] = _sigmoid(tg1).astype(BF16)
    _gdn_token_gates(tba, alog_ref, dtb_ref, ogb_ref, ogbt_ref)
    t_rest = _dot(h, w_ref[:, o_plain + PLAIN_VA:o_plain + SEG_PLAIN])
    ogate_ref[:, out_cols[2]] = _sigmoid(tg2).astype(BF16)
    oz_ref[...] = t_rest[:, PLAIN_Z - PLAIN_VA:PLAIN_Z - PLAIN_VA + 256].astype(BF16)
    for grp in range(TM // LANES):
        rows = slice(grp * LANES, (grp + 1) * LANES)
        ovat_ref[grp] = t_rest[rows, 0:256].T.astype(BF16)
        ovst_ref[grp] = t_rest[rows, PLAIN_VS - PLAIN_VA:PLAIN_VS - PLAIN_VA + 128].T.astype(BF16)


def _layer_spec(a, l, **kw):
    nd = a.ndim - 1
    return pl.BlockSpec((None,) + a.shape[1:], lambda *_: (l,) + (0,) * nd, **kw)


def _const_spec(a):
    return pl.BlockSpec(a.shape, lambda *_: (0,) * a.ndim)


def _inproj(x, l, g, w, gain, rope_blk, rope_off, bd, conv_w, alog_row, dtb_row):
    L = x.shape[0]
    nblk = L // TM
    hb = TM // HALO
    row = lambda w_: pl.BlockSpec((TM, w_), lambda i: (i, 0))
    return pl.pallas_call(
        functools.partial(_inproj_kernel, nblk=nblk),
        grid=(nblk,),
        in_specs=[row(D_MODEL),
                  pl.BlockSpec((HALO, D_MODEL), lambda i: (jnp.maximum(i * hb - 1, 0), 0)),
                  pl.BlockSpec((HALO, D_MODEL), lambda i: (jnp.minimum((i + 1) * hb, L // HALO - 1), 0)),
                  _layer_spec(g, l), _layer_spec(w, l, pipeline_mode=pl.Buffered(1)),
                  _layer_spec(gain, l), pl.BlockSpec((None, 2, LANES), lambda i: (i, 0, 0)), _const_spec(rope_off),
                  _const_spec(bd), _layer_spec(conv_w, l), _layer_spec(alog_row, l), _layer_spec(dtb_row, l)],
        out_specs=[row(SEG_QK), row(256), row(SEG_GATE),
                   pl.BlockSpec((TM // LANES, 256, LANES), lambda i: (i, 0, 0)),
                   pl.BlockSpec((TM // LANES, 128, LANES), lambda i: (i, 0, 0)),
                   pl.BlockSpec((3 * GDN_HEADS, TM, HEAD_DIM), lambda i: (0, i, 0)),
                   row(LANES),
                   pl.BlockSpec((24, TM), lambda i: (0, i))],
        out_shape=[jax.ShapeDtypeStruct((L, SEG_QK), BF16),
                   jax.ShapeDtypeStruct((L, 256), BF16),
                   jax.ShapeDtypeStruct((L, SEG_GATE), BF16),
                   jax.ShapeDtypeStruct((L // LANES, 256, LANES), BF16),
                   jax.ShapeDtypeStruct((L // LANES, 128, LANES), BF16),
                   jax.ShapeDtypeStruct((3 * GDN_HEADS, L, HEAD_DIM), BF16),
                   jax.ShapeDtypeStruct((L, LANES), F32),
                   jax.ShapeDtypeStruct((24, L), F32)],
        compiler_params=_cparams(("parallel",)),
        name="inproj",
    )(x, x, x, g, w, gain, rope_blk, rope_off, bd, conv_w, alog_row, dtb_row)


def _toeplitz_kernel(rpb_ref, onehot_ref, colmask_ref, o_ref):
    r = rpb_ref[...]
    hi = r.astype(BF16)
    r1 = r - hi.astype(F32)
    mid = r1.astype(BF16)
    lo = (r1 - mid.astype(F32)).astype(BF16)
    oh = onehot_ref[...]
    w = _dot(hi, oh) + _dot(mid, oh) + _dot(lo, oh)
    o_ref[...] = jnp.where(colmask_ref[...] > 0.0, w * LOG2E, NEG_BIG)


def _na_assemble_kernel(tiles_ref, o_ref, *, a_idx):
    neg = jnp.full((GRID_W, GRID_W), NEG_BIG, F32)
    for t in range(a_idx.shape[0]):
        for j in range(NA_WIN):
            for e in range(2):
                a = int(a_idx[t, j, e])
                o_ref[t, j * GRID_W:(j + 1) * GRID_W, e * GRID_W:(e + 1) * GRID_W] = neg if a < 0 else tiles_ref[a]


def _na_bias_tables(na_rpb, rows):
    depth, H, nr, nc = na_rpb.shape
    kc = np.arange(GRID_W)[:, None]
    qc = np.arange(GRID_W)[None, :]
    dc = np.clip(kc - qc + (NA_KW - 1), 0, 2 * NA_KW - 2).reshape(-1)
    onehot = np.zeros((32, GRID_W * GRID_W), np.float32)
    onehot[dc, np.arange(GRID_W * GRID_W)] = 1.0
    col_start = np.clip(qc - NA_KW // 2, 0, GRID_W - NA_KW)
    colmask = ((kc >= col_start) & (kc < col_start + NA_KW)).astype(np.float32).reshape(1, -1)
    rpb_rows = jnp.pad(na_rpb, ((0, 0), (0, 0), (0, 16 - nr), (0, 32 - nc))).reshape(depth * H * 16, 32)
    tiles = pl.pallas_call(
        _toeplitz_kernel,
        out_shape=jax.ShapeDtypeStruct((depth * H * 16, GRID_W * GRID_W), F32),
        name="na_bias_tiles",
    )(rpb_rows, jnp.asarray(onehot, BF16), jnp.asarray(colmask))
    tiles = tiles.reshape(depth * H, 16, GRID_W, GRID_W)
    a_idx = np.stack([_na_pair_structure(r0, rows)[1] for r0 in _na_pair_type_rows(rows)])
    nt = a_idx.shape[0]
    tabs = pl.pallas_call(
        functools.partial(_na_assemble_kernel, a_idx=a_idx),
        grid=(depth * H,),
        in_specs=[pl.BlockSpec((None, 16, GRID_W, GRID_W), lambda i: (i, 0, 0, 0))],
        out_specs=pl.BlockSpec((None, nt, NA_WIN * GRID_W, 2 * GRID_W), lambda i: (i, 0, 0, 0)),
        out_shape=jax.ShapeDtypeStruct((depth * H, nt, NA_WIN * GRID_W, 2 * GRID_W), F32),
        compiler_params=_cparams(("parallel",)),
        name="na_bias_tables",
    )(tiles)
    return tabs.reshape(depth, H, nt, NA_WIN * GRID_W, 2 * GRID_W)


def _na_pair_structure(r0, rows):
    wstart = int(np.clip(r0 - NA_KH // 2, 0, rows - NA_KH - 1)) // 2 * 2
    a_idx = np.full((NA_WIN, 2), -1, np.int64)
    for e in range(2):
        rr = r0 + e
        rs = int(np.clip(rr - NA_KH // 2, 0, rows - NA_KH))
        for j in range(NA_WIN):
            krow = wstart + j
            if rs <= krow < rs + NA_KH:
                a_idx[j, e] = krow - rr + NA_KH - 1
    return wstart, a_idx


def _na_pair_type_rows(rows):
    reps = [0, 2, 4, rows - 4, rows - 2]
    for r0 in range(0, rows, 2):
        t = 3 + (r0 - (rows - 4)) // 2 if r0 >= rows - 4 else min(r0 // 2, 2)
        ws, a = _na_pair_structure(r0, rows)
        ws_t, a_t = _na_pair_structure(reps[t], rows)
        assert (a == a_t).all() and r0 - ws == reps[t] - ws_t
    return reps


def _na_kernel(q_ref, kp_ref, kc_ref, kn_ref, vp_ref, vc_ref, vn_ref, bias_ref, o_ref, kbuf, vbuf, *, rows):
    i = pl.program_id(0)
    blk = NA_ROWS * GRID_W
    grp = blk // LANES
    kbuf[0:blk, :] = kp_ref[...]
    kbuf[blk:2 * blk, :] = kc_ref[...]
    kbuf[2 * blk:3 * blk, :] = kn_ref[...]
    vbuf[0:grp] = vp_ref[...]
    vbuf[grp:2 * grp] = vc_ref[...]
    vbuf[2 * grp:3 * grp] = vn_ref[...]
    win = NA_WIN * GRID_W
    wgrp = win // LANES
    pair = 2 * GRID_W

    npair = NA_ROWS // 2
    goffs, types = [], []
    for pp in range(npair):
        r0 = i * NA_ROWS + 2 * pp
        wstart = jnp.clip(r0 - NA_KH // 2, 0, rows - NA_KH - 1) // 2 * 2
        types.append(jnp.where(r0 >= rows - 4, 3 + (r0 - (rows - 4)) // 2, jnp.minimum(r0 // 2, 2)))
        goffs.append((wstart - (i - 1) * NA_ROWS) // 2)

    def logits(pp, h):
        sl = slice(h * HEAD_DIM, (h + 1) * HEAD_DIM)
        kw = kbuf[pl.ds(pl.multiple_of(goffs[pp] * LANES, LANES), win), sl]
        s = lax.dot_general(kw, q_ref[pp * pair:(pp + 1) * pair, sl],
                            (((1,), (1,)), ((), ())), preferred_element_type=F32)
        return s + bias_ref[h, types[pp]]

    def softmax(s):
        m = jnp.max(s, axis=0, keepdims=True)
        p = jnp.exp2(s - m)
        return p.astype(BF16), 1.0 / jnp.sum(p, axis=0, keepdims=True)

    def weighted(p_inv, pp, h):
        p, inv_l = p_inv
        vt = jnp.concatenate([vbuf[goffs[pp] + j, h * HEAD_DIM:(h + 1) * HEAD_DIM, :] for j in range(wgrp)], axis=-1)
        return _dot(vt, p) * inv_l

    tiles = [(pp, h) for pp in range(npair) for h in range(NA_HEADS)]
    s_q, p_q, outs = {}, {}, {}
    lead, lag = 2, 1
    for n in range(len(tiles) + lead + lag):
        if n < len(tiles):
            s_q[n] = logits(*tiles[n])
        if lead <= n < len(tiles) + lead:
            p_q[n - lead] = softmax(s_q.pop(n - lead))
        if n >= lead + lag:
            pp, h = tiles[n - lead - lag]
            outs[(pp, h)] = weighted(p_q.pop(n - lead - lag), pp, h)
            if h == NA_HEADS - 1:
                o_t = jnp.concatenate([outs.pop((pp, hh)) for hh in range(NA_HEADS)], axis=0)
                o_ref[pp * pair:(pp + 1) * pair, :] = o_t.T.astype(BF16)


def _na_attention(qk, va_t, l, bias):
    L = qk.shape[0]
    rows = L // GRID_W
    nblk = rows // NA_ROWS
    blk = NA_ROWS * GRID_W
    grp = blk // LANES
    w = NA_HEADS * HEAD_DIM
    prev = lambda i: jnp.maximum(i - 1, 0)
    nxt = lambda i: jnp.minimum(i + 1, nblk - 1)
    spec = lambda f, c: pl.BlockSpec((blk, w), lambda i: (f(i), c))
    vspec = lambda f: pl.BlockSpec((grp, w, LANES), lambda i: (f(i), 0, 0))
    same = lambda i: i
    return pl.pallas_call(
        functools.partial(_na_kernel, rows=rows),
        grid=(nblk,),
        in_specs=[spec(same, 0),
                  spec(prev, 1), spec(same, 1), spec(nxt, 1),
                  vspec(prev), vspec(same), vspec(nxt),
                  _layer_spec(bias, l)],
        out_specs=pl.BlockSpec((blk, w), lambda i: (i, 0)),
        out_shape=jax.ShapeDtypeStruct((L, w), BF16),
        scratch_shapes=[pltpu.VMEM((3 * blk, w), BF16), pltpu.VMEM((3 * grp, w, LANES), BF16)],
        compiler_params=_cparams(("parallel",)),
        name="na_attn",
    )(qk, qk, qk, qk, va_t, va_t, va_t, bias)


def _swa_kernel(q_ref, kp_ref, kc_ref, kn_ref, vp_ref, vc_ref, vn_ref, band_ref, sink_ref, o_ref, *, nstep):
    i = pl.program_id(0)
    B = SWA_BLOCK
    G = SWA_Q_HEADS // SWA_KV_HEADS
    k_all = jnp.concatenate([kp_ref[...], kc_ref[...], kn_ref[...]], axis=0)
    vt_all = [vp_ref[0]] + [vc_ref[j] for j in range(SWA_QB)] + [vn_ref[0]]

    def band_index(b):
        if b == 0:
            return jnp.where(i == 0, 1, 0)
        if b == SWA_QB - 1:
            return jnp.where(i == nstep - 1, 2, 0)
        return 0

    def logits(b, hp):
        g = hp // (G // 2)
        ks = k_all[b * B:(b + 3) * B, g * HEAD_DIM:(g + 1) * HEAD_DIM]
        qs = jnp.concatenate([q_ref[b * B:(b + 1) * B, (2 * hp + e) * HEAD_DIM:(2 * hp + e + 1) * HEAD_DIM]
                              for e in range(2)], axis=0)
        s = lax.dot_general(ks, qs, (((1,), (1,)), ((), ())), preferred_element_type=F32)
        t = band_index(b)
        return jnp.concatenate([s[0:B] + band_ref[t, 0:B], s[B:2 * B], s[2 * B:3 * B] + band_ref[t, 2 * B:3 * B]],
                               axis=0)

    def softmax(s, hp):
        sink = sink_ref[hp]
        m = jnp.maximum(jnp.max(s, axis=0, keepdims=True), sink)
        p = jnp.exp2(s - m)
        l = jnp.sum(p, axis=0, keepdims=True) + jnp.exp2(sink - m)
        return p.astype(BF16), 1.0 / l

    def weighted(p_inv, b, hp):
        p, inv_l = p_inv
        g = hp // (G // 2)
        vt = jnp.concatenate([vt_all[b + j][g * HEAD_DIM:(g + 1) * HEAD_DIM, :] for j in range(3)], axis=-1)
        return _dot(vt, p) * inv_l

    tiles = [(b, hp) for b in range(SWA_QB) for hp in range(SWA_Q_HEADS // 2)]
    s_q, p_q = {}, {}
    lead, lag = 2, 1
    for n in range(len(tiles) + lead + lag):
        if n < len(tiles):
            s_q[n] = logits(*tiles[n])
        if lead <= n < len(tiles) + lead:
            p_q[n - lead] = softmax(s_q.pop(n - lead), tiles[n - lead][1])
        if n >= lead + lag:
            b, hp = tiles[n - lead - lag]
            o_t = weighted(p_q.pop(n - lead - lag), b, hp)
            pair_t = jnp.concatenate([o_t[:, 0:B], o_t[:, B:2 * B]], axis=0)
            o_ref[b * B:(b + 1) * B, 2 * hp * HEAD_DIM:(2 * hp + 2) * HEAD_DIM] = pair_t.T.astype(BF16)


def _swa_band_tables():
    B = SWA_BLOCK
    qi = np.arange(2 * B)[None, :] % B
    kj = np.arange(3 * B)[:, None]
    band = np.abs(kj - B - qi) <= B
    tabs = [band, band & (kj >= B), band & (kj < 2 * B)]
    return jnp.asarray(np.where(np.stack(tabs), 0.0, NEG_BIG), F32)


def _swa_attention(qk, vs_t, l, band, sink_row):
    L = qk.shape[0]
    B = SWA_BLOCK
    T = SWA_QB * B
    nstep = L // T
    nblk = L // B
    prev = lambda i: jnp.maximum(i * SWA_QB - 1, 0)
    nxt = lambda i: jnp.minimum((i + 1) * SWA_QB, nblk - 1)
    return pl.pallas_call(
        functools.partial(_swa_kernel, nstep=nstep),
        grid=(nstep,),
        in_specs=[pl.BlockSpec((T, 512), lambda i: (i, 1)),
                  pl.BlockSpec((B, LANES), lambda i: (prev(i), 8)),
                  pl.BlockSpec((T, LANES), lambda i: (i, 8)),
                  pl.BlockSpec((B, LANES), lambda i: (nxt(i), 8)),
                  pl.BlockSpec((1, LANES, B), lambda i: (prev(i), 0, 0)),
                  pl.BlockSpec((SWA_QB, LANES, B), lambda i: (i, 0, 0)),
                  pl.BlockSpec((1, LANES, B), lambda i: (nxt(i), 0, 0)),
                  _const_spec(band), _layer_spec(sink_row, l)],
        out_specs=pl.BlockSpec((T, 512), lambda i: (i, 0)),
        out_shape=jax.ShapeDtypeStruct((L, 512), BF16),
        compiler_params=_cparams(("parallel",)),
        name="swa_attn",
    )(qk, qk, qk, qk, vs_t, vs_t, vs_t, band, sink_row)


def _bmm(a, b):
    return jnp.einsum('bij,bjk->bik', a.astype(BF16), b.astype(BF16), preferred_element_type=F32)


def _bmm_nt(a, b):
    return jnp.einsum('bid,bjd->bij', a.astype(BF16), b.astype(BF16), preferred_element_type=F32)


def _gdn_chunk_terms(qkv, gb, gbt, rev, out):
    G, C, H = GDN_G, GDN_CHUNK, GDN_HEADS

    def heads(part):
        return qkv[part * H:(part + 1) * H].reshape(H * G, C, HEAD_DIM)

    q = heads(0)
    k = heads(1)
    v = heads(2).astype(F32)
    kf = k.astype(F32)
    qf = q.astype(F32)
    d0 = H if rev else 0

    def colv(base):
        return jnp.concatenate([gb[:, base + h:base + h + 1].reshape(G, C, 1) for h in range(H)], axis=0)

    beta = jnp.broadcast_to(colv(d0), (H * G, C, HEAD_DIM))
    gcc = jnp.broadcast_to(colv(8 + d0), (H * G, C, HEAD_DIM))
    gl = jnp.broadcast_to(colv(16 + d0), (H * G, C, HEAD_DIM))
    grow = jnp.concatenate([gbt[8 + d0 + h:9 + d0 + h, c * C:(c + 1) * C].reshape(1, 1, C)
                            for h in range(H) for c in range(G)], axis=0)
    ii = lax.broadcasted_iota(jnp.int32, (1, C, C), 1)
    jj = lax.broadcasted_iota(jnp.int32, (1, C, C), 2)
    incl = (jj >= ii) if rev else (jj <= ii)
    strict = (jj > ii) if rev else (jj < ii)
    decay = jnp.exp(jnp.where(incl, gcc - grow, NEG_BIG))
    kk = _bmm_nt(k, k)
    qk = _bmm_nt(q, k)
    yield
    nmat = jnp.where(strict, kk * decay, 0.0) * beta
    eg = jnp.exp(gcc)
    rhs = jnp.concatenate([v * beta, kf * (beta * eg)], axis=-1)
    same_blk = (ii // GDN_SUB) == (jj // GDN_SUB)
    n_d = jnp.where(same_blk, nmat, 0.0)
    n_off = nmat - n_d
    t = jnp.where(ii == jj, 1.0, 0.0) - n_d
    p = _bmm(n_d, n_d)
    yield
    for _ in range(GDN_SUB.bit_length() - 3):
        m = _bmm(jnp.concatenate([t, p], axis=1), p)
        yield
        t = t + m[:, 0:C]
        p = m[:, C:2 * C]
    t = t + _bmm(t, p)
    yield
    m = _bmm(t, jnp.concatenate([rhs, n_off], axis=-1))
    yield
    y = m[:, :, 0:2 * HEAD_DIM]
    mm = m[:, :, 2 * HEAD_DIM:]
    m = _bmm(mm, jnp.concatenate([y, mm], axis=-1))
    yield
    x = y - m[:, :, 0:2 * HEAD_DIM]
    p = m[:, :, 2 * HEAD_DIM:]
    for _ in range((C // GDN_SUB).bit_length() - 3):
        m = _bmm(p, jnp.concatenate([x, p], axis=-1))
        yield
        x = x + m[:, :, 0:2 * HEAD_DIM]
        p = m[:, :, 2 * HEAD_DIM:]
    x = x + _bmm(p, x)
    yield
    u = x[:, :, 0:HEAD_DIM]
    w = x[:, :, HEAD_DIM:2 * HEAD_DIM]
    qkm = jnp.where(incl, qk * decay, 0.0)
    k_tail_t = jnp.swapaxes(kf * jnp.exp(gl - gcc), 1, 2)
    lhs1 = jnp.concatenate([w, qf * eg], axis=1).astype(BF16)
    lhs2 = jnp.concatenate([qkm, k_tail_t], axis=1).astype(BF16)
    dch = jnp.exp(gl[:, 0:8, :])
    out.extend([u, lhs1, lhs2, dch])


def _gdn_scan_group(u_ref, l1_ref, l2_ref, d_ref, s_ref, o_ref, rev):
    G, C, H = GDN_G, GDN_CHUNK, GDN_HEADS
    S = s_ref[...]
    for c in (range(G - 1, -1, -1) if rev else range(G)):
        r1 = _bmm(l1_ref[:, c], S)
        yield
        vn = u_ref[:, c] - r1[:, 0:C]
        r2 = _bmm(l2_ref[:, c], vn)
        yield
        oc = r1[:, C:2 * C] + r2[:, 0:C]
        S = S * d_ref[:, c][:, 0:1, :] + r2[:, C:2 * C]
        o_ref[c * C:(c + 1) * C, :] = jnp.concatenate([oc[h] for h in range(H)], axis=-1).astype(BF16)
    s_ref[...] = S


def _gdn_kernel(qkvf_ref, gbf_ref, gbtf_ref, qkvb_ref, gbb_ref, gbtb_ref, of_ref, ob_ref,
                sf_ref, sb_ref, uf_ref, l1f_ref, l2f_ref, df_ref, ub_ref, l1b_ref, l2b_ref, db_ref):
    G, C, H = GDN_G, GDN_CHUNK, GDN_HEADS
    step = pl.program_id(0)

    @pl.when(step == 0)
    def _():
        for r in (sf_ref, sb_ref, uf_ref, l1f_ref, l2f_ref, df_ref, ub_ref, l1b_ref, l2b_ref, db_ref):
            r[...] = jnp.zeros_like(r)

    terms_f, terms_b = [], []
    strands = [_gdn_scan_group(uf_ref, l1f_ref, l2f_ref, df_ref, sf_ref, of_ref, False),
               _gdn_scan_group(ub_ref, l1b_ref, l2b_ref, db_ref, sb_ref, ob_ref, True),
               _gdn_chunk_terms(qkvf_ref[...], gbf_ref[...], gbtf_ref[...], False, terms_f),
               _gdn_chunk_terms(qkvb_ref[...], gbb_ref[...], gbtb_ref[...], True, terms_b)]
    while strands:
        strands = [s for s in strands if next(s, True) is None]
    for (u, l1, l2, dch), (u_ref, l1_ref, l2_ref, d_ref) in ((terms_f, (uf_ref, l1f_ref, l2f_ref, df_ref)),
                                                          (terms_b, (ub_ref, l1b_ref, l2b_ref, db_ref))):
        u_ref[...] = u.reshape(H, G, C, HEAD_DIM)
        l1_ref[...] = l1.reshape(H, G, 2 * C, HEAD_DIM)
        l2_ref[...] = l2.reshape(H, G, 2 * C, C)
        d_ref[...] = dch.reshape(H, G, 8, HEAD_DIM)


def _gdn(qkv, gb, gbt):
    L = qkv.shape[1]
    G, C, H = GDN_G, GDN_CHUNK, GDN_HEADS
    T = G * C
    n = L // T
    fin = lambda s: jnp.minimum(s, n - 1)
    bin_ = lambda s: jnp.maximum(n - 1 - s, 0)
    fout = lambda s: jnp.maximum(s - 1, 0)
    bout = lambda s: jnp.minimum(n - s, n - 1)
    term_scratch = [pltpu.VMEM((H, G, C, HEAD_DIM), F32), pltpu.VMEM((H, G, 2 * C, HEAD_DIM), BF16),
                    pltpu.VMEM((H, G, 2 * C, C), BF16), pltpu.VMEM((H, G, 8, HEAD_DIM), F32)]
    return pl.pallas_call(
        _gdn_kernel,
        grid=(n + 1,),
        in_specs=[pl.BlockSpec((3 * H, T, HEAD_DIM), lambda s: (0, fin(s), 0)),
                  pl.BlockSpec((T, LANES), lambda s: (fin(s), 0)),
                  pl.BlockSpec((24, T), lambda s: (0, fin(s))),
                  pl.BlockSpec((3 * H, T, HEAD_DIM), lambda s: (0, bin_(s), 0)),
                  pl.BlockSpec((T, LANES), lambda s: (bin_(s), 0)),
                  pl.BlockSpec((24, T), lambda s: (0, bin_(s)))],
        out_specs=[pl.BlockSpec((T, 256), lambda s: (fout(s), 0)),
                   pl.BlockSpec((T, 256), lambda s: (bout(s), 0))],
        out_shape=[jax.ShapeDtypeStruct((L, 256), BF16), jax.ShapeDtypeStruct((L, 256), BF16)],
        scratch_shapes=[pltpu.VMEM((H, HEAD_DIM, HEAD_DIM), F32), pltpu.VMEM((H, HEAD_DIM, HEAD_DIM), F32)]
                       + term_scratch + term_scratch,
        compiler_params=_cparams(("arbitrary",)),
        name="gdn_scan",
    )(qkv, gb, gbt, qkv, gb, gbt)


def _merge_kernel(x_ref, yna_ref, yswa_ref, of_ref, ob_ref, z_ref, gate_ref, gn_ref, bd_ref,
                  wna_ref, wswa_ref, wgdn_ref, wout_ref, o_ref):
    o = of_ref[...].astype(F32) + ob_ref[...].astype(F32)
    z = z_ref[...].astype(F32)
    ms = _group_sum(o * o, bd_ref) * (1.0 / HEAD_DIM)
    ygdn = o * lax.rsqrt(ms + NORM_EPS) * gn_ref[...] * (z * _sigmoid(z))
    m = (gate_ref[:, 0:D_MODEL].astype(F32) * _dot(yna_ref[...], wna_ref[...])
         + gate_ref[:, D_MODEL:2 * D_MODEL].astype(F32) * _dot(yswa_ref[...], wswa_ref[...])
         + gate_ref[:, 2 * D_MODEL:3 * D_MODEL].astype(F32) * _dot(ygdn.astype(BF16), wgdn_ref[...]))
    o_ref[...] = x_ref[...] + _dot(m.astype(BF16), wout_ref[...])


def _merge(x, yna, yswa, of, ob, z, gate, l, gn, bd, wna, wswa, wgdn, wout):
    L = x.shape[0]
    row = lambda w_: pl.BlockSpec((TM, w_), lambda i: (i, 0))
    return pl.pallas_call(
        _merge_kernel,
        grid=(L // TM,),
        in_specs=[row(D_MODEL), row(256), row(512), row(256), row(256), row(256), row(SEG_GATE),
                  _layer_spec(gn, l), _const_spec(bd), _layer_spec(wna, l), _layer_spec(wswa, l),
                  _layer_spec(wgdn, l), _layer_spec(wout, l)],
        out_specs=row(D_MODEL),
        out_shape=jax.ShapeDtypeStruct((L, D_MODEL), F32),
        compiler_params=_cparams(("parallel",)),
        name="merge",
    )(x, yna, yswa, of, ob, z, gate, gn, bd, wna, wswa, wgdn, wout)


def _ffn_kernel(x_ref, xp_ref, xn_ref, g_ref, wu_ref, cw_ref, cb_ref, wd_ref, o_ref, hbuf, *, nblk):
    i = pl.program_id(0)

    def normed(xv):
        ms = jnp.mean(xv * xv, axis=-1, keepdims=True)
        return xv * lax.rsqrt(ms + NORM_EPS) * g_ref[...]

    hbuf[0:HALO, :] = jnp.where(i > 0, normed(xp_ref[...]), 0.0).astype(BF16)
    hbuf[HALO:HALO + TM, :] = normed(x_ref[...]).astype(BF16)
    hbuf[HALO + TM:, :] = jnp.where(i < nblk - 1, normed(xn_ref[...]), 0.0).astype(BF16)
    h = hbuf[...]
    n = TM + 2 * HALO
    chunks = [slice(a, b) for a, b in zip(FFN_SPLIT[:-1], FFN_SPLIT[1:])]
    gate_cols = [slice(D_FF + a, D_FF + b) for a, b in zip(FFN_SPLIT[:-1], FFN_SPLIT[1:])]

    def conv(u, cols):
        y = (pltpu.roll(u, 1, 0) * cw_ref[0:1, cols] + u * cw_ref[1:2, cols]
             + pltpu.roll(u, n - 1, 0) * cw_ref[2:3, cols])
        return y[HALO:HALO + TM] + cb_ref[:, cols]

    ups = [(_dot(h, wu_ref[:, ca]), _dot(h, wu_ref[:, cb])) for ca, cb in zip(chunks, gate_cols)]
    acc = x_ref[...]
    for (ua, ub), ca, cb in zip(ups, chunks, gate_cols):
        a = conv(ua, ca)
        gated = (a * _sigmoid(a) * conv(ub, cb)).astype(BF16)
        acc = acc + _dot(gated, wd_ref[ca, :])
    o_ref[...] = acc


def _ffn(x, l, g, w_up, conv_w, conv_b, w_down):
    L = x.shape[0]
    nblk = L // TM
    hb = TM // HALO
    once = pl.Buffered(1)
    return pl.pallas_call(
        functools.partial(_ffn_kernel, nblk=nblk),
        grid=(nblk,),
        in_specs=[pl.BlockSpec((TM, D_MODEL), lambda i: (i, 0)),
                  pl.BlockSpec((HALO, D_MODEL), lambda i: (jnp.maximum(i * hb - 1, 0), 0)),
                  pl.BlockSpec((HALO, D_MODEL), lambda i: (jnp.minimum((i + 1) * hb, L // HALO - 1), 0)),
                  _layer_spec(g, l), _layer_spec(w_up, l, pipeline_mode=once), _layer_spec(conv_w, l),
                  _layer_spec(conv_b, l), _layer_spec(w_down, l, pipeline_mode=once)],
        out_specs=pl.BlockSpec((TM, D_MODEL), lambda i: (i, 0)),
        out_shape=jax.ShapeDtypeStruct((L, D_MODEL), F32),
        scratch_shapes=[pltpu.VMEM((TM + 2 * HALO, D_MODEL), BF16)],
        compiler_params=_cparams(("parallel",)),
        name="ffn",
    )(x, x, x, g, w_up, conv_w, conv_b, w_down)


PACK_SEGMENTS = ((0, 256, 0), (256, 512, 256), (768, 1280, 512), (1280, 1408, 1024),
                 (1536, 2304, 1152), (512, 768, 1920), (2304, 2560, 2176), (1408, 1536, 2432),
                 (2576, 5648, 2560), (2560, 2576, 5632))
PACK_K = 256
PACK_PIECE = 512


def _pack_kernel(w_ref, o_ref):
    for s0, s1, d0 in PACK_SEGMENTS[:-1]:
        for p0 in range(s0, s1, PACK_PIECE):
            p1 = min(p0 + PACK_PIECE, s1)
            o_ref[:, d0 + p0 - s0:d0 + p1 - s0] = w_ref[p0:p1, :].T.astype(BF16)
    s0, s1, d0 = PACK_SEGMENTS[-1]
    lane = lax.broadcasted_iota(jnp.int32, (1, LANES), 1)
    o_ref[:, d0:d0 + LANES] = jnp.where(lane < s1 - s0, w_ref[s0:s0 + LANES, :].T, 0.0).astype(BF16)


def _pack_w_in(w):
    depth, k, n = w.shape
    return pl.pallas_call(
        _pack_kernel,
        grid=(depth, k // PACK_K),
        in_specs=[pl.BlockSpec((None, n, PACK_K), lambda l, i: (l, 0, i))],
        out_specs=pl.BlockSpec((None, PACK_K, IN_PACKED), lambda l, i: (l, i, 0)),
        out_shape=jax.ShapeDtypeStruct((depth, k, IN_PACKED), BF16),
        compiler_params=_cparams(("parallel", "parallel")),
        name="pack_w_in",
    )(jnp.swapaxes(w, 1, 2))


def _rope_tables(L):
    inv = 1.0 / (ROPE_THETA ** (np.arange(0, HEAD_DIM, 2, dtype=np.float64) / HEAD_DIM))
    freq = np.tile(inv, LANES // inv.size)
    sign = np.where(np.arange(LANES) % HEAD_DIM < HEAD_DIM // 2, -1.0, 1.0)
    start = np.arange(L // TM, dtype=np.float64)[:, None] * TM * freq[None, :]
    off = np.arange(TM, dtype=np.float64)[:, None] * freq[None, :]
    blk = np.stack([np.cos(start), np.sin(start)], axis=1)
    offs = np.stack([np.cos(off), np.sin(off), sign * np.cos(off), sign * np.sin(off)])
    return jnp.asarray(blk, F32), jnp.asarray(offs, F32)


def kernel(x, attn_norm, w_in, qk_norm, na_rpb, swa_sink, gdn_conv_w, gdn_a_log, gdn_dt_bias, gdn_norm,
           w_branch_na, w_branch_swa, w_branch_gdn, w_out, ffn_norm, w_up, ffn_conv_w, ffn_conv_b, w_down):
    B, L, D = x.shape
    assert B == 1 and D == D_MODEL and L % (NA_ROWS * GRID_W) == 0 and L // GRID_W >= 2 * NA_ROWS
    depth = w_in.shape[0]
    rope_blk, rope_off = _rope_tables(L)
    blockdiag = jnp.asarray(np.kron(np.eye(4), np.ones((HEAD_DIM, HEAD_DIM))), BF16)
    bias_tabs = _na_bias_tables(na_rpb, L // GRID_W)
    swa_band = _swa_band_tables()
    scale = HEAD_DIM ** -0.5 * LOG2E
    G = SWA_Q_HEADS // SWA_KV_HEADS
    gain = jnp.concatenate([jnp.tile(qk_norm[:, 0] * scale, (1, NA_HEADS)), jnp.tile(qk_norm[:, 1], (1, NA_HEADS)),
                            jnp.tile(qk_norm[:, 2] * scale, (1, SWA_Q_HEADS)),
                            jnp.tile(qk_norm[:, 3], (1, SWA_KV_HEADS))], axis=1)[:, None, :]
    sink_row = jnp.repeat(swa_sink.reshape(depth, SWA_Q_HEADS // 2, 2) * LOG2E, SWA_BLOCK, axis=2)[:, :, None, :]
    alog_row = jnp.pad(gdn_a_log.reshape(depth, 1, 8), ((0, 0), (0, 0), (8, LANES - 16)))
    dtb_row = jnp.pad(gdn_dt_bias.reshape(depth, 1, 8), ((0, 0), (0, 0), (8, LANES - 16)))
    gdn_gain = jnp.tile(gdn_norm, (1, GDN_HEADS))[:, None, :]
    attn_g = attn_norm[:, None, :]
    ffn_g = ffn_norm[:, None, :]
    ffn_b = ffn_conv_b[:, None, :]
    w_in_p = _pack_w_in(w_in)
    w_na, w_swa, w_gdn = w_branch_na.astype(BF16), w_branch_swa.astype(BF16), w_branch_gdn.astype(BF16)
    w_o, w_u, w_d = w_out.astype(BF16), w_up.astype(BF16), w_down.astype(BF16)
    xs = x[0]
    for l in range(depth):
        qk, z, gate, va_t, vs_t, qkv_n, gb, gbt = _inproj(xs, l, attn_g, w_in_p, gain, rope_blk, rope_off, blockdiag,
                                                          gdn_conv_w, alog_row, dtb_row)
        y_na = _na_attention(qk, va_t, l, bias_tabs)
        y_swa = _swa_attention(qk, vs_t, l, swa_band, sink_row)
        o_f, o_b = _gdn(qkv_n, gb, gbt)
        xs = _merge(xs, y_na, y_swa, o_f, o_b, z, gate, l, gdn_gain, blockdiag, w_na, w_swa, w_gdn, w_o)
        xs = _ffn(xs, l, ffn_g, w_u, ffn_conv_w, ffn_b, w_d)
    return xs[None]
```

```python
import functools

import numpy as np
import jax
import jax.numpy as jnp
from jax import lax
from jax.experimental import pallas as pl
from jax.experimental.pallas import tpu as pltpu

F32 = jnp.float32
BF16 = jnp.bfloat16

D_MODEL = 1024
HEAD_DIM = 64
GRID_W = 64
NORM_EPS = 1e-6
NA_HEADS = 4
NA_KH = 8
NA_KW = 16
SWA_Q_HEADS = 8
SWA_KV_HEADS = 2
SWA_BLOCK = 128
ROPE_THETA = 10000.0
GDN_HEADS = 4
GDN_CHUNK = 64
D_FF = 2816

LANES = 128
NEG_BIG = -1e30
LOG2E = 1.4426950408889634
VMEM_LIMIT = 56 * 1024 * 1024

SEG_QK = 1152
SEG_PLAIN = 1408
PLAIN_VA = 768
PLAIN_Z = 1024
PLAIN_VS = 1280
SEG_GATE = 3072
SEG_BA = 128
IN_PACKED = SEG_QK + SEG_PLAIN + SEG_GATE + SEG_BA

TM = 512
NA_ROWS = 16
NA_WIN = NA_KH + 2
SWA_QB = 8
GDN_G = 4
GDN_SUB = 8
MXU_TILE = 256
FFN_SPLIT = (0, 6 * MXU_TILE, D_FF)
HALO = 8


def _cparams(sem):
    return pltpu.CompilerParams(dimension_semantics=sem, vmem_limit_bytes=VMEM_LIMIT)


def _dot(a, b):
    return jnp.dot(a, b, preferred_element_type=F32)


def _sigmoid(x):
    return 1.0 / (1.0 + jnp.exp(-x))


def _group_sum(sq, bd_ref):
    w = sq.shape[-1]
    return _dot(sq.astype(BF16), bd_ref[0:w, 0:w])


def _gdn_token_qkv(x, row_prev, row_next, cw_ref, bd_ref, qkv_ref):
    rid = lax.broadcasted_iota(jnp.int32, (TM, 1), 0)
    xprev = jnp.where(rid == 0, row_prev, pltpu.roll(x, 1, 0))
    xnext = jnp.where(rid == TM - 1, row_next, pltpu.roll(x, TM - 1, 0))
    y = xprev * cw_ref[0:1, :] + x * cw_ref[1:2, :] + xnext * cw_ref[2:3, :]
    y = y * _sigmoid(y)
    for part in range(3):
        yp = y[:, part * 256:(part + 1) * 256]
        if part < 2:
            yp = yp * lax.rsqrt(_group_sum(yp * yp, bd_ref) + NORM_EPS)
        if part == 0:
            yp = yp * (HEAD_DIM ** -0.5)
        yp = yp.astype(BF16)
        for hh in range(GDN_HEADS):
            qkv_ref[part * GDN_HEADS + hh] = yp[:, hh * HEAD_DIM:(hh + 1) * HEAD_DIM]


def _gdn_token_gates(ba, alog_ref, dtb_ref, gb_ref, gbt_ref):
    rid = lax.broadcasted_iota(jnp.int32, (TM, 1), 0)
    lane = lax.broadcasted_iota(jnp.int32, (1, LANES), 1)
    beta = _sigmoid(ba)
    sp_in = ba + dtb_ref[...]
    softplus = jnp.maximum(sp_in, 0.0) + jnp.log(1.0 + jnp.exp(-jnp.abs(sp_in)))
    g = jnp.where((lane >= 8) & (lane < 16), -jnp.exp(alog_ref[...]) * softplus, 0.0)
    rc = rid % GDN_CHUNK
    pre = g
    suf = g
    s = 1
    while s < GDN_CHUNK:
        pre = pre + jnp.where(rc >= s, pltpu.roll(pre, s, 0), 0.0)
        suf = suf + jnp.where(rc < GDN_CHUNK - s, pltpu.roll(suf, TM - s, 0), 0.0)
        s *= 2
    tot = pre + suf - g
    gc = jnp.where(lane < 12, pre, suf)
    slab = jnp.where(lane < 8, beta, jnp.where(lane < 16, gc, jnp.where(lane < 24, pltpu.roll(tot, 8, 1), 0.0)))
    gb_ref[...] = slab
    gbt_ref[...] = slab.T[0:24, :]


def _inproj_kernel(x_ref, xp_ref, xn_ref, g_ref, w_ref, gain_ref, rope_blk_ref, rope_off_ref, bd_ref,
                   cw_ref, alog_ref, dtb_ref,
                   oqk_ref, oz_ref, ogate_ref, ovat_ref, ovst_ref, oqkvh_ref, ogb_ref, ogbt_ref, *, nblk):
    i = pl.program_id(0)

    def normed(xv):
        ms = jnp.mean(xv * xv, axis=-1, keepdims=True)
        return xv * lax.rsqrt(ms + NORM_EPS) * g_ref[...]

    h = normed(x_ref[...]).astype(BF16)
    h_halo = jnp.concatenate([jnp.where(i > 0, normed(xp_ref[...]), 0.0),
                              jnp.where(i < nblk - 1, normed(xn_ref[...]), 0.0)], axis=0).astype(BF16)
    cw = 2 * LANES
    lane = lax.broadcasted_iota(jnp.int32, (1, cw), 1)
    first_half = (lane % HEAD_DIM) < (HEAD_DIM // 2)
    c_a, s_a = rope_blk_ref[0:1, :], rope_blk_ref[1:2, :]
    cos128 = c_a * rope_off_ref[0] - s_a * rope_off_ref[1]
    sin128 = s_a * rope_off_ref[2] + c_a * rope_off_ref[3]
    cos = jnp.concatenate([cos128, cos128], axis=1)
    sin = jnp.concatenate([sin128, sin128], axis=1)

    def head_norm(t, c):
        cols = slice(c * cw, min((c + 1) * cw, SEG_QK))
        w = cols.stop - cols.start
        tc = t[:, cols]
        ss = _group_sum(tc * tc, bd_ref)
        y = tc * lax.rsqrt(ss * (1.0 / HEAD_DIM) + NORM_EPS) * gain_ref[:, cols]
        if cols.start >= 512:
            rot = jnp.where(first_half[:, 0:w], pltpu.roll(y, w - HEAD_DIM // 2, 1), pltpu.roll(y, HEAD_DIM // 2, 1))
            y = y * cos[:, 0:w] + rot * sin[:, 0:w]
        oqk_ref[:, cols] = y.astype(BF16)

    o_plain, o_gate, o_ba = SEG_QK, SEG_QK + SEG_PLAIN, SEG_QK + SEG_PLAIN + SEG_GATE
    gate_cols = [slice(o_gate + c * D_MODEL, o_gate + (c + 1) * D_MODEL) for c in range(3)]
    out_cols = [slice(c * D_MODEL, (c + 1) * D_MODEL) for c in range(3)]
    t = _dot(h, w_ref[:, 0:SEG_QK])
    t_gdn = _dot(h, w_ref[:, o_plain:o_plain + 768])
    tba = _dot(h, w_ref[:, o_ba:o_ba + SEG_BA])
    t_halo = _dot(h_halo, w_ref[:, o_plain:o_plain + 768])
    for c in (0, 1):
        head_norm(t, c)
    tg0 = _dot(h, w_ref[:, gate_cols[0]])
    for c in (2, 3, 4):
        head_norm(t, c)
    tg1 = _dot(h, w_ref[:, gate_cols[1]])
    ogate_ref[:, out_cols[0]] = _sigmoid(tg0).astype(BF16)
    _gdn_token_qkv(t_gdn, t_halo[HALO - 1:HALO, :], t_halo[HALO:HALO + 1, :], cw_ref, bd_ref, oqkvh_ref)
    tg2 = _dot(h, w_ref[:, gate_cols[2]])
    ogate_ref[:, out_cols[1]] = _sigmoid(tg1).astype(BF16)
    _gdn_token_gates(tba, alog_ref, dtb_ref, ogb_ref, ogbt_ref)
    t_rest = _dot(h, w_ref[:, o_plain + PLAIN_VA:o_plain + SEG_PLAIN])
    ogate_ref[:, out_cols[2]] = _sigmoid(tg2).astype(BF16)
    oz_ref[...] = t_rest[:, PLAIN_Z - PLAIN_VA:PLAIN_Z - PLAIN_VA + 256].astype(BF16)
    for grp in range(TM // LANES):
        rows = slice(grp * LANES, (grp + 1) * LANES)
        ovat_ref[grp] = t_rest[rows, 0:256].T.astype(BF16)
        ovst_ref[grp] = t_rest[rows, PLAIN_VS - PLAIN_VA:PLAIN_VS - PLAIN_VA + 128].T.astype(BF16)


def _layer_spec(a, l, **kw):
    nd = a.ndim - 1
    return pl.BlockSpec((None,) + a.shape[1:], lambda *_: (l,) + (0,) * nd, **kw)


def _const_spec(a):
    return pl.BlockSpec(a.shape, lambda *_: (0,) * a.ndim)


def _inproj(x, l, g, w, gain, rope_blk, rope_off, bd, conv_w, alog_row, dtb_row):
    L = x.shape[0]
    nblk = L // TM
    hb = TM // HALO
    row = lambda w_: pl.BlockSpec((TM, w_), lambda i: (i, 0))
    return pl.pallas_call(
        functools.partial(_inproj_kernel, nblk=nblk),
        grid=(nblk,),
        in_specs=[row(D_MODEL),
                  pl.BlockSpec((HALO, D_MODEL), lambda i: (jnp.maximum(i * hb - 1, 0), 0)),
                  pl.BlockSpec((HALO, D_MODEL), lambda i: (jnp.minimum((i + 1) * hb, L // HALO - 1), 0)),
                  _layer_spec(g, l), _layer_spec(w, l, pipeline_mode=pl.Buffered(1)),
                  _layer_spec(gain, l), pl.BlockSpec((None, 2, LANES), lambda i: (i, 0, 0)), _const_spec(rope_off),
                  _const_spec(bd), _layer_spec(conv_w, l), _layer_spec(alog_row, l), _layer_spec(dtb_row, l)],
        out_specs=[row(SEG_QK), row(256), row(SEG_GATE),
                   pl.BlockSpec((TM // LANES, 256, LANES), lambda i: (i, 0, 0)),
                   pl.BlockSpec((TM // LANES, 128, LANES), lambda i: (i, 0, 0)),
                   pl.BlockSpec((3 * GDN_HEADS, TM, HEAD_DIM), lambda i: (0, i, 0)),
                   row(LANES),
                   pl.BlockSpec((24, TM), lambda i: (0, i))],
        out_shape=[jax.ShapeDtypeStruct((L, SEG_QK), BF16),
                   jax.ShapeDtypeStruct((L, 256), BF16),
                   jax.ShapeDtypeStruct((L, SEG_GATE), BF16),
                   jax.ShapeDtypeStruct((L // LANES, 256, LANES), BF16),
                   jax.ShapeDtypeStruct((L // LANES, 128, LANES), BF16),
                   jax.ShapeDtypeStruct((3 * GDN_HEADS, L, HEAD_DIM), BF16),
                   jax.ShapeDtypeStruct((L, LANES), F32),
                   jax.ShapeDtypeStruct((24, L), F32)],
        compiler_params=_cparams(("parallel",)),
        name="inproj",
    )(x, x, x, g, w, gain, rope_blk, rope_off, bd, conv_w, alog_row, dtb_row)


def _toeplitz_kernel(rpb_ref, onehot_ref, colmask_ref, o_ref):
    r = rpb_ref[...]
    hi = r.astype(BF16)
    r1 = r - hi.astype(F32)
    mid = r1.astype(BF16)
    lo = (r1 - mid.astype(F32)).astype(BF16)
    oh = onehot_ref[...]
    w = _dot(hi, oh) + _dot(mid, oh) + _dot(lo, oh)
    o_ref[...] = jnp.where(colmask_ref[...] > 0.0, w * LOG2E, NEG_BIG)


def _na_assemble_kernel(tiles_ref, o_ref, *, a_idx):
    neg = jnp.full((GRID_W, GRID_W), NEG_BIG, F32)
    for t in range(a_idx.shape[0]):
        for j in range(NA_WIN):
            for e in range(2):
                a = int(a_idx[t, j, e])
                o_ref[t, j * GRID_W:(j + 1) * GRID_W, e * GRID_W:(e + 1) * GRID_W] = neg if a < 0 else tiles_ref[a]


def _na_bias_tables(na_rpb, rows):
    depth, H, nr, nc = na_rpb.shape
    kc = np.arange(GRID_W)[:, None]
    qc = np.arange(GRID_W)[None, :]
    dc = np.clip(kc - qc + (NA_KW - 1), 0, 2 * NA_KW - 2).reshape(-1)
    onehot = np.zeros((32, GRID_W * GRID_W), np.float32)
    onehot[dc, np.arange(GRID_W * GRID_W)] = 1.0
    col_start = np.clip(qc - NA_KW // 2, 0, GRID_W - NA_KW)
    colmask = ((kc >= col_start) & (kc < col_start + NA_KW)).astype(np.float32).reshape(1, -1)
    rpb_rows = jnp.pad(na_rpb, ((0, 0), (0, 0), (0, 16 - nr), (0, 32 - nc))).reshape(depth * H * 16, 32)
    tiles = pl.pallas_call(
        _toeplitz_kernel,
        out_shape=jax.ShapeDtypeStruct((depth * H * 16, GRID_W * GRID_W), F32),
        name="na_bias_tiles",
    )(rpb_rows, jnp.asarray(onehot, BF16), jnp.asarray(colmask))
    tiles = tiles.reshape(depth * H, 16, GRID_W, GRID_W)
    a_idx = np.stack([_na_pair_structure(r0, rows)[1] for r0 in _na_pair_type_rows(rows)])
    nt = a_idx.shape[0]
    tabs = pl.pallas_call(
        functools.partial(_na_assemble_kernel, a_idx=a_idx),
        grid=(depth * H,),
        in_specs=[pl.BlockSpec((None, 16, GRID_W, GRID_W), lambda i: (i, 0, 0, 0))],
        out_specs=pl.BlockSpec((None, nt, NA_WIN * GRID_W, 2 * GRID_W), lambda i: (i, 0, 0, 0)),
        out_shape=jax.ShapeDtypeStruct((depth * H, nt, NA_WIN * GRID_W, 2 * GRID_W), F32),
        compiler_params=_cparams(("parallel",)),
        name="na_bias_tables",
    )(tiles)
    return tabs.reshape(depth, H, nt, NA_WIN * GRID_W, 2 * GRID_W)


def _na_pair_structure(r0, rows):
    wstart = int(np.clip(r0 - NA_KH // 2, 0, rows - NA_KH - 1)) // 2 * 2
    a_idx = np.full((NA_WIN, 2), -1, np.int64)
    for e in range(2):
        rr = r0 + e
        rs = int(np.clip(rr - NA_KH // 2, 0, rows - NA_KH))
        for j in range(NA_WIN):
            krow = wstart + j
            if rs <= krow < rs + NA_KH:
                a_idx[j, e] = krow - rr + NA_KH - 1
    return wstart, a_idx


def _na_pair_type_rows(rows):
    reps = [0, 2, 4, rows - 4, rows - 2]
    for r0 in range(0, rows, 2):
        t = 3 + (r0 - (rows - 4)) // 2 if r0 >= rows - 4 else min(r0 // 2, 2)
        ws, a = _na_pair_structure(r0, rows)
        ws_t, a_t = _na_pair_structure(reps[t], rows)
        assert (a == a_t).all() and r0 - ws == reps[t] - ws_t
    return reps


def _na_kernel(q_ref, kp_ref, kc_ref, kn_ref, vp_ref, vc_ref, vn_ref, bias_ref, o_ref, kbuf, vbuf, *, rows):
    i = pl.program_id(0)
    blk = NA_ROWS * GRID_W
    grp = blk // LANES
    kbuf[0:blk, :] = kp_ref[...]
    kbuf[blk:2 * blk, :] = kc_ref[...]
    kbuf[2 * blk:3 * blk, :] = kn_ref[...]
    vbuf[0:grp] = vp_ref[...]
    vbuf[grp:2 * grp] = vc_ref[...]
    vbuf[2 * grp:3 * grp] = vn_ref[...]
    win = NA_WIN * GRID_W
    wgrp = win // LANES
    pair = 2 * GRID_W

    npair = NA_ROWS // 2
    goffs, types = [], []
    for pp in range(npair):
        r0 = i * NA_ROWS + 2 * pp
        wstart = jnp.clip(r0 - NA_KH // 2, 0, rows - NA_KH - 1) // 2 * 2
        types.append(jnp.where(r0 >= rows - 4, 3 + (r0 - (rows - 4)) // 2, jnp.minimum(r0 // 2, 2)))
        goffs.append((wstart - (i - 1) * NA_ROWS) // 2)

    def logits(pp, h):
        sl = slice(h * HEAD_DIM, (h + 1) * HEAD_DIM)
        kw = kbuf[pl.ds(pl.multiple_of(goffs[pp] * LANES, LANES), win), sl]
        s = lax.dot_general(kw, q_ref[pp * pair:(pp + 1) * pair, sl],
                            (((1,), (1,)), ((), ())), preferred_element_type=F32)
        return s + bias_ref[h, types[pp]]

    def softmax(s):
        m = jnp.max(s, axis=0, keepdims=True)
        p = jnp.exp2(s - m)
        return p.astype(BF16), 1.0 / jnp.sum(p, axis=0, keepdims=True)

    def weighted(p_inv, pp, h):
        p, inv_l = p_inv
        vt = jnp.concatenate([vbuf[goffs[pp] + j, h * HEAD_DIM:(h + 1) * HEAD_DIM, :] for j in range(wgrp)], axis=-1)
        return _dot(vt, p) * inv_l

    tiles = [(pp, h) for pp in range(npair) for h in range(NA_HEADS)]
    s_q, p_q, outs = {}, {}, {}
    lead, lag = 2, 1
    for n in range(len(tiles) + lead + lag):
        if n < len(tiles):
            s_q[n] = logits(*tiles[n])
        if lead <= n < len(tiles) + lead:
            p_q[n - lead] = softmax(s_q.pop(n - lead))
        if n >= lead + lag:
            pp, h = tiles[n - lead - lag]
            outs[(pp, h)] = weighted(p_q.pop(n - lead - lag), pp, h)
            if h == NA_HEADS - 1:
                o_t = jnp.concatenate([outs.pop((pp, hh)) for hh in range(NA_HEADS)], axis=0)
                o_ref[pp * pair:(pp + 1) * pair, :] = o_t.T.astype(BF16)


def _na_attention(qk, va_t, l, bias):
    L = qk.shape[0]
    rows = L // GRID_W
    nblk = rows // NA_ROWS
    blk = NA_ROWS * GRID_W
    grp = blk // LANES
    w = NA_HEADS * HEAD_DIM
    prev = lambda i: jnp.maximum(i - 1, 0)
    nxt = lambda i: jnp.minimum(i + 1, nblk - 1)
    spec = lambda f, c: pl.BlockSpec((blk, w), lambda i: (f(i), c))
    vspec = lambda f: pl.BlockSpec((grp, w, LANES), lambda i: (f(i), 0, 0))
    same = lambda i: i
    return pl.pallas_call(
        functools.partial(_na_kernel, rows=rows),
        grid=(nblk,),
        in_specs=[spec(same, 0),
                  spec(prev, 1), spec(same, 1), spec(nxt, 1),
                  vspec(prev), vspec(same), vspec(nxt),
                  _layer_spec(bias, l)],
        out_specs=pl.BlockSpec((blk, w), lambda i: (i, 0)),
        out_shape=jax.ShapeDtypeStruct((L, w), BF16),
        scratch_shapes=[pltpu.VMEM((3 * blk, w), BF16), pltpu.VMEM((3 * grp, w, LANES), BF16)],
        compiler_params=_cparams(("parallel",)),
        name="na_attn",
    )(qk, qk, qk, qk, va_t, va_t, va_t, bias)


def _swa_kernel(q_ref, kp_ref, kc_ref, kn_ref, vp_ref, vc_ref, vn_ref, band_ref, sink_ref, o_ref, *, nstep):
    i = pl.program_id(0)
    B = SWA_BLOCK
    G = SWA_Q_HEADS // SWA_KV_HEADS
    k_all = jnp.concatenate([kp_ref[...], kc_ref[...], kn_ref[...]], axis=0)
    vt_all = [vp_ref[0]] + [vc_ref[j] for j in range(SWA_QB)] + [vn_ref[0]]

    def band_index(b):
        if b == 0:
            return jnp.where(i == 0, 1, 0)
        if b == SWA_QB - 1:
            return jnp.where(i == nstep - 1, 2, 0)
        return 0

    def logits(b, hp):
        g = hp // (G // 2)
        ks = k_all[b * B:(b + 3) * B, g * HEAD_DIM:(g + 1) * HEAD_DIM]
        qs = jnp.concatenate([q_ref[b * B:(b + 1) * B, (2 * hp + e) * HEAD_DIM:(2 * hp + e + 1) * HEAD_DIM]
                              for e in range(2)], axis=0)
        s = lax.dot_general(ks, qs, (((1,), (1,)), ((), ())), preferred_element_type=F32)
        t = band_index(b)
        return jnp.concatenate([s[0:B] + band_ref[t, 0:B], s[B:2 * B], s[2 * B:3 * B] + band_ref[t, 2 * B:3 * B]],
                               axis=0)

    def softmax(s, hp):
        sink = sink_ref[hp]
        m = jnp.maximum(jnp.max(s, axis=0, keepdims=True), sink)
        p = jnp.exp2(s - m)
        l = jnp.sum(p, axis=0, keepdims=True) + jnp.exp2(sink - m)
        return p.astype(BF16), 1.0 / l

    def weighted(p_inv, b, hp):
        p, inv_l = p_inv
        g = hp // (G // 2)
        vt = jnp.concatenate([vt_all[b + j][g * HEAD_DIM:(g + 1) * HEAD_DIM, :] for j in range(3)], axis=-1)
        return _dot(vt, p) * inv_l

    tiles = [(b, hp) for b in range(SWA_QB) for hp in range(SWA_Q_HEADS // 2)]
    s_q, p_q = {}, {}
    lead, lag = 2, 1
    for n in range(len(tiles) + lead + lag):
        if n < len(tiles):
            s_q[n] = logits(*tiles[n])
        if lead <= n < len(tiles) + lead:
            p_q[n - lead] = softmax(s_q.pop(n - lead), tiles[n - lead][1])
        if n >= lead + lag:
            b, hp = tiles[n - lead - lag]
            o_t = weighted(p_q.pop(n - lead - lag), b, hp)
            pair_t = jnp.concatenate([o_t[:, 0:B], o_t[:, B:2 * B]], axis=0)
            o_ref[b * B:(b + 1) * B, 2 * hp * HEAD_DIM:(2 * hp + 2) * HEAD_DIM] = pair_t.T.astype(BF16)


def _swa_band_tables():
    B = SWA_BLOCK
    qi = np.arange(2 * B)[None, :] % B
    kj = np.arange(3 * B)[:, None]
    band = np.abs(kj - B - qi) <= B
    tabs = [band, band & (kj >= B), band & (kj < 2 * B)]
    return jnp.asarray(np.where(np.stack(tabs), 0.0, NEG_BIG), F32)


def _swa_attention(qk, vs_t, l, band, sink_row):
    L = qk.shape[0]
    B = SWA_BLOCK
    T = SWA_QB * B
    nstep = L // T
    nblk = L // B
    prev = lambda i: jnp.maximum(i * SWA_QB - 1, 0)
    nxt = lambda i: jnp.minimum((i + 1) * SWA_QB, nblk - 1)
    return pl.pallas_call(
        functools.partial(_swa_kernel, nstep=nstep),
        grid=(nstep,),
        in_specs=[pl.BlockSpec((T, 512), lambda i: (i, 1)),
                  pl.BlockSpec((B, LANES), lambda i: (prev(i), 8)),
                  pl.BlockSpec((T, LANES), lambda i: (i, 8)),
                  pl.BlockSpec((B, LANES), lambda i: (nxt(i), 8)),
                  pl.BlockSpec((1, LANES, B), lambda i: (prev(i), 0, 0)),
                  pl.BlockSpec((SWA_QB, LANES, B), lambda i: (i, 0, 0)),
                  pl.BlockSpec((1, LANES, B), lambda i: (nxt(i), 0, 0)),
                  _const_spec(band), _layer_spec(sink_row, l)],
        out_specs=pl.BlockSpec((T, 512), lambda i: (i, 0)),
        out_shape=jax.ShapeDtypeStruct((L, 512), BF16),
        compiler_params=_cparams(("parallel",)),
        name="swa_attn",
    )(qk, qk, qk, qk, vs_t, vs_t, vs_t, band, sink_row)


def _bmm(a, b):
    return jnp.einsum('bij,bjk->bik', a.astype(BF16), b.astype(BF16), preferred_element_type=F32)


def _bmm_nt(a, b):
    return jnp.einsum('bid,bjd->bij', a.astype(BF16), b.astype(BF16), preferred_element_type=F32)


def _gdn_chunk_terms(qkv, gb, gbt, rev, out):
    G, C, H = GDN_G, GDN_CHUNK, GDN_HEADS

    def heads(part):
        return qkv[part * H:(part + 1) * H].reshape(H * G, C, HEAD_DIM)

    q = heads(0)
    k = heads(1)
    v = heads(2).astype(F32)
    kf = k.astype(F32)
    qf = q.astype(F32)
    d0 = H if rev else 0

    def colv(base):
        return jnp.concatenate([gb[:, base + h:base + h + 1].reshape(G, C, 1) for h in range(H)], axis=0)

    beta = jnp.broadcast_to(colv(d0), (H * G, C, HEAD_DIM))
    gcc = jnp.broadcast_to(colv(8 + d0), (H * G, C, HEAD_DIM))
    gl = jnp.broadcast_to(colv(16 + d0), (H * G, C, HEAD_DIM))
    grow = jnp.concatenate([gbt[8 + d0 + h:9 + d0 + h, c * C:(c + 1) * C].reshape(1, 1, C)
                            for h in range(H) for c in range(G)], axis=0)
    ii = lax.broadcasted_iota(jnp.int32, (1, C, C), 1)
    jj = lax.broadcasted_iota(jnp.int32, (1, C, C), 2)
    incl = (jj >= ii) if rev else (jj <= ii)
    strict = (jj > ii) if rev else (jj < ii)
    decay = jnp.exp(jnp.where(incl, gcc - grow, NEG_BIG))
    kk = _bmm_nt(k, k)
    qk = _bmm_nt(q, k)
    yield
    nmat = jnp.where(strict, kk * decay, 0.0) * beta
    eg = jnp.exp(gcc)
    rhs = jnp.concatenate([v * beta, kf * (beta * eg)], axis=-1)
    same_blk = (ii // GDN_SUB) == (jj // GDN_SUB)
    n_d = jnp.where(same_blk, nmat, 0.0)
    n_off = nmat - n_d
    t = jnp.where(ii == jj, 1.0, 0.0) - n_d
    p = _bmm(n_d, n_d)
    yield
    for _ in range(GDN_SUB.bit_length() - 3):
        m = _bmm(jnp.concatenate([t, p], axis=1), p)
        yield
        t = t + m[:, 0:C]
        p = m[:, C:2 * C]
    t = t + _bmm(t, p)
    yield
    m = _bmm(t, jnp.concatenate([rhs, n_off], axis=-1))
    yield
    y = m[:, :, 0:2 * HEAD_DIM]
    mm = m[:, :, 2 * HEAD_DIM:]
    m = _bmm(mm, jnp.concatenate([y, mm], axis=-1))
    yield
    x = y - m[:, :, 0:2 * HEAD_DIM]
    p = m[:, :, 2 * HEAD_DIM:]
    for _ in range((C // GDN_SUB).bit_length() - 3):
        m = _bmm(p, jnp.concatenate([x, p], axis=-1))
        yield
        x = x + m[:, :, 0:2 * HEAD_DIM]
        p = m[:, :, 2 * HEAD_DIM:]
    x = x + _bmm(p, x)
    yield
    u = x[:, :, 0:HEAD_DIM]
    w = x[:, :, HEAD_DIM:2 * HEAD_DIM]
    qkm = jnp.where(incl, qk * decay, 0.0)
    k_tail_t = jnp.swapaxes(kf * jnp.exp(gl - gcc), 1, 2)
    lhs1 = jnp.concatenate([w, qf * eg], axis=1).astype(BF16)
    lhs2 = jnp.concatenate([qkm, k_tail_t], axis=1).astype(BF16)
    dch = jnp.exp(gl[:, 0:8, :])
    out.extend([u, lhs1, lhs2, dch])


def _gdn_scan_group(u_ref, l1_ref, l2_ref, d_ref, s_ref, o_ref, rev):
    G, C, H = GDN_G, GDN_CHUNK, GDN_HEADS
    S = s_ref[...]
    for c in (range(G - 1, -1, -1) if rev else range(G)):
        r1 = _bmm(l1_ref[:, c], S)
        yield
        vn = u_ref[:, c] - r1[:, 0:C]
        r2 = _bmm(l2_ref[:, c], vn)
        yield
        oc = r1[:, C:2 * C] + r2[:, 0:C]
        S = S * d_ref[:, c][:, 0:1, :] + r2[:, C:2 * C]
        o_ref[c * C:(c + 1) * C, :] = jnp.concatenate([oc[h] for h in range(H)], axis=-1).astype(BF16)
    s_ref[...] = S


def _gdn_kernel(qkvf_ref, gbf_ref, gbtf_ref, qkvb_ref, gbb_ref, gbtb_ref, of_ref, ob_ref,
                sf_ref, sb_ref, uf_ref, l1f_ref, l2f_ref, df_ref, ub_ref, l1b_ref, l2b_ref, db_ref):
    G, C, H = GDN_G, GDN_CHUNK, GDN_HEADS
    step = pl.program_id(0)

    @pl.when(step == 0)
    def _():
        for r in (sf_ref, sb_ref, uf_ref, l1f_ref, l2f_ref, df_ref, ub_ref, l1b_ref, l2b_ref, db_ref):
            r[...] = jnp.zeros_like(r)

    terms_f, terms_b = [], []
    for strands in ([_gdn_chunk_terms(qkvf_ref[...], gbf_ref[...], gbtf_ref[...], False, terms_f),
                     _gdn_scan_group(uf_ref, l1f_ref, l2f_ref, df_ref, sf_ref, of_ref, False)],
                    [_gdn_chunk_terms(qkvb_ref[...], gbb_ref[...], gbtb_ref[...], True, terms_b),
                     _gdn_scan_group(ub_ref, l1b_ref, l2b_ref, db_ref, sb_ref, ob_ref, True)]):
        while strands:
            strands = [s for s in strands if next(s, True) is None]
    for (u, l1, l2, dch), (u_ref, l1_ref, l2_ref, d_ref) in ((terms_f, (uf_ref, l1f_ref, l2f_ref, df_ref)),
                                                          (terms_b, (ub_ref, l1b_ref, l2b_ref, db_ref))):
        u_ref[...] = u.reshape(H, G, C, HEAD_DIM)
        l1_ref[...] = l1.reshape(H, G, 2 * C, HEAD_DIM)
        l2_ref[...] = l2.reshape(H, G, 2 * C, C)
        d_ref[...] = dch.reshape(H, G, 8, HEAD_DIM)


def _gdn(qkv, gb, gbt):
    L = qkv.shape[1]
    G, C, H = GDN_G, GDN_CHUNK, GDN_HEADS
    T = G * C
    n = L // T
    fin = lambda s: jnp.minimum(s, n - 1)
    bin_ = lambda s: jnp.maximum(n - 1 - s, 0)
    fout = lambda s: jnp.maximum(s - 1, 0)
    bout = lambda s: jnp.minimum(n - s, n - 1)
    term_scratch = [pltpu.VMEM((H, G, C, HEAD_DIM), F32), pltpu.VMEM((H, G, 2 * C, HEAD_DIM), BF16),
                    pltpu.VMEM((H, G, 2 * C, C), BF16), pltpu.VMEM((H, G, 8, HEAD_DIM), F32)]
    return pl.pallas_call(
        _gdn_kernel,
        grid=(n + 1,),
        in_specs=[pl.BlockSpec((3 * H, T, HEAD_DIM), lambda s: (0, fin(s), 0)),
                  pl.BlockSpec((T, LANES), lambda s: (fin(s), 0)),
                  pl.BlockSpec((24, T), lambda s: (0, fin(s))),
                  pl.BlockSpec((3 * H, T, HEAD_DIM), lambda s: (0, bin_(s), 0)),
                  pl.BlockSpec((T, LANES), lambda s: (bin_(s), 0)),
                  pl.BlockSpec((24, T), lambda s: (0, bin_(s)))],
        out_specs=[pl.BlockSpec((T, 256), lambda s: (fout(s), 0)),
                   pl.BlockSpec((T, 256), lambda s: (bout(s), 0))],
        out_shape=[jax.ShapeDtypeStruct((L, 256), BF16), jax.ShapeDtypeStruct((L, 256), BF16)],
        scratch_shapes=[pltpu.VMEM((H, HEAD_DIM, HEAD_DIM), F32), pltpu.VMEM((H, HEAD_DIM, HEAD_DIM), F32)]
                       + term_scratch + term_scratch,
        compiler_params=_cparams(("arbitrary",)),
        name="gdn_scan",
    )(qkv, gb, gbt, qkv, gb, gbt)


def _merge_kernel(x_ref, yna_ref, yswa_ref, of_ref, ob_ref, z_ref, gate_ref, gn_ref, bd_ref,
                  wna_ref, wswa_ref, wgdn_ref, wout_ref, o_ref):
    o = of_ref[...].astype(F32) + ob_ref[...].astype(F32)
    z = z_ref[...].astype(F32)
    ms = _group_sum(o * o, bd_ref) * (1.0 / HEAD_DIM)
    ygdn = o * lax.rsqrt(ms + NORM_EPS) * gn_ref[...] * (z * _sigmoid(z))
    m = (gate_ref[:, 0:D_MODEL].astype(F32) * _dot(yna_ref[...], wna_ref[...])
         + gate_ref[:, D_MODEL:2 * D_MODEL].astype(F32) * _dot(yswa_ref[...], wswa_ref[...])
         + gate_ref[:, 2 * D_MODEL:3 * D_MODEL].astype(F32) * _dot(ygdn.astype(BF16), wgdn_ref[...]))
    o_ref[...] = x_ref[...] + _dot(m.astype(BF16), wout_ref[...])


def _merge(x, yna, yswa, of, ob, z, gate, l, gn, bd, wna, wswa, wgdn, wout):
    L = x.shape[0]
    row = lambda w_: pl.BlockSpec((TM, w_), lambda i: (i, 0))
    return pl.pallas_call(
        _merge_kernel,
        grid=(L // TM,),
        in_specs=[row(D_MODEL), row(256), row(512), row(256), row(256), row(256), row(SEG_GATE),
                  _layer_spec(gn, l), _const_spec(bd), _layer_spec(wna, l), _layer_spec(wswa, l),
                  _layer_spec(wgdn, l), _layer_spec(wout, l)],
        out_specs=row(D_MODEL),
        out_shape=jax.ShapeDtypeStruct((L, D_MODEL), F32),
        compiler_params=_cparams(("parallel",)),
        name="merge",
    )(x, yna, yswa, of, ob, z, gate, gn, bd, wna, wswa, wgdn, wout)


def _ffn_kernel(x_ref, xp_ref, xn_ref, g_ref, wu_ref, cw_ref, cb_ref, wd_ref, o_ref, hbuf, *, nblk):
    i = pl.program_id(0)

    def normed(xv):
        ms = jnp.mean(xv * xv, axis=-1, keepdims=True)
        return xv * lax.rsqrt(ms + NORM_EPS) * g_ref[...]

    hbuf[0:HALO, :] = jnp.where(i > 0, normed(xp_ref[...]), 0.0).astype(BF16)
    hbuf[HALO:HALO + TM, :] = normed(x_ref[...]).astype(BF16)
    hbuf[HALO + TM:, :] = jnp.where(i < nblk - 1, normed(xn_ref[...]), 0.0).astype(BF16)
    h = hbuf[...]
    n = TM + 2 * HALO
    chunks = [slice(a, b) for a, b in zip(FFN_SPLIT[:-1], FFN_SPLIT[1:])]
    gate_cols = [slice(D_FF + a, D_FF + b) for a, b in zip(FFN_SPLIT[:-1], FFN_SPLIT[1:])]

    def conv(u, cols):
        y = (pltpu.roll(u, 1, 0) * cw_ref[0:1, cols] + u * cw_ref[1:2, cols]
             + pltpu.roll(u, n - 1, 0) * cw_ref[2:3, cols])
        return y[HALO:HALO + TM] + cb_ref[:, cols]

    ups = [(_dot(h, wu_ref[:, ca]), _dot(h, wu_ref[:, cb])) for ca, cb in zip(chunks, gate_cols)]
    acc = x_ref[...]
    for (ua, ub), ca, cb in zip(ups, chunks, gate_cols):
        a = conv(ua, ca)
        gated = (a * _sigmoid(a) * conv(ub, cb)).astype(BF16)
        acc = acc + _dot(gated, wd_ref[ca, :])
    o_ref[...] = acc


def _ffn(x, l, g, w_up, conv_w, conv_b, w_down):
    L = x.shape[0]
    nblk = L // TM
    hb = TM // HALO
    once = pl.Buffered(1)
    return pl.pallas_call(
        functools.partial(_ffn_kernel, nblk=nblk),
        grid=(nblk,),
        in_specs=[pl.BlockSpec((TM, D_MODEL), lambda i: (i, 0)),
                  pl.BlockSpec((HALO, D_MODEL), lambda i: (jnp.maximum(i * hb - 1, 0), 0)),
                  pl.BlockSpec((HALO, D_MODEL), lambda i: (jnp.minimum((i + 1) * hb, L // HALO - 1), 0)),
                  _layer_spec(g, l), _layer_spec(w_up, l, pipeline_mode=once), _layer_spec(conv_w, l),
                  _layer_spec(conv_b, l), _layer_spec(w_down, l, pipeline_mode=once)],
        out_specs=pl.BlockSpec((TM, D_MODEL), lambda i: (i, 0)),
        out_shape=jax.ShapeDtypeStruct((L, D_MODEL), F32),
        scratch_shapes=[pltpu.VMEM((TM + 2 * HALO, D_MODEL), BF16)],
        compiler_params=_cparams(("parallel",)),
        name="ffn",
    )(x, x, x, g, w_up, conv_w, conv_b, w_down)


PACK_SEGMENTS = ((0, 256, 0), (256, 512, 256), (768, 1280, 512), (1280, 1408, 1024),
                 (1536, 2304, 1152), (512, 768, 1920), (2304, 2560, 2176), (1408, 1536, 2432),
                 (2576, 5648, 2560), (2560, 2576, 5632))
PACK_K = 256
PACK_PIECE = 512


def _pack_kernel(w_ref, o_ref):
    for s0, s1, d0 in PACK_SEGMENTS[:-1]:
        for p0 in range(s0, s1, PACK_PIECE):
            p1 = min(p0 + PACK_PIECE, s1)
            o_ref[:, d0 + p0 - s0:d0 + p1 - s0] = w_ref[p0:p1, :].T.astype(BF16)
    s0, s1, d0 = PACK_SEGMENTS[-1]
    lane = lax.broadcasted_iota(jnp.int32, (1, LANES), 1)
    o_ref[:, d0:d0 + LANES] = jnp.where(lane < s1 - s0, w_ref[s0:s0 + LANES, :].T, 0.0).astype(BF16)


def _pack_w_in(w):
    depth, k, n = w.shape
    return pl.pallas_call(
        _pack_kernel,
        grid=(depth, k // PACK_K),
        in_specs=[pl.BlockSpec((None, n, PACK_K), lambda l, i: (l, 0, i))],
        out_specs=pl.BlockSpec((None, PACK_K, IN_PACKED), lambda l, i: (l, i, 0)),
        out_shape=jax.ShapeDtypeStruct((depth, k, IN_PACKED), BF16),
        compiler_params=_cparams(("parallel", "parallel")),
        name="pack_w_in",
    )(jnp.swapaxes(w, 1, 2))


def _rope_tables(L):
    inv = 1.0 / (ROPE_THETA ** (np.arange(0, HEAD_DIM, 2, dtype=np.float64) / HEAD_DIM))
    freq = np.tile(inv, LANES // inv.size)
    sign = np.where(np.arange(LANES) % HEAD_DIM < HEAD_DIM // 2, -1.0, 1.0)
    start = np.arange(L // TM, dtype=np.float64)[:, None] * TM * freq[None, :]
    off = np.arange(TM, dtype=np.float64)[:, None] * freq[None, :]
    blk = np.stack([np.cos(start), np.sin(start)], axis=1)
    offs = np.stack([np.cos(off), np.sin(off), sign * np.cos(off), sign * np.sin(off)])
    return jnp.asarray(blk, F32), jnp.asarray(offs, F32)


def kernel(x, attn_norm, w_in, qk_norm, na_rpb, swa_sink, gdn_conv_w, gdn_a_log, gdn_dt_bias, gdn_norm,
           w_branch_na, w_branch_swa, w_branch_gdn, w_out, ffn_norm, w_up, ffn_conv_w, ffn_conv_b, w_down):
    B, L, D = x.shape
    assert B == 1 and D == D_MODEL and L % (NA_ROWS * GRID_W) == 0 and L // GRID_W >= 2 * NA_ROWS
    depth = w_in.shape[0]
    rope_blk, rope_off = _rope_tables(L)
    blockdiag = jnp.asarray(np.kron(np.eye(4), np.ones((HEAD_DIM, HEAD_DIM))), BF16)
    bias_tabs = _na_bias_tables(na_rpb, L // GRID_W)
    swa_band = _swa_band_tables()
    scale = HEAD_DIM ** -0.5 * LOG2E
    G = SWA_Q_HEADS // SWA_KV_HEADS
    gain = jnp.concatenate([jnp.tile(qk_norm[:, 0] * scale, (1, NA_HEADS)), jnp.tile(qk_norm[:, 1], (1, NA_HEADS)),
                            jnp.tile(qk_norm[:, 2] * scale, (1, SWA_Q_HEADS)),
                            jnp.tile(qk_norm[:, 3], (1, SWA_KV_HEADS))], axis=1)[:, None, :]
    sink_row = jnp.repeat(swa_sink.reshape(depth, SWA_Q_HEADS // 2, 2) * LOG2E, SWA_BLOCK, axis=2)[:, :, None, :]
    alog_row = jnp.pad(gdn_a_log.reshape(depth, 1, 8), ((0, 0), (0, 0), (8, LANES - 16)))
    dtb_row = jnp.pad(gdn_dt_bias.reshape(depth, 1, 8), ((0, 0), (0, 0), (8, LANES - 16)))
    gdn_gain = jnp.tile(gdn_norm, (1, GDN_HEADS))[:, None, :]
    attn_g = attn_norm[:, None, :]
    ffn_g = ffn_norm[:, None, :]
    ffn_b = ffn_conv_b[:, None, :]
    w_in_p = _pack_w_in(w_in)
    w_na, w_swa, w_gdn = w_branch_na.astype(BF16), w_branch_swa.astype(BF16), w_branch_gdn.astype(BF16)
    w_o, w_u, w_d = w_out.astype(BF16), w_up.astype(BF16), w_down.astype(BF16)
    xs = x[0]
    for l in range(depth):
        qk, z, gate, va_t, vs_t, qkv_n, gb, gbt = _inproj(xs, l, attn_g, w_in_p, gain, rope_blk, rope_off, blockdiag,
                                                          gdn_conv_w, alog_row, dtb_row)
        y_na = _na_attention(qk, va_t, l, bias_tabs)
        y_swa = _swa_attention(qk, vs_t, l, swa_band, sink_row)
        o_f, o_b = _gdn(qkv_n, gb, gbt)
        xs = _merge(xs, y_na, y_swa, o_f, o_b, z, gate, l, gdn_gain, blockdiag, w_na, w_swa, w_gdn, w_o)
        xs = _ffn(xs, l, ffn_g, w_u, ffn_conv_w, ffn_b, w_d)
    return xs[None]
```

```python
import functools

import numpy as np
import jax
import jax.numpy as jnp
from jax import lax
from jax.experimental import pallas as pl
from jax.experimental.pallas import tpu as pltpu

F32 = jnp.float32
BF16 = jnp.bfloat16

D_MODEL = 1024
HEAD_DIM = 64
GRID_W = 64
NORM_EPS = 1e-6
NA_HEADS = 4
NA_KH = 8
NA_KW = 16
SWA_Q_HEADS = 8
SWA_KV_HEADS = 2
SWA_BLOCK = 128
ROPE_THETA = 10000.0
GDN_HEADS = 4
GDN_CHUNK = 64
D_FF = 2816

LANES = 128
NEG_BIG = -1e30
LOG2E = 1.4426950408889634
VMEM_LIMIT = 56 * 1024 * 1024

SEG_QK = 1152
SEG_PLAIN = 1408
PLAIN_VA = 768
PLAIN_Z = 1024
PLAIN_VS = 1280
SEG_GATE = 3072
SEG_BA = 128
IN_PACKED = SEG_QK + SEG_PLAIN + SEG_GATE + SEG_BA

TM = 512
NA_ROWS = 32
NA_WIN = NA_KH + 2
SWA_QB = 16
GDN_G = 4
GDN_SUB = 8
MXU_TILE = 256
FFN_SPLIT = (0, 6 * MXU_TILE, D_FF)
HALO = 8


def _cparams(sem):
    return pltpu.CompilerParams(dimension_semantics=sem, vmem_limit_bytes=VMEM_LIMIT)


def _dot(a, b):
    return jnp.dot(a, b, preferred_element_type=F32)


def _sigmoid(x):
    return 1.0 / (1.0 + jnp.exp(-x))


def _group_sum(sq, bd_ref):
    w = sq.shape[-1]
    return _dot(sq.astype(BF16), bd_ref[0:w, 0:w])


def _gdn_token_qkv(x, row_prev, row_next, cw_ref, bd_ref, qkv_ref):
    rid = lax.broadcasted_iota(jnp.int32, (TM, 1), 0)
    xprev = jnp.where(rid == 0, row_prev, pltpu.roll(x, 1, 0))
    xnext = jnp.where(rid == TM - 1, row_next, pltpu.roll(x, TM - 1, 0))
    y = xprev * cw_ref[0:1, :] + x * cw_ref[1:2, :] + xnext * cw_ref[2:3, :]
    y = y * _sigmoid(y)
    for part in range(3):
        yp = y[:, part * 256:(part + 1) * 256]
        if part < 2:
            yp = yp * lax.rsqrt(_group_sum(yp * yp, bd_ref) + NORM_EPS)
        if part == 0:
            yp = yp * (HEAD_DIM ** -0.5)
        yp = yp.astype(BF16)
        for hh in range(GDN_HEADS):
            qkv_ref[part * GDN_HEADS + hh] = yp[:, hh * HEAD_DIM:(hh + 1) * HEAD_DIM]


def _gdn_token_gates(ba, alog_ref, dtb_ref, gb_ref, gbt_ref):
    rid = lax.broadcasted_iota(jnp.int32, (TM, 1), 0)
    lane = lax.broadcasted_iota(jnp.int32, (1, LANES), 1)
    beta = _sigmoid(ba)
    sp_in = ba + dtb_ref[...]
    softplus = jnp.maximum(sp_in, 0.0) + jnp.log(1.0 + jnp.exp(-jnp.abs(sp_in)))
    g = jnp.where((lane >= 8) & (lane < 16), -jnp.exp(alog_ref[...]) * softplus, 0.0)
    rc = rid % GDN_CHUNK
    pre = g
    suf = g
    s = 1
    while s < GDN_CHUNK:
        pre = pre + jnp.where(rc >= s, pltpu.roll(pre, s, 0), 0.0)
        suf = suf + jnp.where(rc < GDN_CHUNK - s, pltpu.roll(suf, TM - s, 0), 0.0)
        s *= 2
    tot = pre + suf - g
    gc = jnp.where(lane < 12, pre, suf)
    slab = jnp.where(lane < 8, beta, jnp.where(lane < 16, gc, jnp.where(lane < 24, pltpu.roll(tot, 8, 1), 0.0)))
    gb_ref[...] = slab
    gbt_ref[...] = slab.T[0:24, :]


def _inproj_kernel(x_ref, xp_ref, xn_ref, g_ref, w_ref, gain_ref, rope_blk_ref, rope_off_ref, bd_ref,
                   cw_ref, alog_ref, dtb_ref,
                   oqk_ref, oz_ref, ogate_ref, ovat_ref, ovst_ref, oqkvh_ref, ogb_ref, ogbt_ref, *, nblk):
    i = pl.program_id(0)

    def normed(xv):
        ms = jnp.mean(xv * xv, axis=-1, keepdims=True)
        return xv * lax.rsqrt(ms + NORM_EPS) * g_ref[...]

    h = normed(x_ref[...]).astype(BF16)
    h_halo = jnp.concatenate([jnp.where(i > 0, normed(xp_ref[...]), 0.0),
                              jnp.where(i < nblk - 1, normed(xn_ref[...]), 0.0)], axis=0).astype(BF16)
    cw = 2 * LANES
    lane = lax.broadcasted_iota(jnp.int32, (1, cw), 1)
    first_half = (lane % HEAD_DIM) < (HEAD_DIM // 2)
    c_a, s_a = rope_blk_ref[0:1, :], rope_blk_ref[1:2, :]
    cos128 = c_a * rope_off_ref[0] - s_a * rope_off_ref[1]
    sin128 = s_a * rope_off_ref[2] + c_a * rope_off_ref[3]
    cos = jnp.concatenate([cos128, cos128], axis=1)
    sin = jnp.concatenate([sin128, sin128], axis=1)

    def head_norm(t, c):
        cols = slice(c * cw, min((c + 1) * cw, SEG_QK))
        w = cols.stop - cols.start
        tc = t[:, cols]
        ss = _group_sum(tc * tc, bd_ref)
        y = tc * lax.rsqrt(ss * (1.0 / HEAD_DIM) + NORM_EPS) * gain_ref[:, cols]
        if cols.start >= 512:
            rot = jnp.where(first_half[:, 0:w], pltpu.roll(y, w - HEAD_DIM // 2, 1), pltpu.roll(y, HEAD_DIM // 2, 1))
            y = y * cos[:, 0:w] + rot * sin[:, 0:w]
        oqk_ref[:, cols] = y.astype(BF16)

    o_plain, o_gate, o_ba = SEG_QK, SEG_QK + SEG_PLAIN, SEG_QK + SEG_PLAIN + SEG_GATE
    gate_cols = [slice(o_gate + c * D_MODEL, o_gate + (c + 1) * D_MODEL) for c in range(3)]
    out_cols = [slice(c * D_MODEL, (c + 1) * D_MODEL) for c in range(3)]
    t = _dot(h, w_ref[:, 0:SEG_QK])
    t_gdn = _dot(h, w_ref[:, o_plain:o_plain + 768])
    tba = _dot(h, w_ref[:, o_ba:o_ba + SEG_BA])
    t_halo = _dot(h_halo, w_ref[:, o_plain:o_plain + 768])
    for c in (0, 1):
        head_norm(t, c)
    tg0 = _dot(h, w_ref[:, gate_cols[0]])
    for c in (2, 3, 4):
        head_norm(t, c)
    tg1 = _dot(h, w_ref[:, gate_cols[1]])
    ogate_ref[:, out_cols[0]] = _sigmoid(tg0).astype(BF16)
    _gdn_token_qkv(t_gdn, t_halo[HALO - 1:HALO, :], t_halo[HALO:HALO + 1, :], cw_ref, bd_ref, oqkvh_ref)
    tg2 = _dot(h, w_ref[:, gate_cols[2]])
    ogate_ref[:, out_cols[1]] = _sigmoid(tg1).astype(BF16)
    _gdn_token_gates(tba, alog_ref, dtb_ref, ogb_ref, ogbt_ref)
    t_rest = _dot(h, w_ref[:, o_plain + PLAIN_VA:o_plain + SEG_PLAIN])
    ogate_ref[:, out_cols[2]] = _sigmoid(tg2).astype(BF16)
    oz_ref[...] = t_rest[:, PLAIN_Z - PLAIN_VA:PLAIN_Z - PLAIN_VA + 256].astype(BF16)
    for grp in range(TM // LANES):
        rows = slice(grp * LANES, (grp + 1) * LANES)
        ovat_ref[grp] = t_rest[rows, 0:256].T.astype(BF16)
        ovst_ref[grp] = t_rest[rows, PLAIN_VS - PLAIN_VA:PLAIN_VS - PLAIN_VA + 128].T.astype(BF16)


def _layer_spec(a, l, **kw):
    nd = a.ndim - 1
    return pl.BlockSpec((None,) + a.shape[1:], lambda *_: (l,) + (0,) * nd, **kw)


def _const_spec(a):
    return pl.BlockSpec(a.shape, lambda *_: (0,) * a.ndim)


def _inproj(x, l, g, w, gain, rope_blk, rope_off, bd, conv_w, alog_row, dtb_row):
    L = x.shape[0]
    nblk = L // TM
    hb = TM // HALO
    row = lambda w_: pl.BlockSpec((TM, w_), lambda i: (i, 0))
    return pl.pallas_call(
        functools.partial(_inproj_kernel, nblk=nblk),
        grid=(nblk,),
        in_specs=[row(D_MODEL),
                  pl.BlockSpec((HALO, D_MODEL), lambda i: (jnp.maximum(i * hb - 1, 0), 0)),
                  pl.BlockSpec((HALO, D_MODEL), lambda i: (jnp.minimum((i + 1) * hb, L // HALO - 1), 0)),
                  _layer_spec(g, l), _layer_spec(w, l, pipeline_mode=pl.Buffered(1)),
                  _layer_spec(gain, l), pl.BlockSpec((None, 2, LANES), lambda i: (i, 0, 0)), _const_spec(rope_off),
                  _const_spec(bd), _layer_spec(conv_w, l), _layer_spec(alog_row, l), _layer_spec(dtb_row, l)],
        out_specs=[row(SEG_QK), row(256), row(SEG_GATE),
                   pl.BlockSpec((TM // LANES, 256, LANES), lambda i: (i, 0, 0)),
                   pl.BlockSpec((TM // LANES, 128, LANES), lambda i: (i, 0, 0)),
                   pl.BlockSpec((3 * GDN_HEADS, TM, HEAD_DIM), lambda i: (0, i, 0)),
                   row(LANES),
                   pl.BlockSpec((24, TM), lambda i: (0, i))],
        out_shape=[jax.ShapeDtypeStruct((L, SEG_QK), BF16),
                   jax.ShapeDtypeStruct((L, 256), BF16),
                   jax.ShapeDtypeStruct((L, SEG_GATE), BF16),
                   jax.ShapeDtypeStruct((L // LANES, 256, LANES), BF16),
                   jax.ShapeDtypeStruct((L // LANES, 128, LANES), BF16),
                   jax.ShapeDtypeStruct((3 * GDN_HEADS, L, HEAD_DIM), BF16),
                   jax.ShapeDtypeStruct((L, LANES), F32),
                   jax.ShapeDtypeStruct((24, L), F32)],
        compiler_params=_cparams(("parallel",)),
        name="inproj",
    )(x, x, x, g, w, gain, rope_blk, rope_off, bd, conv_w, alog_row, dtb_row)


def _toeplitz_kernel(rpb_ref, onehot_ref, colmask_ref, o_ref):
    r = rpb_ref[...]
    hi = r.astype(BF16)
    r1 = r - hi.astype(F32)
    mid = r1.astype(BF16)
    lo = (r1 - mid.astype(F32)).astype(BF16)
    oh = onehot_ref[...]
    w = _dot(hi, oh) + _dot(mid, oh) + _dot(lo, oh)
    o_ref[...] = jnp.where(colmask_ref[...] > 0.0, w * LOG2E, NEG_BIG)


def _na_assemble_kernel(tiles_ref, o_ref, *, a_idx):
    neg = jnp.full((GRID_W, GRID_W), NEG_BIG, F32)
    for t in range(a_idx.shape[0]):
        for j in range(NA_WIN):
            for e in range(2):
                a = int(a_idx[t, j, e])
                o_ref[t, j * GRID_W:(j + 1) * GRID_W, e * GRID_W:(e + 1) * GRID_W] = neg if a < 0 else tiles_ref[a]


def _na_bias_tables(na_rpb, rows):
    depth, H, nr, nc = na_rpb.shape
    kc = np.arange(GRID_W)[:, None]
    qc = np.arange(GRID_W)[None, :]
    dc = np.clip(kc - qc + (NA_KW - 1), 0, 2 * NA_KW - 2).reshape(-1)
    onehot = np.zeros((32, GRID_W * GRID_W), np.float32)
    onehot[dc, np.arange(GRID_W * GRID_W)] = 1.0
    col_start = np.clip(qc - NA_KW // 2, 0, GRID_W - NA_KW)
    colmask = ((kc >= col_start) & (kc < col_start + NA_KW)).astype(np.float32).reshape(1, -1)
    rpb_rows = jnp.pad(na_rpb, ((0, 0), (0, 0), (0, 16 - nr), (0, 32 - nc))).reshape(depth * H * 16, 32)
    tiles = pl.pallas_call(
        _toeplitz_kernel,
        out_shape=jax.ShapeDtypeStruct((depth * H * 16, GRID_W * GRID_W), F32),
        name="na_bias_tiles",
    )(rpb_rows, jnp.asarray(onehot, BF16), jnp.asarray(colmask))
    tiles = tiles.reshape(depth * H, 16, GRID_W, GRID_W)
    a_idx = np.stack([_na_pair_structure(r0, rows)[1] for r0 in _na_pair_type_rows(rows)])
    nt = a_idx.shape[0]
    tabs = pl.pallas_call(
        functools.partial(_na_assemble_kernel, a_idx=a_idx),
        grid=(depth * H,),
        in_specs=[pl.BlockSpec((None, 16, GRID_W, GRID_W), lambda i: (i, 0, 0, 0))],
        out_specs=pl.BlockSpec((None, nt, NA_WIN * GRID_W, 2 * GRID_W), lambda i: (i, 0, 0, 0)),
        out_shape=jax.ShapeDtypeStruct((depth * H, nt, NA_WIN * GRID_W, 2 * GRID_W), F32),
        compiler_params=_cparams(("parallel",)),
        name="na_bias_tables",
    )(tiles)
    return tabs.reshape(depth, H, nt, NA_WIN * GRID_W, 2 * GRID_W)


def _na_pair_structure(r0, rows):
    wstart = int(np.clip(r0 - NA_KH // 2, 0, rows - NA_KH - 1)) // 2 * 2
    a_idx = np.full((NA_WIN, 2), -1, np.int64)
    for e in range(2):
        rr = r0 + e
        rs = int(np.clip(rr - NA_KH // 2, 0, rows - NA_KH))
        for j in range(NA_WIN):
            krow = wstart + j
            if rs <= krow < rs + NA_KH:
                a_idx[j, e] = krow - rr + NA_KH - 1
    return wstart, a_idx


def _na_pair_type_rows(rows):
    reps = [0, 2, 4, rows - 4, rows - 2]
    for r0 in range(0, rows, 2):
        t = 3 + (r0 - (rows - 4)) // 2 if r0 >= rows - 4 else min(r0 // 2, 2)
        ws, a = _na_pair_structure(r0, rows)
        ws_t, a_t = _na_pair_structure(reps[t], rows)
        assert (a == a_t).all() and r0 - ws == reps[t] - ws_t
    return reps


def _na_kernel(q_ref, kp_ref, kc_ref, kn_ref, vp_ref, vc_ref, vn_ref, bias_ref, o_ref, kbuf, vbuf, *, rows):
    i = pl.program_id(0)
    blk = NA_ROWS * GRID_W
    grp = blk // LANES
    kbuf[0:blk, :] = kp_ref[...]
    kbuf[blk:2 * blk, :] = kc_ref[...]
    kbuf[2 * blk:3 * blk, :] = kn_ref[...]
    vbuf[0:grp] = vp_ref[...]
    vbuf[grp:2 * grp] = vc_ref[...]
    vbuf[2 * grp:3 * grp] = vn_ref[...]
    win = NA_WIN * GRID_W
    wgrp = win // LANES
    pair = 2 * GRID_W

    npair = NA_ROWS // 2
    goffs, types = [], []
    for pp in range(npair):
        r0 = i * NA_ROWS + 2 * pp
        wstart = jnp.clip(r0 - NA_KH // 2, 0, rows - NA_KH - 1) // 2 * 2
        types.append(jnp.where(r0 >= rows - 4, 3 + (r0 - (rows - 4)) // 2, jnp.minimum(r0 // 2, 2)))
        goffs.append((wstart - (i - 1) * NA_ROWS) // 2)

    def logits(pp, h):
        sl = slice(h * HEAD_DIM, (h + 1) * HEAD_DIM)
        kw = kbuf[pl.ds(pl.multiple_of(goffs[pp] * LANES, LANES), win), sl]
        s = lax.dot_general(kw, q_ref[pp * pair:(pp + 1) * pair, sl],
                            (((1,), (1,)), ((), ())), preferred_element_type=F32)
        return s + bias_ref[h, types[pp]]

    def softmax(s):
        m = jnp.max(s, axis=0, keepdims=True)
        p = jnp.exp2(s - m)
        return p.astype(BF16), 1.0 / jnp.sum(p, axis=0, keepdims=True)

    def weighted(p_inv, pp, h):
        p, inv_l = p_inv
        vt = jnp.concatenate([vbuf[goffs[pp] + j, h * HEAD_DIM:(h + 1) * HEAD_DIM, :] for j in range(wgrp)], axis=-1)
        return _dot(vt, p) * inv_l

    tiles = [(pp, h) for pp in range(npair) for h in range(NA_HEADS)]
    s_q, p_q, outs = {}, {}, {}
    lead, lag = 2, 1
    for n in range(len(tiles) + lead + lag):
        if n < len(tiles):
            s_q[n] = logits(*tiles[n])
        if lead <= n < len(tiles) + lead:
            p_q[n - lead] = softmax(s_q.pop(n - lead))
        if n >= lead + lag:
            pp, h = tiles[n - lead - lag]
            outs[(pp, h)] = weighted(p_q.pop(n - lead - lag), pp, h)
            if h == NA_HEADS - 1:
                o_t = jnp.concatenate([outs.pop((pp, hh)) for hh in range(NA_HEADS)], axis=0)
                o_ref[pp * pair:(pp + 1) * pair, :] = o_t.T.astype(BF16)


def _na_attention(qk, va_t, l, bias):
    L = qk.shape[0]
    rows = L // GRID_W
    nblk = rows // NA_ROWS
    blk = NA_ROWS * GRID_W
    grp = blk // LANES
    w = NA_HEADS * HEAD_DIM
    prev = lambda i: jnp.maximum(i - 1, 0)
    nxt = lambda i: jnp.minimum(i + 1, nblk - 1)
    spec = lambda f, c: pl.BlockSpec((blk, w), lambda i: (f(i), c))
    vspec = lambda f: pl.BlockSpec((grp, w, LANES), lambda i: (f(i), 0, 0))
    same = lambda i: i
    return pl.pallas_call(
        functools.partial(_na_kernel, rows=rows),
        grid=(nblk,),
        in_specs=[spec(same, 0),
                  spec(prev, 1), spec(same, 1), spec(nxt, 1),
                  vspec(prev), vspec(same), vspec(nxt),
                  _layer_spec(bias, l)],
        out_specs=pl.BlockSpec((blk, w), lambda i: (i, 0)),
        out_shape=jax.ShapeDtypeStruct((L, w), BF16),
        scratch_shapes=[pltpu.VMEM((3 * blk, w), BF16), pltpu.VMEM((3 * grp, w, LANES), BF16)],
        compiler_params=_cparams(("parallel",)),
        name="na_attn",
    )(qk, qk, qk, qk, va_t, va_t, va_t, bias)


def _swa_kernel(q_ref, kp_ref, kc_ref, kn_ref, vp_ref, vc_ref, vn_ref, band_ref, sink_ref, o_ref, *, nstep):
    i = pl.program_id(0)
    B = SWA_BLOCK
    G = SWA_Q_HEADS // SWA_KV_HEADS
    k_all = jnp.concatenate([kp_ref[...], kc_ref[...], kn_ref[...]], axis=0)
    vt_all = [vp_ref[0]] + [vc_ref[j] for j in range(SWA_QB)] + [vn_ref[0]]

    def band_index(b):
        if b == 0:
            return jnp.where(i == 0, 1, 0)
        if b == SWA_QB - 1:
            return jnp.where(i == nstep - 1, 2, 0)
        return 0

    def logits(b, hp):
        g = hp // (G // 2)
        ks = k_all[b * B:(b + 3) * B, g * HEAD_DIM:(g + 1) * HEAD_DIM]
        qs = jnp.concatenate([q_ref[b * B:(b + 1) * B, (2 * hp + e) * HEAD_DIM:(2 * hp + e + 1) * HEAD_DIM]
                              for e in range(2)], axis=0)
        s = lax.dot_general(ks, qs, (((1,), (1,)), ((), ())), preferred_element_type=F32)
        t = band_index(b)
        return jnp.concatenate([s[0:B] + band_ref[t, 0:B], s[B:2 * B], s[2 * B:3 * B] + band_ref[t, 2 * B:3 * B]],
                               axis=0)

    def softmax(s, hp):
        sink = sink_ref[hp]
        m = jnp.maximum(jnp.max(s, axis=0, keepdims=True), sink)
        p = jnp.exp2(s - m)
        l = jnp.sum(p, axis=0, keepdims=True) + jnp.exp2(sink - m)
        return p.astype(BF16), 1.0 / l

    def weighted(p_inv, b, hp):
        p, inv_l = p_inv
        g = hp // (G // 2)
        vt = jnp.concatenate([vt_all[b + j][g * HEAD_DIM:(g + 1) * HEAD_DIM, :] for j in range(3)], axis=-1)
        return _dot(vt, p) * inv_l

    tiles = [(b, hp) for b in range(SWA_QB) for hp in range(SWA_Q_HEADS // 2)]
    s_q, p_q = {}, {}
    lead, lag = 2, 1
    for n in range(len(tiles) + lead + lag):
        if n < len(tiles):
            s_q[n] = logits(*tiles[n])
        if lead <= n < len(tiles) + lead:
            p_q[n - lead] = softmax(s_q.pop(n - lead), tiles[n - lead][1])
        if n >= lead + lag:
            b, hp = tiles[n - lead - lag]
            o_t = weighted(p_q.pop(n - lead - lag), b, hp)
            pair_t = jnp.concatenate([o_t[:, 0:B], o_t[:, B:2 * B]], axis=0)
            o_ref[b * B:(b + 1) * B, 2 * hp * HEAD_DIM:(2 * hp + 2) * HEAD_DIM] = pair_t.T.astype(BF16)


def _swa_band_tables():
    B = SWA_BLOCK
    qi = np.arange(2 * B)[None, :] % B
    kj = np.arange(3 * B)[:, None]
    band = np.abs(kj - B - qi) <= B
    tabs = [band, band & (kj >= B), band & (kj < 2 * B)]
    return jnp.asarray(np.where(np.stack(tabs), 0.0, NEG_BIG), F32)


def _swa_attention(qk, vs_t, l, band, sink_row):
    L = qk.shape[0]
    B = SWA_BLOCK
    T = SWA_QB * B
    nstep = L // T
    nblk = L // B
    prev = lambda i: jnp.maximum(i * SWA_QB - 1, 0)
    nxt = lambda i: jnp.minimum((i + 1) * SWA_QB, nblk - 1)
    return pl.pallas_call(
        functools.partial(_swa_kernel, nstep=nstep),
        grid=(nstep,),
        in_specs=[pl.BlockSpec((T, 512), lambda i: (i, 1)),
                  pl.BlockSpec((B, LANES), lambda i: (prev(i), 8)),
                  pl.BlockSpec((T, LANES), lambda i: (i, 8)),
                  pl.BlockSpec((B, LANES), lambda i: (nxt(i), 8)),
                  pl.BlockSpec((1, LANES, B), lambda i: (prev(i), 0, 0)),
                  pl.BlockSpec((SWA_QB, LANES, B), lambda i: (i, 0, 0)),
                  pl.BlockSpec((1, LANES, B), lambda i: (nxt(i), 0, 0)),
                  _const_spec(band), _layer_spec(sink_row, l)],
        out_specs=pl.BlockSpec((T, 512), lambda i: (i, 0)),
        out_shape=jax.ShapeDtypeStruct((L, 512), BF16),
        compiler_params=_cparams(("parallel",)),
        name="swa_attn",
    )(qk, qk, qk, qk, vs_t, vs_t, vs_t, band, sink_row)


def _bmm(a, b):
    return jnp.einsum('bij,bjk->bik', a.astype(BF16), b.astype(BF16), preferred_element_type=F32)


def _bmm_nt(a, b):
    return jnp.einsum('bid,bjd->bij', a.astype(BF16), b.astype(BF16), preferred_element_type=F32)


def _gdn_chunk_terms(qkv, gb, gbt, rev, out):
    G, C, H = GDN_G, GDN_CHUNK, GDN_HEADS

    def heads(part):
        return qkv[part * H:(part + 1) * H].reshape(H * G, C, HEAD_DIM)

    q = heads(0)
    k = heads(1)
    v = heads(2).astype(F32)
    kf = k.astype(F32)
    qf = q.astype(F32)
    d0 = H if rev else 0

    def colv(base):
        return jnp.concatenate([gb[:, base + h:base + h + 1].reshape(G, C, 1) for h in range(H)], axis=0)

    beta = jnp.broadcast_to(colv(d0), (H * G, C, HEAD_DIM))
    gcc = jnp.broadcast_to(colv(8 + d0), (H * G, C, HEAD_DIM))
    gl = jnp.broadcast_to(colv(16 + d0), (H * G, C, HEAD_DIM))
    grow = jnp.concatenate([gbt[8 + d0 + h:9 + d0 + h, c * C:(c + 1) * C].reshape(1, 1, C)
                            for h in range(H) for c in range(G)], axis=0)
    ii = lax.broadcasted_iota(jnp.int32, (1, C, C), 1)
    jj = lax.broadcasted_iota(jnp.int32, (1, C, C), 2)
    incl = (jj >= ii) if rev else (jj <= ii)
    strict = (jj > ii) if rev else (jj < ii)
    decay = jnp.exp(jnp.where(incl, gcc - grow, NEG_BIG))
    kk = _bmm_nt(k, k)
    qk = _bmm_nt(q, k)
    yield
    nmat = jnp.where(strict, kk * decay, 0.0) * beta
    eg = jnp.exp(gcc)
    rhs = jnp.concatenate([v * beta, kf * (beta * eg)], axis=-1)
    same_blk = (ii // GDN_SUB) == (jj // GDN_SUB)
    n_d = jnp.where(same_blk, nmat, 0.0)
    n_off = nmat - n_d
    t = jnp.where(ii == jj, 1.0, 0.0) - n_d
    p = _bmm(n_d, n_d)
    yield
    for _ in range(GDN_SUB.bit_length() - 3):
        m = _bmm(jnp.concatenate([t, p], axis=1), p)
        yield
        t = t + m[:, 0:C]
        p = m[:, C:2 * C]
    t = t + _bmm(t, p)
    yield
    m = _bmm(t, jnp.concatenate([rhs, n_off], axis=-1))
    yield
    y = m[:, :, 0:2 * HEAD_DIM]
    mm = m[:, :, 2 * HEAD_DIM:]
    m = _bmm(mm, jnp.concatenate([y, mm], axis=-1))
    yield
    x = y - m[:, :, 0:2 * HEAD_DIM]
    p = m[:, :, 2 * HEAD_DIM:]
    for _ in range((C // GDN_SUB).bit_length() - 3):
        m = _bmm(p, jnp.concatenate([x, p], axis=-1))
        yield
        x = x + m[:, :, 0:2 * HEAD_DIM]
        p = m[:, :, 2 * HEAD_DIM:]
    x = x + _bmm(p, x)
    yield
    u = x[:, :, 0:HEAD_DIM]
    w = x[:, :, HEAD_DIM:2 * HEAD_DIM]
    qkm = jnp.where(incl, qk * decay, 0.0)
    k_tail_t = jnp.swapaxes(kf * jnp.exp(gl - gcc), 1, 2)
    lhs1 = jnp.concatenate([w, qf * eg], axis=1).astype(BF16)
    lhs2 = jnp.concatenate([qkm, k_tail_t], axis=1).astype(BF16)
    dch = jnp.exp(gl[:, 0:8, :])
    out.extend([u, lhs1, lhs2, dch])


def _gdn_scan_group(u_ref, l1_ref, l2_ref, d_ref, s_ref, o_ref, rev):
    G, C, H = GDN_G, GDN_CHUNK, GDN_HEADS
    S = s_ref[...]
    for c in (range(G - 1, -1, -1) if rev else range(G)):
        r1 = _bmm(l1_ref[:, c], S)
        yield
        vn = u_ref[:, c] - r1[:, 0:C]
        r2 = _bmm(l2_ref[:, c], vn)
        yield
        oc = r1[:, C:2 * C] + r2[:, 0:C]
        S = S * d_ref[:, c][:, 0:1, :] + r2[:, C:2 * C]
        o_ref[c * C:(c + 1) * C, :] = jnp.concatenate([oc[h] for h in range(H)], axis=-1).astype(BF16)
    s_ref[...] = S


def _gdn_kernel(qkvf_ref, gbf_ref, gbtf_ref, qkvb_ref, gbb_ref, gbtb_ref, of_ref, ob_ref,
                sf_ref, sb_ref, uf_ref, l1f_ref, l2f_ref, df_ref, ub_ref, l1b_ref, l2b_ref, db_ref):
    G, C, H = GDN_G, GDN_CHUNK, GDN_HEADS
    step = pl.program_id(0)

    @pl.when(step == 0)
    def _():
        for r in (sf_ref, sb_ref, uf_ref, l1f_ref, l2f_ref, df_ref, ub_ref, l1b_ref, l2b_ref, db_ref):
            r[...] = jnp.zeros_like(r)

    terms_f, terms_b = [], []
    for strands in ([_gdn_chunk_terms(qkvf_ref[...], gbf_ref[...], gbtf_ref[...], False, terms_f),
                     _gdn_scan_group(uf_ref, l1f_ref, l2f_ref, df_ref, sf_ref, of_ref, False)],
                    [_gdn_chunk_terms(qkvb_ref[...], gbb_ref[...], gbtb_ref[...], True, terms_b),
                     _gdn_scan_group(ub_ref, l1b_ref, l2b_ref, db_ref, sb_ref, ob_ref, True)]):
        while strands:
            strands = [s for s in strands if next(s, True) is None]
    for (u, l1, l2, dch), (u_ref, l1_ref, l2_ref, d_ref) in ((terms_f, (uf_ref, l1f_ref, l2f_ref, df_ref)),
                                                          (terms_b, (ub_ref, l1b_ref, l2b_ref, db_ref))):
        u_ref[...] = u.reshape(H, G, C, HEAD_DIM)
        l1_ref[...] = l1.reshape(H, G, 2 * C, HEAD_DIM)
        l2_ref[...] = l2.reshape(H, G, 2 * C, C)
        d_ref[...] = dch.reshape(H, G, 8, HEAD_DIM)


def _gdn(qkv, gb, gbt):
    L = qkv.shape[1]
    G, C, H = GDN_G, GDN_CHUNK, GDN_HEADS
    T = G * C
    n = L // T
    fin = lambda s: jnp.minimum(s, n - 1)
    bin_ = lambda s: jnp.maximum(n - 1 - s, 0)
    fout = lambda s: jnp.maximum(s - 1, 0)
    bout = lambda s: jnp.minimum(n - s, n - 1)
    term_scratch = [pltpu.VMEM((H, G, C, HEAD_DIM), F32), pltpu.VMEM((H, G, 2 * C, HEAD_DIM), BF16),
                    pltpu.VMEM((H, G, 2 * C, C), BF16), pltpu.VMEM((H, G, 8, HEAD_DIM), F32)]
    return pl.pallas_call(
        _gdn_kernel,
        grid=(n + 1,),
        in_specs=[pl.BlockSpec((3 * H, T, HEAD_DIM), lambda s: (0, fin(s), 0)),
                  pl.BlockSpec((T, LANES), lambda s: (fin(s), 0)),
                  pl.BlockSpec((24, T), lambda s: (0, fin(s))),
                  pl.BlockSpec((3 * H, T, HEAD_DIM), lambda s: (0, bin_(s), 0)),
                  pl.BlockSpec((T, LANES), lambda s: (bin_(s), 0)),
                  pl.BlockSpec((24, T), lambda s: (0, bin_(s)))],
        out_specs=[pl.BlockSpec((T, 256), lambda s: (fout(s), 0)),
                   pl.BlockSpec((T, 256), lambda s: (bout(s), 0))],
        out_shape=[jax.ShapeDtypeStruct((L, 256), BF16), jax.ShapeDtypeStruct((L, 256), BF16)],
        scratch_shapes=[pltpu.VMEM((H, HEAD_DIM, HEAD_DIM), F32), pltpu.VMEM((H, HEAD_DIM, HEAD_DIM), F32)]
                       + term_scratch + term_scratch,
        compiler_params=_cparams(("arbitrary",)),
        name="gdn_scan",
    )(qkv, gb, gbt, qkv, gb, gbt)


def _merge_kernel(x_ref, yna_ref, yswa_ref, of_ref, ob_ref, z_ref, gate_ref, gn_ref, bd_ref,
                  wna_ref, wswa_ref, wgdn_ref, wout_ref, o_ref):
    o = of_ref[...].astype(F32) + ob_ref[...].astype(F32)
    z = z_ref[...].astype(F32)
    ms = _group_sum(o * o, bd_ref) * (1.0 / HEAD_DIM)
    ygdn = o * lax.rsqrt(ms + NORM_EPS) * gn_ref[...] * (z * _sigmoid(z))
    m = (gate_ref[:, 0:D_MODEL].astype(F32) * _dot(yna_ref[...], wna_ref[...])
         + gate_ref[:, D_MODEL:2 * D_MODEL].astype(F32) * _dot(yswa_ref[...], wswa_ref[...])
         + gate_ref[:, 2 * D_MODEL:3 * D_MODEL].astype(F32) * _dot(ygdn.astype(BF16), wgdn_ref[...]))
    o_ref[...] = x_ref[...] + _dot(m.astype(BF16), wout_ref[...])


def _merge(x, yna, yswa, of, ob, z, gate, l, gn, bd, wna, wswa, wgdn, wout):
    L = x.shape[0]
    row = lambda w_: pl.BlockSpec((TM, w_), lambda i: (i, 0))
    return pl.pallas_call(
        _merge_kernel,
        grid=(L // TM,),
        in_specs=[row(D_MODEL), row(256), row(512), row(256), row(256), row(256), row(SEG_GATE),
                  _layer_spec(gn, l), _const_spec(bd), _layer_spec(wna, l), _layer_spec(wswa, l),
                  _layer_spec(wgdn, l), _layer_spec(wout, l)],
        out_specs=row(D_MODEL),
        out_shape=jax.ShapeDtypeStruct((L, D_MODEL), F32),
        compiler_params=_cparams(("parallel",)),
        name="merge",
    )(x, yna, yswa, of, ob, z, gate, gn, bd, wna, wswa, wgdn, wout)


def _ffn_kernel(x_ref, xp_ref, xn_ref, g_ref, wu_ref, cw_ref, cb_ref, wd_ref, o_ref, hbuf, *, nblk):
    i = pl.program_id(0)

    def normed(xv):
        ms = jnp.mean(xv * xv, axis=-1, keepdims=True)
        return xv * lax.rsqrt(ms + NORM_EPS) * g_ref[...]

    hbuf[0:HALO, :] = jnp.where(i > 0, normed(xp_ref[...]), 0.0).astype(BF16)
    hbuf[HALO:HALO + TM, :] = normed(x_ref[...]).astype(BF16)
    hbuf[HALO + TM:, :] = jnp.where(i < nblk - 1, normed(xn_ref[...]), 0.0).astype(BF16)
    h = hbuf[...]
    n = TM + 2 * HALO
    chunks = [slice(a, b) for a, b in zip(FFN_SPLIT[:-1], FFN_SPLIT[1:])]
    gate_cols = [slice(D_FF + a, D_FF + b) for a, b in zip(FFN_SPLIT[:-1], FFN_SPLIT[1:])]

    def conv(u, cols):
        y = (pltpu.roll(u, 1, 0) * cw_ref[0:1, cols] + u * cw_ref[1:2, cols]
             + pltpu.roll(u, n - 1, 0) * cw_ref[2:3, cols])
        return y[HALO:HALO + TM] + cb_ref[:, cols]

    ups = [(_dot(h, wu_ref[:, ca]), _dot(h, wu_ref[:, cb])) for ca, cb in zip(chunks, gate_cols)]
    acc = x_ref[...]
    for (ua, ub), ca, cb in zip(ups, chunks, gate_cols):
        a = conv(ua, ca)
        gated = (a * _sigmoid(a) * conv(ub, cb)).astype(BF16)
        acc = acc + _dot(gated, wd_ref[ca, :])
    o_ref[...] = acc


def _ffn(x, l, g, w_up, conv_w, conv_b, w_down):
    L = x.shape[0]
    nblk = L // TM
    hb = TM // HALO
    once = pl.Buffered(1)
    return pl.pallas_call(
        functools.partial(_ffn_kernel, nblk=nblk),
        grid=(nblk,),
        in_specs=[pl.BlockSpec((TM, D_MODEL), lambda i: (i, 0)),
                  pl.BlockSpec((HALO, D_MODEL), lambda i: (jnp.maximum(i * hb - 1, 0), 0)),
                  pl.BlockSpec((HALO, D_MODEL), lambda i: (jnp.minimum((i + 1) * hb, L // HALO - 1), 0)),
                  _layer_spec(g, l), _layer_spec(w_up, l, pipeline_mode=once), _layer_spec(conv_w, l),
                  _layer_spec(conv_b, l), _layer_spec(w_down, l, pipeline_mode=once)],
        out_specs=pl.BlockSpec((TM, D_MODEL), lambda i: (i, 0)),
        out_shape=jax.ShapeDtypeStruct((L, D_MODEL), F32),
        scratch_shapes=[pltpu.VMEM((TM + 2 * HALO, D_MODEL), BF16)],
        compiler_params=_cparams(("parallel",)),
        name="ffn",
    )(x, x, x, g, w_up, conv_w, conv_b, w_down)


PACK_SEGMENTS = ((0, 256, 0), (256, 512, 256), (768, 1280, 512), (1280, 1408, 1024),
                 (1536, 2304, 1152), (512, 768, 1920), (2304, 2560, 2176), (1408, 1536, 2432),
                 (2576, 5648, 2560), (2560, 2576, 5632))
PACK_K = 256
PACK_PIECE = 512


def _pack_kernel(w_ref, o_ref):
    for s0, s1, d0 in PACK_SEGMENTS[:-1]:
        for p0 in range(s0, s1, PACK_PIECE):
            p1 = min(p0 + PACK_PIECE, s1)
            o_ref[:, d0 + p0 - s0:d0 + p1 - s0] = w_ref[p0:p1, :].T.astype(BF16)
    s0, s1, d0 = PACK_SEGMENTS[-1]
    lane = lax.broadcasted_iota(jnp.int32, (1, LANES), 1)
    o_ref[:, d0:d0 + LANES] = jnp.where(lane < s1 - s0, w_ref[s0:s0 + LANES, :].T, 0.0).astype(BF16)


def _pack_w_in(w):
    depth, k, n = w.shape
    return pl.pallas_call(
        _pack_kernel,
        grid=(depth, k // PACK_K),
        in_specs=[pl.BlockSpec((None, n, PACK_K), lambda l, i: (l, 0, i))],
        out_specs=pl.BlockSpec((None, PACK_K, IN_PACKED), lambda l, i: (l, i, 0)),
        out_shape=jax.ShapeDtypeStruct((depth, k, IN_PACKED), BF16),
        compiler_params=_cparams(("parallel", "parallel")),
        name="pack_w_in",
    )(jnp.swapaxes(w, 1, 2))


def _rope_tables(L):
    inv = 1.0 / (ROPE_THETA ** (np.arange(0, HEAD_DIM, 2, dtype=np.float64) / HEAD_DIM))
    freq = np.tile(inv, LANES // inv.size)
    sign = np.where(np.arange(LANES) % HEAD_DIM < HEAD_DIM // 2, -1.0, 1.0)
    start = np.arange(L // TM, dtype=np.float64)[:, None] * TM * freq[None, :]
    off = np.arange(TM, dtype=np.float64)[:, None] * freq[None, :]
    blk = np.stack([np.cos(start), np.sin(start)], axis=1)
    offs = np.stack([np.cos(off), np.sin(off), sign * np.cos(off), sign * np.sin(off)])
    return jnp.asarray(blk, F32), jnp.asarray(offs, F32)


def kernel(x, attn_norm, w_in, qk_norm, na_rpb, swa_sink, gdn_conv_w, gdn_a_log, gdn_dt_bias, gdn_norm,
           w_branch_na, w_branch_swa, w_branch_gdn, w_out, ffn_norm, w_up, ffn_conv_w, ffn_conv_b, w_down):
    B, L, D = x.shape
    assert B == 1 and D == D_MODEL and L % (NA_ROWS * GRID_W) == 0 and L // GRID_W >= 2 * NA_ROWS
    depth = w_in.shape[0]
    rope_blk, rope_off = _rope_tables(L)
    blockdiag = jnp.asarray(np.kron(np.eye(4), np.ones((HEAD_DIM, HEAD_DIM))), BF16)
    bias_tabs = _na_bias_tables(na_rpb, L // GRID_W)
    swa_band = _swa_band_tables()
    scale = HEAD_DIM ** -0.5 * LOG2E
    G = SWA_Q_HEADS // SWA_KV_HEADS
    gain = jnp.concatenate([jnp.tile(qk_norm[:, 0] * scale, (1, NA_HEADS)), jnp.tile(qk_norm[:, 1], (1, NA_HEADS)),
                            jnp.tile(qk_norm[:, 2] * scale, (1, SWA_Q_HEADS)),
                            jnp.tile(qk_norm[:, 3], (1, SWA_KV_HEADS))], axis=1)[:, None, :]
    sink_row = jnp.repeat(swa_sink.reshape(depth, SWA_Q_HEADS // 2, 2) * LOG2E, SWA_BLOCK, axis=2)[:, :, None, :]
    alog_row = jnp.pad(gdn_a_log.reshape(depth, 1, 8), ((0, 0), (0, 0), (8, LANES - 16)))
    dtb_row = jnp.pad(gdn_dt_bias.reshape(depth, 1, 8), ((0, 0), (0, 0), (8, LANES - 16)))
    gdn_gain = jnp.tile(gdn_norm, (1, GDN_HEADS))[:, None, :]
    attn_g = attn_norm[:, None, :]
    ffn_g = ffn_norm[:, None, :]
    ffn_b = ffn_conv_b[:, None, :]
    w_in_p = _pack_w_in(w_in)
    w_na, w_swa, w_gdn = w_branch_na.astype(BF16), w_branch_swa.astype(BF16), w_branch_gdn.astype(BF16)
    w_o, w_u, w_d = w_out.astype(BF16), w_up.astype(BF16), w_down.astype(BF16)
    xs = x[0]
    for l in range(depth):
        qk, z, gate, va_t, vs_t, qkv_n, gb, gbt = _inproj(xs, l, attn_g, w_in_p, gain, rope_blk, rope_off, blockdiag,
                                                          gdn_conv_w, alog_row, dtb_row)
        y_na = _na_attention(qk, va_t, l, bias_tabs)
        y_swa = _swa_attention(qk, vs_t, l, swa_band, sink_row)
        o_f, o_b = _gdn(qkv_n, gb, gbt)
        xs = _merge(xs, y_na, y_swa, o_f, o_b, z, gate, l, gdn_gain, blockdiag, w_na, w_swa, w_gdn, w_o)
        xs = _ffn(xs, l, ffn_g, w_u, ffn_conv_w, ffn_b, w_d)
    return xs[None]
```

```python
import functools

import numpy as np
import jax
import jax.numpy as jnp
from jax import lax
from jax.experimental import pallas as pl
from jax.experimental.pallas import tpu as pltpu

F32 = jnp.float32
BF16 = jnp.bfloat16

D_MODEL = 1024
HEAD_DIM = 64
GRID_W = 64
NORM_EPS = 1e-6
NA_HEADS = 4
NA_KH = 8
NA_KW = 16
SWA_Q_HEADS = 8
SWA_KV_HEADS = 2
SWA_BLOCK = 128
ROPE_THETA = 10000.0
GDN_HEADS = 4
GDN_CHUNK = 64
D_FF = 2816

LANES = 128
NEG_BIG = -1e30
LOG2E = 1.4426950408889634
VMEM_LIMIT = 56 * 1024 * 1024

SEG_QK = 1152
SEG_PLAIN = 1408
PLAIN_VA = 768
PLAIN_Z = 1024
PLAIN_VS = 1280
SEG_GATE = 3072
SEG_BA = 128
IN_PACKED = SEG_QK + SEG_PLAIN + SEG_GATE + SEG_BA

TM = 512
TM_MERGE = 1024
NA_ROWS = 32
NA_WIN = NA_KH + 2
SWA_QB = 16
GDN_G = 4
GDN_SUB = 8
MXU_TILE = 256
FFN_SPLIT = (0, 6 * MXU_TILE, D_FF)
HALO = 8


def _cparams(sem):
    return pltpu.CompilerParams(dimension_semantics=sem, vmem_limit_bytes=VMEM_LIMIT)


def _dot(a, b):
    return jnp.dot(a, b, preferred_element_type=F32)


def _sigmoid(x):
    return 1.0 / (1.0 + jnp.exp(-x))


def _group_sum(sq, bd_ref):
    w = sq.shape[-1]
    return _dot(sq.astype(BF16), bd_ref[0:w, 0:w])


def _gdn_token_qkv(x, row_prev, row_next, cw_ref, bd_ref, qkv_ref):
    rid = lax.broadcasted_iota(jnp.int32, (TM, 1), 0)
    xprev = jnp.where(rid == 0, row_prev, pltpu.roll(x, 1, 0))
    xnext = jnp.where(rid == TM - 1, row_next, pltpu.roll(x, TM - 1, 0))
    y = xprev * cw_ref[0:1, :] + x * cw_ref[1:2, :] + xnext * cw_ref[2:3, :]
    y = y * _sigmoid(y)
    for part in range(3):
        yp = y[:, part * 256:(part + 1) * 256]
        if part < 2:
            yp = yp * lax.rsqrt(_group_sum(yp * yp, bd_ref) + NORM_EPS)
        if part == 0:
            yp = yp * (HEAD_DIM ** -0.5)
        yp = yp.astype(BF16)
        for hh in range(GDN_HEADS):
            qkv_ref[part * GDN_HEADS + hh] = yp[:, hh * HEAD_DIM:(hh + 1) * HEAD_DIM]


def _gdn_token_gates(ba, alog_ref, dtb_ref, gb_ref, gbt_ref):
    rid = lax.broadcasted_iota(jnp.int32, (TM, 1), 0)
    lane = lax.broadcasted_iota(jnp.int32, (1, LANES), 1)
    beta = _sigmoid(ba)
    sp_in = ba + dtb_ref[...]
    softplus = jnp.maximum(sp_in, 0.0) + jnp.log(1.0 + jnp.exp(-jnp.abs(sp_in)))
    g = jnp.where((lane >= 8) & (lane < 16), -jnp.exp(alog_ref[...]) * softplus, 0.0)
    rc = rid % GDN_CHUNK
    pre = g
    suf = g
    s = 1
    while s < GDN_CHUNK:
        pre = pre + jnp.where(rc >= s, pltpu.roll(pre, s, 0), 0.0)
        suf = suf + jnp.where(rc < GDN_CHUNK - s, pltpu.roll(suf, TM - s, 0), 0.0)
        s *= 2
    tot = pre + suf - g
    gc = jnp.where(lane < 12, pre, suf)
    slab = jnp.where(lane < 8, beta, jnp.where(lane < 16, gc, jnp.where(lane < 24, pltpu.roll(tot, 8, 1), 0.0)))
    gb_ref[...] = slab
    gbt_ref[...] = slab.T[0:24, :]


def _inproj_kernel(x_ref, xp_ref, xn_ref, g_ref, w_ref, gain_ref, rope_blk_ref, rope_off_ref, bd_ref,
                   cw_ref, alog_ref, dtb_ref,
                   oqk_ref, oz_ref, ogate_ref, ovat_ref, ovst_ref, oqkvh_ref, ogb_ref, ogbt_ref, *, nblk):
    i = pl.program_id(0)

    def normed(xv):
        ms = jnp.mean(xv * xv, axis=-1, keepdims=True)
        return xv * lax.rsqrt(ms + NORM_EPS) * g_ref[...]

    h = normed(x_ref[...]).astype(BF16)
    h_halo = jnp.concatenate([jnp.where(i > 0, normed(xp_ref[...]), 0.0),
                              jnp.where(i < nblk - 1, normed(xn_ref[...]), 0.0)], axis=0).astype(BF16)
    cw = 2 * LANES
    lane = lax.broadcasted_iota(jnp.int32, (1, cw), 1)
    first_half = (lane % HEAD_DIM) < (HEAD_DIM // 2)
    c_a, s_a = rope_blk_ref[0:1, :], rope_blk_ref[1:2, :]
    cos128 = c_a * rope_off_ref[0] - s_a * rope_off_ref[1]
    sin128 = s_a * rope_off_ref[2] + c_a * rope_off_ref[3]
    cos = jnp.concatenate([cos128, cos128], axis=1)
    sin = jnp.concatenate([sin128, sin128], axis=1)

    def head_norm(t, c):
        cols = slice(c * cw, min((c + 1) * cw, SEG_QK))
        w = cols.stop - cols.start
        tc = t[:, cols]
        ss = _group_sum(tc * tc, bd_ref)
        y = tc * lax.rsqrt(ss * (1.0 / HEAD_DIM) + NORM_EPS) * gain_ref[:, cols]
        if cols.start >= 512:
            rot = jnp.where(first_half[:, 0:w], pltpu.roll(y, w - HEAD_DIM // 2, 1), pltpu.roll(y, HEAD_DIM // 2, 1))
            y = y * cos[:, 0:w] + rot * sin[:, 0:w]
        oqk_ref[:, cols] = y.astype(BF16)

    o_plain, o_gate, o_ba = SEG_QK, SEG_QK + SEG_PLAIN, SEG_QK + SEG_PLAIN + SEG_GATE
    gate_cols = [slice(o_gate + c * D_MODEL, o_gate + (c + 1) * D_MODEL) for c in range(3)]
    out_cols = [slice(c * D_MODEL, (c + 1) * D_MODEL) for c in range(3)]
    t = _dot(h, w_ref[:, 0:SEG_QK])
    t_gdn = _dot(h, w_ref[:, o_plain:o_plain + 768])
    tba = _dot(h, w_ref[:, o_ba:o_ba + SEG_BA])
    t_halo = _dot(h_halo, w_ref[:, o_plain:o_plain + 768])
    for c in (0, 1):
        head_norm(t, c)
    tg0 = _dot(h, w_ref[:, gate_cols[0]])
    for c in (2, 3, 4):
        head_norm(t, c)
    tg1 = _dot(h, w_ref[:, gate_cols[1]])
    ogate_ref[:, out_cols[0]] = _sigmoid(tg0).astype(BF16)
    _gdn_token_qkv(t_gdn, t_halo[HALO - 1:HALO, :], t_halo[HALO:HALO + 1, :], cw_ref, bd_ref, oqkvh_ref)
    tg2 = _dot(h, w_ref[:, gate_cols[2]])
    ogate_ref[:, out_cols[1]] = _sigmoid(tg1).astype(BF16)
    _gdn_token_gates(tba, alog_ref, dtb_ref, ogb_ref, ogbt_ref)
    t_rest = _dot(h, w_ref[:, o_plain + PLAIN_VA:o_plain + SEG_PLAIN])
    ogate_ref[:, out_cols[2]] = _sigmoid(tg2).astype(BF16)
    oz_ref[...] = t_rest[:, PLAIN_Z - PLAIN_VA:PLAIN_Z - PLAIN_VA + 256].astype(BF16)
    for grp in range(TM // LANES):
        rows = slice(grp * LANES, (grp + 1) * LANES)
        ovat_ref[grp] = t_rest[rows, 0:256].T.astype(BF16)
        ovst_ref[grp] = t_rest[rows, PLAIN_VS - PLAIN_VA:PLAIN_VS - PLAIN_VA + 128].T.astype(BF16)


def _layer_spec(a, l, **kw):
    nd = a.ndim - 1
    return pl.BlockSpec((None,) + a.shape[1:], lambda *_: (l,) + (0,) * nd, **kw)


def _const_spec(a):
    return pl.BlockSpec(a.shape, lambda *_: (0,) * a.ndim)


def _inproj(x, l, g, w, gain, rope_blk, rope_off, bd, conv_w, alog_row, dtb_row):
    L = x.shape[0]
    nblk = L // TM
    hb = TM // HALO
    row = lambda w_: pl.BlockSpec((TM, w_), lambda i: (i, 0))
    return pl.pallas_call(
        functools.partial(_inproj_kernel, nblk=nblk),
        grid=(nblk,),
        in_specs=[row(D_MODEL),
                  pl.BlockSpec((HALO, D_MODEL), lambda i: (jnp.maximum(i * hb - 1, 0), 0)),
                  pl.BlockSpec((HALO, D_MODEL), lambda i: (jnp.minimum((i + 1) * hb, L // HALO - 1), 0)),
                  _layer_spec(g, l), _layer_spec(w, l, pipeline_mode=pl.Buffered(1)),
                  _layer_spec(gain, l), pl.BlockSpec((None, 2, LANES), lambda i: (i, 0, 0)), _const_spec(rope_off),
                  _const_spec(bd), _layer_spec(conv_w, l), _layer_spec(alog_row, l), _layer_spec(dtb_row, l)],
        out_specs=[row(SEG_QK), row(256), row(SEG_GATE),
                   pl.BlockSpec((TM // LANES, 256, LANES), lambda i: (i, 0, 0)),
                   pl.BlockSpec((TM // LANES, 128, LANES), lambda i: (i, 0, 0)),
                   pl.BlockSpec((3 * GDN_HEADS, TM, HEAD_DIM), lambda i: (0, i, 0)),
                   row(LANES),
                   pl.BlockSpec((24, TM), lambda i: (0, i))],
        out_shape=[jax.ShapeDtypeStruct((L, SEG_QK), BF16),
                   jax.ShapeDtypeStruct((L, 256), BF16),
                   jax.ShapeDtypeStruct((L, SEG_GATE), BF16),
                   jax.ShapeDtypeStruct((L // LANES, 256, LANES), BF16),
                   jax.ShapeDtypeStruct((L // LANES, 128, LANES), BF16),
                   jax.ShapeDtypeStruct((3 * GDN_HEADS, L, HEAD_DIM), BF16),
                   jax.ShapeDtypeStruct((L, LANES), F32),
                   jax.ShapeDtypeStruct((24, L), F32)],
        compiler_params=_cparams(("parallel",)),
        name="inproj",
    )(x, x, x, g, w, gain, rope_blk, rope_off, bd, conv_w, alog_row, dtb_row)


def _toeplitz_kernel(rpb_ref, onehot_ref, colmask_ref, o_ref):
    r = rpb_ref[...]
    hi = r.astype(BF16)
    r1 = r - hi.astype(F32)
    mid = r1.astype(BF16)
    lo = (r1 - mid.astype(F32)).astype(BF16)
    oh = onehot_ref[...]
    w = _dot(hi, oh) + _dot(mid, oh) + _dot(lo, oh)
    o_ref[...] = jnp.where(colmask_ref[...] > 0.0, w * LOG2E, NEG_BIG)


def _na_assemble_kernel(tiles_ref, o_ref, *, a_idx):
    neg = jnp.full((GRID_W, GRID_W), NEG_BIG, F32)
    for t in range(a_idx.shape[0]):
        for j in range(NA_WIN):
            for e in range(2):
                a = int(a_idx[t, j, e])
                o_ref[t, j * GRID_W:(j + 1) * GRID_W, e * GRID_W:(e + 1) * GRID_W] = neg if a < 0 else tiles_ref[a]


def _na_bias_tables(na_rpb, rows):
    depth, H, nr, nc = na_rpb.shape
    kc = np.arange(GRID_W)[:, None]
    qc = np.arange(GRID_W)[None, :]
    dc = np.clip(kc - qc + (NA_KW - 1), 0, 2 * NA_KW - 2).reshape(-1)
    onehot = np.zeros((32, GRID_W * GRID_W), np.float32)
    onehot[dc, np.arange(GRID_W * GRID_W)] = 1.0
    col_start = np.clip(qc - NA_KW // 2, 0, GRID_W - NA_KW)
    colmask = ((kc >= col_start) & (kc < col_start + NA_KW)).astype(np.float32).reshape(1, -1)
    rpb_rows = jnp.pad(na_rpb, ((0, 0), (0, 0), (0, 16 - nr), (0, 32 - nc))).reshape(depth * H * 16, 32)
    tiles = pl.pallas_call(
        _toeplitz_kernel,
        out_shape=jax.ShapeDtypeStruct((depth * H * 16, GRID_W * GRID_W), F32),
        name="na_bias_tiles",
    )(rpb_rows, jnp.asarray(onehot, BF16), jnp.asarray(colmask))
    tiles = tiles.reshape(depth * H, 16, GRID_W, GRID_W)
    a_idx = np.stack([_na_pair_structure(r0, rows)[1] for r0 in _na_pair_type_rows(rows)])
    nt = a_idx.shape[0]
    tabs = pl.pallas_call(
        functools.partial(_na_assemble_kernel, a_idx=a_idx),
        grid=(depth * H,),
        in_specs=[pl.BlockSpec((None, 16, GRID_W, GRID_W), lambda i: (i, 0, 0, 0))],
        out_specs=pl.BlockSpec((None, nt, NA_WIN * GRID_W, 2 * GRID_W), lambda i: (i, 0, 0, 0)),
        out_shape=jax.ShapeDtypeStruct((depth * H, nt, NA_WIN * GRID_W, 2 * GRID_W), F32),
        compiler_params=_cparams(("parallel",)),
        name="na_bias_tables",
    )(tiles)
    return tabs.reshape(depth, H, nt, NA_WIN * GRID_W, 2 * GRID_W)


def _na_pair_structure(r0, rows):
    wstart = int(np.clip(r0 - NA_KH // 2, 0, rows - NA_KH - 1)) // 2 * 2
    a_idx = np.full((NA_WIN, 2), -1, np.int64)
    for e in range(2):
        rr = r0 + e
        rs = int(np.clip(rr - NA_KH // 2, 0, rows - NA_KH))
        for j in range(NA_WIN):
            krow = wstart + j
            if rs <= krow < rs + NA_KH:
                a_idx[j, e] = krow - rr + NA_KH - 1
    return wstart, a_idx


def _na_pair_type_rows(rows):
    reps = [0, 2, 4, rows - 4, rows - 2]
    for r0 in range(0, rows, 2):
        t = 3 + (r0 - (rows - 4)) // 2 if r0 >= rows - 4 else min(r0 // 2, 2)
        ws, a = _na_pair_structure(r0, rows)
        ws_t, a_t = _na_pair_structure(reps[t], rows)
        assert (a == a_t).all() and r0 - ws == reps[t] - ws_t
    return reps


def _na_kernel(q_ref, kp_ref, kc_ref, kn_ref, vp_ref, vc_ref, vn_ref, bias_ref, o_ref, kbuf, vbuf, *, rows):
    i = pl.program_id(0)
    blk = NA_ROWS * GRID_W
    grp = blk // LANES
    kbuf[0:blk, :] = kp_ref[...]
    kbuf[blk:2 * blk, :] = kc_ref[...]
    kbuf[2 * blk:3 * blk, :] = kn_ref[...]
    vbuf[0:grp] = vp_ref[...]
    vbuf[grp:2 * grp] = vc_ref[...]
    vbuf[2 * grp:3 * grp] = vn_ref[...]
    win = NA_WIN * GRID_W
    wgrp = win // LANES
    pair = 2 * GRID_W

    npair = NA_ROWS // 2
    goffs, types = [], []
    for pp in range(npair):
        r0 = i * NA_ROWS + 2 * pp
        wstart = jnp.clip(r0 - NA_KH // 2, 0, rows - NA_KH - 1) // 2 * 2
        types.append(jnp.where(r0 >= rows - 4, 3 + (r0 - (rows - 4)) // 2, jnp.minimum(r0 // 2, 2)))
        goffs.append((wstart - (i - 1) * NA_ROWS) // 2)

    def logits(pp, h):
        sl = slice(h * HEAD_DIM, (h + 1) * HEAD_DIM)
        kw = kbuf[pl.ds(pl.multiple_of(goffs[pp] * LANES, LANES), win), sl]
        s = lax.dot_general(kw, q_ref[pp * pair:(pp + 1) * pair, sl],
                            (((1,), (1,)), ((), ())), preferred_element_type=F32)
        return s + bias_ref[h, types[pp]]

    def softmax(s):
        m = jnp.max(s, axis=0, keepdims=True)
        p = jnp.exp2(s - m)
        return p.astype(BF16), 1.0 / jnp.sum(p, axis=0, keepdims=True)

    def weighted(p_inv, pp, h):
        p, inv_l = p_inv
        vt = jnp.concatenate([vbuf[goffs[pp] + j, h * HEAD_DIM:(h + 1) * HEAD_DIM, :] for j in range(wgrp)], axis=-1)
        return _dot(vt, p) * inv_l

    tiles = [(pp, h) for pp in range(npair) for h in range(NA_HEADS)]
    s_q, p_q, outs = {}, {}, {}
    lead, lag = 2, 1
    for n in range(len(tiles) + lead + lag):
        if n < len(tiles):
            s_q[n] = logits(*tiles[n])
        if lead <= n < len(tiles) + lead:
            p_q[n - lead] = softmax(s_q.pop(n - lead))
        if n >= lead + lag:
            pp, h = tiles[n - lead - lag]
            outs[(pp, h)] = weighted(p_q.pop(n - lead - lag), pp, h)
            if h == NA_HEADS - 1:
                o_t = jnp.concatenate([outs.pop((pp, hh)) for hh in range(NA_HEADS)], axis=0)
                o_ref[pp * pair:(pp + 1) * pair, :] = o_t.T.astype(BF16)


def _na_attention(qk, va_t, l, bias):
    L = qk.shape[0]
    rows = L // GRID_W
    nblk = rows // NA_ROWS
    blk = NA_ROWS * GRID_W
    grp = blk // LANES
    w = NA_HEADS * HEAD_DIM
    prev = lambda i: jnp.maximum(i - 1, 0)
    nxt = lambda i: jnp.minimum(i + 1, nblk - 1)
    spec = lambda f, c: pl.BlockSpec((blk, w), lambda i: (f(i), c))
    vspec = lambda f: pl.BlockSpec((grp, w, LANES), lambda i: (f(i), 0, 0))
    same = lambda i: i
    return pl.pallas_call(
        functools.partial(_na_kernel, rows=rows),
        grid=(nblk,),
        in_specs=[spec(same, 0),
                  spec(prev, 1), spec(same, 1), spec(nxt, 1),
                  vspec(prev), vspec(same), vspec(nxt),
                  _layer_spec(bias, l)],
        out_specs=pl.BlockSpec((blk, w), lambda i: (i, 0)),
        out_shape=jax.ShapeDtypeStruct((L, w), BF16),
        scratch_shapes=[pltpu.VMEM((3 * blk, w), BF16), pltpu.VMEM((3 * grp, w, LANES), BF16)],
        compiler_params=_cparams(("parallel",)),
        name="na_attn",
    )(qk, qk, qk, qk, va_t, va_t, va_t, bias)


def _swa_kernel(q_ref, kp_ref, kc_ref, kn_ref, vp_ref, vc_ref, vn_ref, band_ref, sink_ref, o_ref, *, nstep):
    i = pl.program_id(0)
    B = SWA_BLOCK
    G = SWA_Q_HEADS // SWA_KV_HEADS
    k_all = jnp.concatenate([kp_ref[...], kc_ref[...], kn_ref[...]], axis=0)
    vt_all = [vp_ref[0]] + [vc_ref[j] for j in range(SWA_QB)] + [vn_ref[0]]

    def band_index(b):
        if b == 0:
            return jnp.where(i == 0, 1, 0)
        if b == SWA_QB - 1:
            return jnp.where(i == nstep - 1, 2, 0)
        return 0

    def logits(b, hp):
        g = hp // (G // 2)
        ks = k_all[b * B:(b + 3) * B, g * HEAD_DIM:(g + 1) * HEAD_DIM]
        qs = jnp.concatenate([q_ref[b * B:(b + 1) * B, (2 * hp + e) * HEAD_DIM:(2 * hp + e + 1) * HEAD_DIM]
                              for e in range(2)], axis=0)
        s = lax.dot_general(ks, qs, (((1,), (1,)), ((), ())), preferred_element_type=F32)
        t = band_index(b)
        return jnp.concatenate([s[0:B] + band_ref[t, 0:B], s[B:2 * B], s[2 * B:3 * B] + band_ref[t, 2 * B:3 * B]],
                               axis=0)

    def softmax(s, hp):
        sink = sink_ref[hp]
        m = jnp.maximum(jnp.max(s, axis=0, keepdims=True), sink)
        p = jnp.exp2(s - m)
        l = jnp.sum(p, axis=0, keepdims=True) + jnp.exp2(sink - m)
        return p.astype(BF16), 1.0 / l

    def weighted(p_inv, b, hp):
        p, inv_l = p_inv
        g = hp // (G // 2)
        vt = jnp.concatenate([vt_all[b + j][g * HEAD_DIM:(g + 1) * HEAD_DIM, :] for j in range(3)], axis=-1)
        return _dot(vt, p) * inv_l

    tiles = [(b, hp) for b in range(SWA_QB) for hp in range(SWA_Q_HEADS // 2)]
    s_q, p_q = {}, {}
    lead, lag = 2, 1
    for n in range(len(tiles) + lead + lag):
        if n < len(tiles):
            s_q[n] = logits(*tiles[n])
        if lead <= n < len(tiles) + lead:
            p_q[n - lead] = softmax(s_q.pop(n - lead), tiles[n - lead][1])
        if n >= lead + lag:
            b, hp = tiles[n - lead - lag]
            o_t = weighted(p_q.pop(n - lead - lag), b, hp)
            pair_t = jnp.concatenate([o_t[:, 0:B], o_t[:, B:2 * B]], axis=0)
            o_ref[b * B:(b + 1) * B, 2 * hp * HEAD_DIM:(2 * hp + 2) * HEAD_DIM] = pair_t.T.astype(BF16)


def _swa_band_tables():
    B = SWA_BLOCK
    qi = np.arange(2 * B)[None, :] % B
    kj = np.arange(3 * B)[:, None]
    band = np.abs(kj - B - qi) <= B
    tabs = [band, band & (kj >= B), band & (kj < 2 * B)]
    return jnp.asarray(np.where(np.stack(tabs), 0.0, NEG_BIG), F32)


def _swa_attention(qk, vs_t, l, band, sink_row):
    L = qk.shape[0]
    B = SWA_BLOCK
    T = SWA_QB * B
    nstep = L // T
    nblk = L // B
    prev = lambda i: jnp.maximum(i * SWA_QB - 1, 0)
    nxt = lambda i: jnp.minimum((i + 1) * SWA_QB, nblk - 1)
    return pl.pallas_call(
        functools.partial(_swa_kernel, nstep=nstep),
        grid=(nstep,),
        in_specs=[pl.BlockSpec((T, 512), lambda i: (i, 1)),
                  pl.BlockSpec((B, LANES), lambda i: (prev(i), 8)),
                  pl.BlockSpec((T, LANES), lambda i: (i, 8)),
                  pl.BlockSpec((B, LANES), lambda i: (nxt(i), 8)),
                  pl.BlockSpec((1, LANES, B), lambda i: (prev(i), 0, 0)),
                  pl.BlockSpec((SWA_QB, LANES, B), lambda i: (i, 0, 0)),
                  pl.BlockSpec((1, LANES, B), lambda i: (nxt(i), 0, 0)),
                  _const_spec(band), _layer_spec(sink_row, l)],
        out_specs=pl.BlockSpec((T, 512), lambda i: (i, 0)),
        out_shape=jax.ShapeDtypeStruct((L, 512), BF16),
        compiler_params=_cparams(("parallel",)),
        name="swa_attn",
    )(qk, qk, qk, qk, vs_t, vs_t, vs_t, band, sink_row)


def _bmm(a, b):
    return jnp.einsum('bij,bjk->bik', a.astype(BF16), b.astype(BF16), preferred_element_type=F32)


def _bmm_nt(a, b):
    return jnp.einsum('bid,bjd->bij', a.astype(BF16), b.astype(BF16), preferred_element_type=F32)


def _gdn_chunk_terms(qkv, gb, gbt, rev, out):
    G, C, H = GDN_G, GDN_CHUNK, GDN_HEADS

    def heads(part):
        return qkv[part * H:(part + 1) * H].reshape(H * G, C, HEAD_DIM)

    q = heads(0)
    k = heads(1)
    v = heads(2).astype(F32)
    kf = k.astype(F32)
    qf = q.astype(F32)
    d0 = H if rev else 0

    def colv(base):
        return jnp.concatenate([gb[:, base + h:base + h + 1].reshape(G, C, 1) for h in range(H)], axis=0)

    beta = jnp.broadcast_to(colv(d0), (H * G, C, HEAD_DIM))
    gcc = jnp.broadcast_to(colv(8 + d0), (H * G, C, HEAD_DIM))
    gl = jnp.broadcast_to(colv(16 + d0), (H * G, C, HEAD_DIM))
    grow = jnp.concatenate([gbt[8 + d0 + h:9 + d0 + h, c * C:(c + 1) * C].reshape(1, 1, C)
                            for h in range(H) for c in range(G)], axis=0)
    ii = lax.broadcasted_iota(jnp.int32, (1, C, C), 1)
    jj = lax.broadcasted_iota(jnp.int32, (1, C, C), 2)
    incl = (jj >= ii) if rev else (jj <= ii)
    strict = (jj > ii) if rev else (jj < ii)
    decay = jnp.exp(jnp.where(incl, gcc - grow, NEG_BIG))
    kk = _bmm_nt(k, k)
    qk = _bmm_nt(q, k)
    yield
    nmat = jnp.where(strict, kk * decay, 0.0) * beta
    eg = jnp.exp(gcc)
    rhs = jnp.concatenate([v * beta, kf * (beta * eg)], axis=-1)
    same_blk = (ii // GDN_SUB) == (jj // GDN_SUB)
    n_d = jnp.where(same_blk, nmat, 0.0)
    n_off = nmat - n_d
    t = jnp.where(ii == jj, 1.0, 0.0) - n_d
    p = _bmm(n_d, n_d)
    yield
    for _ in range(GDN_SUB.bit_length() - 3):
        m = _bmm(jnp.concatenate([t, p], axis=1), p)
        yield
        t = t + m[:, 0:C]
        p = m[:, C:2 * C]
    t = t + _bmm(t, p)
    yield
    m = _bmm(t, jnp.concatenate([rhs, n_off], axis=-1))
    yield
    y = m[:, :, 0:2 * HEAD_DIM]
    mm = m[:, :, 2 * HEAD_DIM:]
    m = _bmm(mm, jnp.concatenate([y, mm], axis=-1))
    yield
    x = y - m[:, :, 0:2 * HEAD_DIM]
    p = m[:, :, 2 * HEAD_DIM:]
    for _ in range((C // GDN_SUB).bit_length() - 3):
        m = _bmm(p, jnp.concatenate([x, p], axis=-1))
        yield
        x = x + m[:, :, 0:2 * HEAD_DIM]
        p = m[:, :, 2 * HEAD_DIM:]
    x = x + _bmm(p, x)
    yield
    u = x[:, :, 0:HEAD_DIM]
    w = x[:, :, HEAD_DIM:2 * HEAD_DIM]
    qkm = jnp.where(incl, qk * decay, 0.0)
    k_tail_t = jnp.swapaxes(kf * jnp.exp(gl - gcc), 1, 2)
    lhs1 = jnp.concatenate([w, qf * eg], axis=1).astype(BF16)
    lhs2 = jnp.concatenate([qkm, k_tail_t], axis=1).astype(BF16)
    dch = jnp.exp(gl[:, 0:8, :])
    out.extend([u, lhs1, lhs2, dch])


def _gdn_scan_group(u_ref, l1_ref, l2_ref, d_ref, s_ref, o_ref, rev):
    G, C, H = GDN_G, GDN_CHUNK, GDN_HEADS
    S = s_ref[...]
    for c in (range(G - 1, -1, -1) if rev else range(G)):
        r1 = _bmm(l1_ref[:, c], S)
        yield
        vn = u_ref[:, c] - r1[:, 0:C]
        r2 = _bmm(l2_ref[:, c], vn)
        yield
        oc = r1[:, C:2 * C] + r2[:, 0:C]
        S = S * d_ref[:, c][:, 0:1, :] + r2[:, C:2 * C]
        o_ref[c * C:(c + 1) * C, :] = jnp.concatenate([oc[h] for h in range(H)], axis=-1).astype(BF16)
    s_ref[...] = S


def _gdn_kernel(qkvf_ref, gbf_ref, gbtf_ref, qkvb_ref, gbb_ref, gbtb_ref, of_ref, ob_ref,
                sf_ref, sb_ref, uf_ref, l1f_ref, l2f_ref, df_ref, ub_ref, l1b_ref, l2b_ref, db_ref):
    G, C, H = GDN_G, GDN_CHUNK, GDN_HEADS
    step = pl.program_id(0)

    @pl.when(step == 0)
    def _():
        for r in (sf_ref, sb_ref, uf_ref, l1f_ref, l2f_ref, df_ref, ub_ref, l1b_ref, l2b_ref, db_ref):
            r[...] = jnp.zeros_like(r)

    terms_f, terms_b = [], []
    for strands in ([_gdn_chunk_terms(qkvf_ref[...], gbf_ref[...], gbtf_ref[...], False, terms_f),
                     _gdn_scan_group(uf_ref, l1f_ref, l2f_ref, df_ref, sf_ref, of_ref, False)],
                    [_gdn_chunk_terms(qkvb_ref[...], gbb_ref[...], gbtb_ref[...], True, terms_b),
                     _gdn_scan_group(ub_ref, l1b_ref, l2b_ref, db_ref, sb_ref, ob_ref, True)]):
        while strands:
            strands = [s for s in strands if next(s, True) is None]
    for (u, l1, l2, dch), (u_ref, l1_ref, l2_ref, d_ref) in ((terms_f, (uf_ref, l1f_ref, l2f_ref, df_ref)),
                                                          (terms_b, (ub_ref, l1b_ref, l2b_ref, db_ref))):
        u_ref[...] = u.reshape(H, G, C, HEAD_DIM)
        l1_ref[...] = l1.reshape(H, G, 2 * C, HEAD_DIM)
        l2_ref[...] = l2.reshape(H, G, 2 * C, C)
        d_ref[...] = dch.reshape(H, G, 8, HEAD_DIM)


def _gdn(qkv, gb, gbt):
    L = qkv.shape[1]
    G, C, H = GDN_G, GDN_CHUNK, GDN_HEADS
    T = G * C
    n = L // T
    fin = lambda s: jnp.minimum(s, n - 1)
    bin_ = lambda s: jnp.maximum(n - 1 - s, 0)
    fout = lambda s: jnp.maximum(s - 1, 0)
    bout = lambda s: jnp.minimum(n - s, n - 1)
    term_scratch = [pltpu.VMEM((H, G, C, HEAD_DIM), F32), pltpu.VMEM((H, G, 2 * C, HEAD_DIM), BF16),
                    pltpu.VMEM((H, G, 2 * C, C), BF16), pltpu.VMEM((H, G, 8, HEAD_DIM), F32)]
    return pl.pallas_call(
        _gdn_kernel,
        grid=(n + 1,),
        in_specs=[pl.BlockSpec((3 * H, T, HEAD_DIM), lambda s: (0, fin(s), 0)),
                  pl.BlockSpec((T, LANES), lambda s: (fin(s), 0)),
                  pl.BlockSpec((24, T), lambda s: (0, fin(s))),
                  pl.BlockSpec((3 * H, T, HEAD_DIM), lambda s: (0, bin_(s), 0)),
                  pl.BlockSpec((T, LANES), lambda s: (bin_(s), 0)),
                  pl.BlockSpec((24, T), lambda s: (0, bin_(s)))],
        out_specs=[pl.BlockSpec((T, 256), lambda s: (fout(s), 0)),
                   pl.BlockSpec((T, 256), lambda s: (bout(s), 0))],
        out_shape=[jax.ShapeDtypeStruct((L, 256), BF16), jax.ShapeDtypeStruct((L, 256), BF16)],
        scratch_shapes=[pltpu.VMEM((H, HEAD_DIM, HEAD_DIM), F32), pltpu.VMEM((H, HEAD_DIM, HEAD_DIM), F32)]
                       + term_scratch + term_scratch,
        compiler_params=_cparams(("arbitrary",)),
        name="gdn_scan",
    )(qkv, gb, gbt, qkv, gb, gbt)


def _merge_kernel(x_ref, yna_ref, yswa_ref, of_ref, ob_ref, z_ref, gate_ref, gn_ref, bd_ref,
                  wna_ref, wswa_ref, wgdn_ref, wout_ref, o_ref):
    o = of_ref[...].astype(F32) + ob_ref[...].astype(F32)
    z = z_ref[...].astype(F32)
    ms = _group_sum(o * o, bd_ref) * (1.0 / HEAD_DIM)
    ygdn = o * lax.rsqrt(ms + NORM_EPS) * gn_ref[...] * (z * _sigmoid(z))
    m = (gate_ref[:, 0:D_MODEL].astype(F32) * _dot(yna_ref[...], wna_ref[...])
         + gate_ref[:, D_MODEL:2 * D_MODEL].astype(F32) * _dot(yswa_ref[...], wswa_ref[...])
         + gate_ref[:, 2 * D_MODEL:3 * D_MODEL].astype(F32) * _dot(ygdn.astype(BF16), wgdn_ref[...]))
    o_ref[...] = x_ref[...] + _dot(m.astype(BF16), wout_ref[...])


def _merge(x, yna, yswa, of, ob, z, gate, l, gn, bd, wna, wswa, wgdn, wout):
    L = x.shape[0]
    row = lambda w_: pl.BlockSpec((TM_MERGE, w_), lambda i: (i, 0))
    once = pl.Buffered(1)
    return pl.pallas_call(
        _merge_kernel,
        grid=(L // TM_MERGE,),
        in_specs=[row(D_MODEL), row(256), row(512), row(256), row(256), row(256), row(SEG_GATE),
                  _layer_spec(gn, l), _const_spec(bd), _layer_spec(wna, l, pipeline_mode=once),
                  _layer_spec(wswa, l, pipeline_mode=once), _layer_spec(wgdn, l, pipeline_mode=once),
                  _layer_spec(wout, l, pipeline_mode=once)],
        out_specs=row(D_MODEL),
        out_shape=jax.ShapeDtypeStruct((L, D_MODEL), F32),
        compiler_params=_cparams(("parallel",)),
        name="merge",
    )(x, yna, yswa, of, ob, z, gate, gn, bd, wna, wswa, wgdn, wout)


def _ffn_kernel(x_ref, xp_ref, xn_ref, g_ref, wu_ref, cw_ref, cb_ref, wd_ref, o_ref, hbuf, *, nblk):
    i = pl.program_id(0)

    def normed(xv):
        ms = jnp.mean(xv * xv, axis=-1, keepdims=True)
        return xv * lax.rsqrt(ms + NORM_EPS) * g_ref[...]

    hbuf[0:HALO, :] = jnp.where(i > 0, normed(xp_ref[...]), 0.0).astype(BF16)
    hbuf[HALO:HALO + TM, :] = normed(x_ref[...]).astype(BF16)
    hbuf[HALO + TM:, :] = jnp.where(i < nblk - 1, normed(xn_ref[...]), 0.0).astype(BF16)
    h = hbuf[...]
    n = TM + 2 * HALO
    chunks = [slice(a, b) for a, b in zip(FFN_SPLIT[:-1], FFN_SPLIT[1:])]
    gate_cols = [slice(D_FF + a, D_FF + b) for a, b in zip(FFN_SPLIT[:-1], FFN_SPLIT[1:])]

    def conv(u, cols):
        y = (pltpu.roll(u, 1, 0) * cw_ref[0:1, cols] + u * cw_ref[1:2, cols]
             + pltpu.roll(u, n - 1, 0) * cw_ref[2:3, cols])
        return y[HALO:HALO + TM] + cb_ref[:, cols]

    ups = [(_dot(h, wu_ref[:, ca]), _dot(h, wu_ref[:, cb])) for ca, cb in zip(chunks, gate_cols)]
    acc = x_ref[...]
    for (ua, ub), ca, cb in zip(ups, chunks, gate_cols):
        a = conv(ua, ca)
        gated = (a * _sigmoid(a) * conv(ub, cb)).astype(BF16)
        acc = acc + _dot(gated, wd_ref[ca, :])
    o_ref[...] = acc


def _ffn(x, l, g, w_up, conv_w, conv_b, w_down):
    L = x.shape[0]
    nblk = L // TM
    hb = TM // HALO
    once = pl.Buffered(1)
    return pl.pallas_call(
        functools.partial(_ffn_kernel, nblk=nblk),
        grid=(nblk,),
        in_specs=[pl.BlockSpec((TM, D_MODEL), lambda i: (i, 0)),
                  pl.BlockSpec((HALO, D_MODEL), lambda i: (jnp.maximum(i * hb - 1, 0), 0)),
                  pl.BlockSpec((HALO, D_MODEL), lambda i: (jnp.minimum((i + 1) * hb, L // HALO - 1), 0)),
                  _layer_spec(g, l), _layer_spec(w_up, l, pipeline_mode=once), _layer_spec(conv_w, l),
                  _layer_spec(conv_b, l), _layer_spec(w_down, l, pipeline_mode=once)],
        out_specs=pl.BlockSpec((TM, D_MODEL), lambda i: (i, 0)),
        out_shape=jax.ShapeDtypeStruct((L, D_MODEL), F32),
        scratch_shapes=[pltpu.VMEM((TM + 2 * HALO, D_MODEL), BF16)],
        compiler_params=_cparams(("parallel",)),
        name="ffn",
    )(x, x, x, g, w_up, conv_w, conv_b, w_down)


PACK_SEGMENTS = ((0, 256, 0), (256, 512, 256), (768, 1280, 512), (1280, 1408, 1024),
                 (1536, 2304, 1152), (512, 768, 1920), (2304, 2560, 2176), (1408, 1536, 2432),
                 (2576, 5648, 2560), (2560, 2576, 5632))
PACK_K = 256
PACK_PIECE = 512


def _pack_kernel(w_ref, o_ref):
    for s0, s1, d0 in PACK_SEGMENTS[:-1]:
        for p0 in range(s0, s1, PACK_PIECE):
            p1 = min(p0 + PACK_PIECE, s1)
            o_ref[:, d0 + p0 - s0:d0 + p1 - s0] = w_ref[p0:p1, :].T.astype(BF16)
    s0, s1, d0 = PACK_SEGMENTS[-1]
    lane = lax.broadcasted_iota(jnp.int32, (1, LANES), 1)
    o_ref[:, d0:d0 + LANES] = jnp.where(lane < s1 - s0, w_ref[s0:s0 + LANES, :].T, 0.0).astype(BF16)


def _pack_w_in(w):
    depth, k, n = w.shape
    return pl.pallas_call(
        _pack_kernel,
        grid=(depth, k // PACK_K),
        in_specs=[pl.BlockSpec((None, n, PACK_K), lambda l, i: (l, 0, i))],
        out_specs=pl.BlockSpec((None, PACK_K, IN_PACKED), lambda l, i: (l, i, 0)),
        out_shape=jax.ShapeDtypeStruct((depth, k, IN_PACKED), BF16),
        compiler_params=_cparams(("parallel", "parallel")),
        name="pack_w_in",
    )(jnp.swapaxes(w, 1, 2))


def _rope_tables(L):
    inv = 1.0 / (ROPE_THETA ** (np.arange(0, HEAD_DIM, 2, dtype=np.float64) / HEAD_DIM))
    freq = np.tile(inv, LANES // inv.size)
    sign = np.where(np.arange(LANES) % HEAD_DIM < HEAD_DIM // 2, -1.0, 1.0)
    start = np.arange(L // TM, dtype=np.float64)[:, None] * TM * freq[None, :]
    off = np.arange(TM, dtype=np.float64)[:, None] * freq[None, :]
    blk = np.stack([np.cos(start), np.sin(start)], axis=1)
    offs = np.stack([np.cos(off), np.sin(off), sign * np.cos(off), sign * np.sin(off)])
    return jnp.asarray(blk, F32), jnp.asarray(offs, F32)


def kernel(x, attn_norm, w_in, qk_norm, na_rpb, swa_sink, gdn_conv_w, gdn_a_log, gdn_dt_bias, gdn_norm,
           w_branch_na, w_branch_swa, w_branch_gdn, w_out, ffn_norm, w_up, ffn_conv_w, ffn_conv_b, w_down):
    B, L, D = x.shape
    assert B == 1 and D == D_MODEL and L % (NA_ROWS * GRID_W) == 0 and L // GRID_W >= 2 * NA_ROWS
    depth = w_in.shape[0]
    rope_blk, rope_off = _rope_tables(L)
    blockdiag = jnp.asarray(np.kron(np.eye(4), np.ones((HEAD_DIM, HEAD_DIM))), BF16)
    bias_tabs = _na_bias_tables(na_rpb, L // GRID_W)
    swa_band = _swa_band_tables()
    scale = HEAD_DIM ** -0.5 * LOG2E
    G = SWA_Q_HEADS // SWA_KV_HEADS
    gain = jnp.concatenate([jnp.tile(qk_norm[:, 0] * scale, (1, NA_HEADS)), jnp.tile(qk_norm[:, 1], (1, NA_HEADS)),
                            jnp.tile(qk_norm[:, 2] * scale, (1, SWA_Q_HEADS)),
                            jnp.tile(qk_norm[:, 3], (1, SWA_KV_HEADS))], axis=1)[:, None, :]
    sink_row = jnp.repeat(swa_sink.reshape(depth, SWA_Q_HEADS // 2, 2) * LOG2E, SWA_BLOCK, axis=2)[:, :, None, :]
    alog_row = jnp.pad(gdn_a_log.reshape(depth, 1, 8), ((0, 0), (0, 0), (8, LANES - 16)))
    dtb_row = jnp.pad(gdn_dt_bias.reshape(depth, 1, 8), ((0, 0), (0, 0), (8, LANES - 16)))
    gdn_gain = jnp.tile(gdn_norm, (1, GDN_HEADS))[:, None, :]
    attn_g = attn_norm[:, None, :]
    ffn_g = ffn_norm[:, None, :]
    ffn_b = ffn_conv_b[:, None, :]
    w_in_p = _pack_w_in(w_in)
    w_na, w_swa, w_gdn = w_branch_na.astype(BF16), w_branch_swa.astype(BF16), w_branch_gdn.astype(BF16)
    w_o, w_u, w_d = w_out.astype(BF16), w_up.astype(BF16), w_down.astype(BF16)
    xs = x[0]
    for l in range(depth):
        qk, z, gate, va_t, vs_t, qkv_n, gb, gbt = _inproj(xs, l, attn_g, w_in_p, gain, rope_blk, rope_off, blockdiag,
                                                          gdn_conv_w, alog_row, dtb_row)
        y_na = _na_attention(qk, va_t, l, bias_tabs)
        y_swa = _swa_attention(qk, vs_t, l, swa_band, sink_row)
        o_f, o_b = _gdn(qkv_n, gb, gbt)
        xs = _merge(xs, y_na, y_swa, o_f, o_b, z, gate, l, gdn_gain, blockdiag, w_na, w_swa, w_gdn, w_o)
        xs = _ffn(xs, l, ffn_g, w_u, ffn_conv_w, ffn_b, w_d)
    return xs[None]
```
